```python
import jax
import jax.numpy as jnp
from jax import lax
import numpy as np

D_MODEL = 1024
BATCH = 2
SEQ = 8192
DEPTH = 2
DEC_BATCH = 32
DEC_SEQ = 1
PAST_LEN = 8192
PAGE_SIZE = 128

HEAD_DIM = 64
NSA_HEADS = 8
NSA_KV_HEADS = 2
NSA_GROUP = NSA_HEADS // NSA_KV_HEADS
NSA_BLOCK = 64
NSA_TOP_BLOCKS = 16
NSA_WINDOW = 512
NSA_FORCE_BONUS = 8.0
FOX_HEADS = 4
FOX_FORGET_BIAS = 3.0
DSA_HEADS = 4
DSA_IDX_HEADS = 4
DSA_IDX_DIM = 64
DSA_TOPK = 256
N_GROUPS = 4
EXPERTS_PER_GROUP = 4
N_EXPERTS = N_GROUPS * EXPERTS_PER_GROUP
TOP_K_IN_GROUP = 2
D_EXPERT = 512

ROPE_THETA = 10000.0
NORM_EPS = 1e-6
Q_BLOCK = 128
NEG_BIG = -1e30
IN_SIZES = (NSA_HEADS * HEAD_DIM, 6 * NSA_KV_HEADS * HEAD_DIM, 3 * NSA_HEADS,
            3 * FOX_HEADS * HEAD_DIM, FOX_HEADS,
            DSA_HEADS * HEAD_DIM, 2 * HEAD_DIM, DSA_IDX_HEADS * DSA_IDX_DIM, DSA_IDX_DIM, DSA_IDX_HEADS,
            3 * D_MODEL)
D_IN = (NSA_HEADS * HEAD_DIM + 6 * NSA_KV_HEADS * HEAD_DIM + 3 * NSA_HEADS
        + 3 * FOX_HEADS * HEAD_DIM + FOX_HEADS
        + DSA_HEADS * HEAD_DIM + 2 * HEAD_DIM + DSA_IDX_HEADS * DSA_IDX_DIM + DSA_IDX_DIM + DSA_IDX_HEADS
        + 3 * D_MODEL)

kernel_name = 'hybrid_nsa_fox_dsa_hmoe_step'


def rms_norm(x, g):
    xf = x.astype(jnp.float32)
    y = xf * lax.rsqrt(jnp.mean(jnp.square(xf), axis=-1, keepdims=True) + NORM_EPS)
    return (y * g.astype(jnp.float32)).astype(x.dtype)


def rope(x, pos):
    d = x.shape[-1]
    half = d // 2
    inv = ROPE_THETA ** (-jnp.arange(half, dtype=jnp.float32) * 2.0 / d)
    ang = pos.astype(jnp.float32)[:, None] * inv[None, :]
    shape = (1, pos.shape[0]) + (1,) * (x.ndim - 3) + (half,)
    cos = jnp.cos(ang).reshape(shape)
    sin = jnp.sin(ang).reshape(shape)
    xf = x.astype(jnp.float32)
    x1, x2 = xf[..., :half], xf[..., half:]
    return jnp.concatenate([x1 * cos - x2 * sin, x2 * cos + x1 * sin], axis=-1).astype(x.dtype)


def masked_softmax(s, mask):
    s = jnp.where(mask, s.astype(jnp.float32), NEG_BIG)
    m = jnp.max(s, axis=-1, keepdims=True)
    e = jnp.where(mask, jnp.exp(s - m), 0.0)
    d = jnp.sum(e, axis=-1, keepdims=True)
    return e / jnp.where(d > 0, d, 1.0)


def token_projections(xn, pos, w_in, nsa_qk_g, fox_qk_g, fox_b_f, dsa_qk_g):
    B, T, _ = xn.shape
    h = jnp.einsum('btd,dn->btn', xn, w_in)
    (nq, nkv, ngate, fqkv, ff, dq, dkv, diq, dik, diw, mg) = jnp.split(
        h, np.cumsum(IN_SIZES)[:-1].tolist(), axis=-1)
    q_nsa = rope(rms_norm(nq.reshape(B, T, NSA_HEADS, HEAD_DIM), nsa_qk_g[0]), pos)
    nkv = nkv.reshape(B, T, 3, 2, NSA_KV_HEADS, HEAD_DIM)
    k_nsa = rope(rms_norm(nkv[:, :, :, 0], nsa_qk_g[1:, None, :]), pos)
    kv_nsa = jnp.stack([k_nsa, nkv[:, :, :, 1]], axis=3)
    fqkv = fqkv.reshape(B, T, 3, FOX_HEADS, HEAD_DIM)
    q_fox = rms_norm(fqkv[:, :, 0], fox_qk_g[0])
    k_fox = rms_norm(fqkv[:, :, 1], fox_qk_g[1])
    logf = jax.nn.log_sigmoid((ff + fox_b_f).astype(jnp.float32))
    q_dsa = rope(rms_norm(dq.reshape(B, T, DSA_HEADS, HEAD_DIM), dsa_qk_g[0]), pos)
    dkv = dkv.reshape(B, T, 2, HEAD_DIM)
    k_dsa = rope(rms_norm(dkv[:, :, 0], dsa_qk_g[1]), pos)
    iq = rope(diq.reshape(B, T, DSA_IDX_HEADS, DSA_IDX_DIM), pos)
    ik = rope(dik, pos)
    return {
        'q_nsa': q_nsa,
        'nsa_rows': kv_nsa[:, :, :2].reshape(B, T, 4, NSA_KV_HEADS, HEAD_DIM),
        'win_rows': kv_nsa[:, :, 2],
        'g_nsa': jax.nn.sigmoid(ngate.reshape(B, T, 3, NSA_HEADS)),
        'q_fox': q_fox,
        'fox_rows': jnp.stack([k_fox, fqkv[:, :, 2]], axis=2),
        'logf': logf,
        'q_dsa': q_dsa,
        'dsa_rows': jnp.stack([k_dsa, dkv[:, :, 1], ik], axis=2),
        'iq': iq,
        'iw': diw * DSA_IDX_HEADS ** -0.5,
        'g_merge': jax.nn.sigmoid(mg.reshape(B, T, 3, D_MODEL)),
    }


def nsa_key_blocks(rows):
    B, L = rows.shape[:2]
    nb = -(-L // NSA_BLOCK)
    rows = jnp.pad(rows, ((0, 0), (0, nb * NSA_BLOCK - L), (0, 0), (0, 0), (0, 0)))
    rb = rows.reshape(B, nb, NSA_BLOCK, 4, NSA_KV_HEADS, HEAD_DIM)
    cmp = jnp.mean(rb[:, :, :, 0:2].astype(jnp.float32), axis=2).astype(rows.dtype)
    slc = jnp.moveaxis(rb[:, :, :, 2:4], 4, 1)
    return cmp[:, :, 0], cmp[:, :, 1], slc[..., 0, :], slc[..., 1, :]


def nsa_attend(q, qpos, gates, k_cmp, v_cmp, k_slc, v_slc, k_win, v_win, kpos_win):
    B, Tq = q.shape[:2]
    G, R, BL = NSA_KV_HEADS, NSA_GROUP, NSA_BLOCK
    nb = k_cmp.shape[1]
    scale = HEAD_DIM ** -0.5
    qg = q.reshape(B, Tq, G, R, HEAD_DIM)
    blk = jnp.arange(nb)
    s_c = jnp.einsum('btgrd,bjgd->btgrj', qg, k_cmp).astype(jnp.float32) * scale
    done = (blk[None, :] + 1) * BL <= qpos[:, None] + 1
    p_c = masked_softmax(s_c, done[None, :, None, None, :])
    o_c = jnp.einsum('btgrj,bjgd->btgrd', p_c.astype(v_cmp.dtype), v_cmp)
    cur = (qpos // BL)[:, None]
    forced = (blk[None, :] == 0) | (blk[None, :] == cur) | (blk[None, :] == cur - 1)
    imp = jnp.sum(p_c, axis=3)
    score = jnp.where(forced[None, :, None, :], imp + NSA_FORCE_BONUS, imp)
    score = jnp.where((blk[None, :] <= cur)[None, :, None, :], score, -1.0)
    _, sel = lax.top_k(score, min(NSA_TOP_BLOCKS, nb))
    n = sel.shape[-1]
    b_idx = jnp.arange(B)[:, None, None, None]
    g_idx = jnp.arange(G)[None, None, :, None]
    ks = k_slc[b_idx, g_idx, sel].reshape(B, Tq, G, n * BL, HEAD_DIM)
    vs = v_slc[b_idx, g_idx, sel].reshape(B, Tq, G, n * BL, HEAD_DIM)
    kpos_s = (sel[..., None] * BL + jnp.arange(BL)).reshape(B, Tq, G, n * BL)
    s_s = jnp.einsum('btgrd,btgmd->btgrm', qg, ks).astype(jnp.float32) * scale
    p_s = masked_softmax(s_s, (kpos_s <= qpos[None, :, None, None])[:, :, :, None, :])
    o_s = jnp.einsum('btgrm,btgmd->btgrd', p_s.astype(vs.dtype), vs)
    s_w = jnp.einsum('btgrd,bsgd->btgrs', qg, k_win).astype(jnp.float32) * scale
    dpos = qpos[:, None] - kpos_win[None, :]
    in_win = (dpos >= 0) & (dpos <= NSA_WINDOW) & (kpos_win[None, :] >= 0)
    p_w = masked_softmax(s_w, in_win[None, :, None, None, :])
    o_w = jnp.einsum('btgrs,bsgd->btgrd', p_w.astype(v_win.dtype), v_win)
    g = gates.reshape(B, Tq, 3, G, R, 1)
    o = g[:, :, 0] * o_c + g[:, :, 1] * o_s + g[:, :, 2] * o_w
    return o.reshape(B, Tq, NSA_HEADS * HEAD_DIM)


def fox_attend(q, qpos, cum_q, k, v, cum_k, kpos):
    s = jnp.einsum('bthd,bshd->bhts', q, k).astype(jnp.float32) * HEAD_DIM ** -0.5
    s = s + jnp.moveaxis(cum_q, 2, 1)[..., :, None] - jnp.moveaxis(cum_k, 2, 1)[..., None, :]
    p = masked_softmax(s, kpos[None, :] <= qpos[:, None])
    return jnp.einsum('bhts,bshd->bthd', p.astype(v.dtype), v)


def dsa_attend(q, iq, iw, qpos, k, v, ik, kpos, n_keep):
    B = q.shape[0]
    isc = jax.nn.relu(jnp.einsum('bthe,bse->btsh', iq, ik).astype(jnp.float32) * DSA_IDX_DIM ** -0.5)
    score = jnp.einsum('btsh,bth->bts', isc, iw.astype(jnp.float32))
    score = jnp.where((kpos[None, :] <= qpos[:, None])[None], score, -jnp.inf)
    _, idx = lax.top_k(score, n_keep)
    b_idx = jnp.arange(B)[:, None, None]
    kg = k[b_idx, idx]
    vg = v[b_idx, idx]
    s = jnp.einsum('bthd,btnd->bthn', q, kg).astype(jnp.float32) * HEAD_DIM ** -0.5
    valid = kpos[idx] <= qpos[None, :, None]
    p = masked_softmax(s, valid[:, :, None, :])
    return jnp.einsum('bthn,btnd->bthd', p.astype(vg.dtype), vg)


def nsa_prompt(q, gates, rows, win_rows):
    B, S = q.shape[:2]
    k_cmp, v_cmp, k_slc, v_slc = nsa_key_blocks(rows)
    win_pad = jnp.pad(win_rows, ((0, 0), (NSA_WINDOW, 0), (0, 0), (0, 0), (0, 0)))

    def body(i):
        start = i * Q_BLOCK
        qpos = start + jnp.arange(Q_BLOCK)
        qb = lax.dynamic_slice_in_dim(q, start, Q_BLOCK, 1)
        gb = lax.dynamic_slice_in_dim(gates, start, Q_BLOCK, 1)
        wb = lax.dynamic_slice_in_dim(win_pad, start, NSA_WINDOW + Q_BLOCK, 1)
        kpos_w = start - NSA_WINDOW + jnp.arange(NSA_WINDOW + Q_BLOCK)
        return nsa_attend(qb, qpos, gb, k_cmp, v_cmp, k_slc, v_slc, wb[:, :, 0], wb[:, :, 1], kpos_w)

    out = lax.map(body, jnp.arange(S // Q_BLOCK))
    return jnp.moveaxis(out, 0, 1).reshape(B, S, NSA_HEADS * HEAD_DIM)


def fox_prompt(q, rows, logf):
    B, S = q.shape[:2]
    cum = jnp.cumsum(logf, axis=1)
    kpos = jnp.arange(S)

    def body(i):
        start = i * Q_BLOCK
        qpos = start + jnp.arange(Q_BLOCK)
        qb = lax.dynamic_slice_in_dim(q, start, Q_BLOCK, 1)
        cq = lax.dynamic_slice_in_dim(cum, start, Q_BLOCK, 1)
        return fox_attend(qb, qpos, cq, rows[:, :, 0], rows[:, :, 1], cum, kpos)

    out = lax.map(body, jnp.arange(S // Q_BLOCK))
    return jnp.moveaxis(out, 0, 1).reshape(B, S, FOX_HEADS * HEAD_DIM)


def dsa_prompt(q, iq, iw, rows):
    B, S = q.shape[:2]
    kpos = jnp.arange(S)
    n_keep = min(DSA_TOPK, S // 4)

    def body(i):
        start = i * Q_BLOCK
        qpos = start + jnp.arange(Q_BLOCK)
        qb = lax.dynamic_slice_in_dim(q, start, Q_BLOCK, 1)
        iqb = lax.dynamic_slice_in_dim(iq, start, Q_BLOCK, 1)
        iwb = lax.dynamic_slice_in_dim(iw, start, Q_BLOCK, 1)
        return dsa_attend(qb, iqb, iwb, qpos, rows[:, :, 0], rows[:, :, 1], rows[:, :, 2], kpos, n_keep)

    out = lax.map(body, jnp.arange(S // Q_BLOCK))
    return jnp.moveaxis(out, 0, 1).reshape(B, S, DSA_HEADS * HEAD_DIM)


def gather_pages(cache_l, page_table):
    g = cache_l[page_table]
    return g.reshape((g.shape[0], g.shape[1] * g.shape[2]) + g.shape[3:])


def merge_branches(o_nsa, o_fox, o_dsa, g_merge, w_bn, w_bf, w_bd, w_o):
    y = (g_merge[:, :, 0] * (o_nsa @ w_bn)
         + g_merge[:, :, 1] * (o_fox @ w_bf)
         + g_merge[:, :, 2] * (o_dsa @ w_bd))
    return y @ w_o


def hier_moe(x, w_rg, b_rg, w_re, b_re, w_gate, w_up, w_down):
    B, T, D = x.shape
    xt = x.reshape(B * T, D)
    lg = (xt @ w_rg).astype(jnp.float32) + b_rg.astype(jnp.float32)
    grp = jnp.argmax(lg, axis=-1)
    g1 = jnp.max(jax.nn.softmax(lg, axis=-1), axis=-1)
    le = ((xt @ w_re).astype(jnp.float32) + b_re.astype(jnp.float32)).reshape(-1, N_GROUPS, EXPERTS_PER_GROUP)
    le = jnp.take_along_axis(le, grp[:, None, None], axis=1)[:, 0]
    top_l, top_i = lax.top_k(le, TOP_K_IN_GROUP)
    g2 = jax.nn.softmax(top_l, axis=-1)
    eid = grp[:, None] * EXPERTS_PER_GROUP + top_i
    comb = jnp.sum(jax.nn.one_hot(eid, N_EXPERTS, dtype=jnp.float32) * (g1[:, None] * g2)[..., None], axis=1)
    comb = comb.astype(x.dtype)
    y = jnp.zeros_like(xt)
    for e in range(N_EXPERTS):
        h = jax.nn.silu(xt @ w_gate[e]) * (xt @ w_up[e])
        y = y + comb[:, e:e + 1] * (h @ w_down[e])
    return y.reshape(B, T, D)


def setup_inputs(seed: int = 0) -> dict:
    key = jax.random.key(seed)
    k = jax.random.split(key, 26)

    def nrm(i, shape, scale=1.0):
        return scale * jax.random.normal(k[i], shape, jnp.float32)

    n_pages = PAST_LEN // PAGE_SIZE
    n_used = DEC_BATCH * n_pages
    n_phys = n_used + (n_used + 3) // 4
    win_keep = min(NSA_WINDOW, PAST_LEN)
    wa = NSA_HEADS * HEAD_DIM
    wf = FOX_HEADS * HEAD_DIM
    wd = DSA_HEADS * HEAD_DIM
    page_table = jax.random.permutation(k[7], n_phys)[:n_used].reshape(DEC_BATCH, n_pages).astype(jnp.int32)
    return {
        'x_prompt': nrm(0, (BATCH, SEQ, D_MODEL)),
        'x_sample': nrm(1, (DEC_BATCH, DEC_SEQ, D_MODEL)),
        'cache_nsa': nrm(2, (DEPTH, n_phys, PAGE_SIZE, 4, NSA_KV_HEADS, HEAD_DIM)),
        'cache_fox': nrm(3, (DEPTH, n_phys, PAGE_SIZE, 2, FOX_HEADS, HEAD_DIM)),
        'cache_fox_logf': jax.nn.log_sigmoid(FOX_FORGET_BIAS + nrm(4, (DEPTH, n_phys, PAGE_SIZE, FOX_HEADS))),
        'cache_dsa': nrm(5, (DEPTH, n_phys, PAGE_SIZE, 3, HEAD_DIM)),
        'cache_win': nrm(6, (DEPTH, DEC_BATCH, win_keep, 2, NSA_KV_HEADS, HEAD_DIM)),
        'page_table': page_table,
        'norm_attn_g': 1.0 + nrm(8, (DEPTH, D_MODEL), 0.02),
        'w_in': nrm(9, (DEPTH, D_MODEL, D_IN), D_MODEL ** -0.5),
        'nsa_qk_g': 1.0 + nrm(10, (DEPTH, 4, HEAD_DIM), 0.02),
        'fox_qk_g': 1.0 + nrm(11, (DEPTH, 2, HEAD_DIM), 0.02),
        'fox_b_f': FOX_FORGET_BIAS + nrm(12, (DEPTH, FOX_HEADS), 0.1),
        'dsa_qk_g': 1.0 + nrm(13, (DEPTH, 2, HEAD_DIM), 0.02),
        'w_branch_nsa': nrm(14, (DEPTH, wa, D_MODEL), wa ** -0.5),
        'w_branch_fox': nrm(15, (DEPTH, wf, D_MODEL), wf ** -0.5),
        'w_branch_dsa': nrm(16, (DEPTH, wd, D_MODEL), wd ** -0.5),
        'w_out': nrm(17, (DEPTH, D_MODEL, D_MODEL), D_MODEL ** -0.5),
        'norm_ffn_g': 1.0 + nrm(18, (DEPTH, D_MODEL), 0.02),
        'w_router_group': nrm(19, (DEPTH, D_MODEL, N_GROUPS), D_MODEL ** -0.5),
        'b_router_group': nrm(20, (DEPTH, N_GROUPS), 0.01),
        'w_router_expert': nrm(21, (DEPTH, D_MODEL, N_EXPERTS), D_MODEL ** -0.5),
        'b_router_expert': nrm(22, (DEPTH, N_EXPERTS), 0.01),
        'w_exp_gate': nrm(23, (DEPTH, N_EXPERTS, D_MODEL, D_EXPERT), D_MODEL ** -0.5),
        'w_exp_up': nrm(24, (DEPTH, N_EXPERTS, D_MODEL, D_EXPERT), D_MODEL ** -0.5),
        'w_exp_down': nrm(25, (DEPTH, N_EXPERTS, D_EXPERT, D_MODEL), D_EXPERT ** -0.5),
    }


def reference(x_prompt, x_sample, cache_nsa, cache_fox, cache_fox_logf, cache_dsa, cache_win, page_table,
              norm_attn_g, w_in, nsa_qk_g, fox_qk_g, fox_b_f, dsa_qk_g,
              w_branch_nsa, w_branch_fox, w_branch_dsa, w_out,
              norm_ffn_g, w_router_group, b_router_group, w_router_expert, b_router_expert,
              w_exp_gate, w_exp_up, w_exp_down):
    S = x_prompt.shape[1]
    T = x_sample.shape[1]
    past_len = page_table.shape[1] * PAGE_SIZE
    win_keep = cache_win.shape[2]
    pos_p = jnp.arange(S)
    pos_s = past_len + jnp.arange(T)
    kpos_s = jnp.arange(past_len + T)
    n_keep_s = min(DSA_TOPK, (past_len + T) // 4)
    xp, xs = x_prompt, x_sample
    nsa_p, nsa_s, fox_p, fox_s, logf_p, logf_s, dsa_p, dsa_s, win_p, win_s = ([] for _ in range(10))
    for l in range(DEPTH):
        hp = token_projections(rms_norm(xp, norm_attn_g[l]), pos_p, w_in[l], nsa_qk_g[l], fox_qk_g[l],
                               fox_b_f[l], dsa_qk_g[l])
        o_nsa = nsa_prompt(hp['q_nsa'], hp['g_nsa'], hp['nsa_rows'], hp['win_rows'])
        o_fox = fox_prompt(hp['q_fox'], hp['fox_rows'], hp['logf'])
        o_dsa = dsa_prompt(hp['q_dsa'], hp['iq'], hp['iw'], hp['dsa_rows'])
        xp = xp + merge_branches(o_nsa, o_fox, o_dsa, hp['g_merge'], w_branch_nsa[l], w_branch_fox[l],
                                 w_branch_dsa[l], w_out[l])
        xp = xp + hier_moe(rms_norm(xp, norm_ffn_g[l]), w_router_group[l], b_router_group[l],
                           w_router_expert[l], b_router_expert[l], w_exp_gate[l], w_exp_up[l], w_exp_down[l])
        nsa_p.append(hp['nsa_rows'])
        fox_p.append(hp['fox_rows'])
        logf_p.append(hp['logf'])
        dsa_p.append(hp['dsa_rows'])
        win_p.append(hp['win_rows'][:, -min(NSA_WINDOW, S):])
        hs = token_projections(rms_norm(xs, norm_attn_g[l]), pos_s, w_in[l], nsa_qk_g[l], fox_qk_g[l],
                               fox_b_f[l], dsa_qk_g[l])
        nsa_all = jnp.concatenate([gather_pages(cache_nsa[l], page_table), hs['nsa_rows']], axis=1)
        k_cmp, v_cmp, k_slc, v_slc = nsa_key_blocks(nsa_all)
        win_all = jnp.concatenate([cache_win[l], hs['win_rows']], axis=1)
        kpos_w = jnp.concatenate([past_len - win_keep + jnp.arange(win_keep), pos_s])
        o_nsa_s = nsa_attend(hs['q_nsa'], pos_s, hs['g_nsa'], k_cmp, v_cmp, k_slc, v_slc,
                             win_all[:, :, 0], win_all[:, :, 1], kpos_w)
        fox_all = jnp.concatenate([gather_pages(cache_fox[l], page_table), hs['fox_rows']], axis=1)
        logf_all = jnp.concatenate([gather_pages(cache_fox_logf[l], page_table).astype(jnp.float32),
                                    hs['logf']], axis=1)
        cum = jnp.cumsum(logf_all, axis=1)
        o_fox_s = fox_attend(hs['q_fox'], pos_s, cum[:, past_len:], fox_all[:, :, 0], fox_all[:, :, 1],
                             cum, kpos_s)
        dsa_all = jnp.concatenate([gather_pages(cache_dsa[l], page_table), hs['dsa_rows']], axis=1)
        o_dsa_s = dsa_attend(hs['q_dsa'], hs['iq'], hs['iw'], pos_s, dsa_all[:, :, 0], dsa_all[:, :, 1],
                             dsa_all[:, :, 2], kpos_s, n_keep_s)
        Bs = xs.shape[0]
        xs = xs + merge_branches(o_nsa_s, o_fox_s.reshape(Bs, T, FOX_HEADS * HEAD_DIM),
                                 o_dsa_s.reshape(Bs, T, DSA_HEADS * HEAD_DIM), hs['g_merge'],
                                 w_branch_nsa[l], w_branch_fox[l], w_branch_dsa[l], w_out[l])
        xs = xs + hier_moe(rms_norm(xs, norm_ffn_g[l]), w_router_group[l], b_router_group[l],
                           w_router_expert[l], b_router_expert[l], w_exp_gate[l], w_exp_up[l], w_exp_down[l])
        nsa_s.append(hs['nsa_rows'])
        fox_s.append(hs['fox_rows'])
        logf_s.append(hs['logf'])
        dsa_s.append(hs['dsa_rows'])
        win_s.append(win_all[:, -win_keep:])
    return (xp, xs,
            jnp.stack(nsa_p), jnp.stack(nsa_s),
            jnp.stack(fox_p), jnp.stack(fox_s),
            jnp.stack(logf_p), jnp.stack(logf_s),
            jnp.stack(dsa_p), jnp.stack(dsa_s),
            jnp.stack(win_p), jnp.stack(win_s))
```

```python
import functools

import numpy as np
import jax
import jax.numpy as jnp
from jax import lax
from jax.experimental import pallas as pl
from jax.experimental.pallas import tpu as pltpu

D_MODEL = 1024
HEAD_DIM = 64
HALF = HEAD_DIM // 2
NSA_HEADS = 8
NSA_KV_HEADS = 2
NSA_GROUP = NSA_HEADS // NSA_KV_HEADS
NSA_BLOCK = 64
NSA_TOP_BLOCKS = 16
NSA_WINDOW = 512
NSA_FORCE_BONUS = 8.0
FOX_HEADS = 4
DSA_HEADS = 4
DSA_IDX_HEADS = 4
DSA_TOPK = 256
N_GROUPS = 4
EXPERTS_PER_GROUP = 4
N_EXPERTS = N_GROUPS * EXPERTS_PER_GROUP
D_EXPERT = 512
ROPE_THETA = 10000.0
NORM_EPS = 1e-6
NEG_BIG = -1e30
PAGE_SIZE = 128
QK_SCALE = HEAD_DIM ** -0.5

LANES = 128
VMEM_LIMIT = 56 * 1024 * 1024

_O_NQ = 0
_O_NKV = _O_NQ + NSA_HEADS * HEAD_DIM
_O_NGATE = _O_NKV + 6 * NSA_KV_HEADS * HEAD_DIM
_O_FQKV = _O_NGATE + 3 * NSA_HEADS
_O_FF = _O_FQKV + 3 * FOX_HEADS * HEAD_DIM
_O_DQ = _O_FF + FOX_HEADS
_O_DKV = _O_DQ + DSA_HEADS * HEAD_DIM
_O_DIQ = _O_DKV + 2 * HEAD_DIM
_O_DIK = _O_DIQ + DSA_IDX_HEADS * HEAD_DIM
_O_DIW = _O_DIK + HEAD_DIM
_O_MG = _O_DIW + DSA_IDX_HEADS
D_IN = _O_MG + 3 * D_MODEL

_M_GATE = HEAD_DIM
_M_FF = _M_GATE + 3 * NSA_HEADS
_M_IW = _M_FF + FOX_HEADS
_M_END = _M_IW + DSA_IDX_HEADS

_C_NQ = 0
_C_NKV = 4
_C_FQ = 10
_C_FK = 12
_C_FV = 14
_C_DQI = 16
_C_DKI = 20
_C_MISC = 21
N_CHUNKS = 22
D_PROJ = N_CHUNKS * LANES


def _proj_column_permutation():
    idx = []
    for c in range(4):
        idx += list(range(_O_NQ + 64 * c, _O_NQ + 64 * c + 64))
        idx += list(range(_O_NQ + 64 * (c + 4), _O_NQ + 64 * (c + 4) + 64))
    idx += list(range(_O_NKV, _O_NKV + 768))
    idx += list(range(_O_FQKV, _O_FQKV + 768))
    for h in range(4):
        idx += list(range(_O_DQ + 64 * h, _O_DQ + 64 * h + 64))
        idx += list(range(_O_DIQ + 64 * h, _O_DIQ + 64 * h + 64))
    idx += list(range(_O_DKV, _O_DKV + 64)) + list(range(_O_DIK, _O_DIK + 64))
    idx += list(range(_O_DKV + 64, _O_DKV + 128))
    idx += list(range(_O_NGATE, _O_NGATE + 24)) + list(range(_O_FF, _O_FF + 4)) + list(range(_O_DIW, _O_DIW + 4))
    idx += [D_IN] * (LANES - _M_END)
    assert len(idx) == D_PROJ
    return np.asarray(idx, np.int32)


def _rope_tables(pos):
    inv = ROPE_THETA ** (-jnp.arange(HALF, dtype=jnp.float32) * 2.0 / HEAD_DIM)
    ang = pos.astype(jnp.float32)[:, None] * inv[None, :]
    cos, sin = jnp.cos(ang), jnp.sin(ang)
    cos = jnp.concatenate([cos, cos, cos, cos], axis=-1)
    sin = jnp.concatenate([-sin, sin, -sin, sin], axis=-1)
    return cos, sin


def _rms_rows(x, g):
    return x * lax.rsqrt(jnp.mean(jnp.square(x), axis=-1, keepdims=True) + NORM_EPS) * g


def _split_dot(a, b_bf16):
    hi = a.astype(jnp.bfloat16)
    lo = (a - hi.astype(jnp.float32)).astype(jnp.bfloat16)
    return (jnp.dot(hi, b_bf16, preferred_element_type=jnp.float32)
            + jnp.dot(lo, b_bf16, preferred_element_type=jnp.float32))


def _proj_kernel(x_ref, g_ref, w_ref, gain_ref, cos_ref, sin_ref, bd_ref, bf_ref, tri_ref,
                 qn_ref, nsa_rows_ref, win_rows_ref, slc_kv_ref, win_kv_ref, cmp_ref,
                 qf_ref, fox_rows_ref, fox_kv_ref, dqi_ref, dki_ref, dki_bf_ref,
                 misc_ref, misc_bf_ref, cum_ref, carry_ref, *, tiles_per_batch, with_blocks):
    i = pl.program_id(0)
    xn = _rms_rows(x_ref[...], g_ref[...]).astype(jnp.bfloat16)
    lane = lax.broadcasted_iota(jnp.int32, (1, LANES), 1)
    left = lane < HEAD_DIM
    first_half = (lane % HEAD_DIM) < HALF
    cos = cos_ref[...]
    sin = sin_ref[...]
    bd = bd_ref[...]

    def chunk(c):
        return jnp.dot(xn, w_ref[:, c * LANES:(c + 1) * LANES], preferred_element_type=jnp.float32)

    def head_norm(h, c, only_left=False):
        ms = _split_dot(h * h, bd)
        y = h * lax.rsqrt(ms + NORM_EPS) * gain_ref[:, c * LANES:(c + 1) * LANES]
        return jnp.where(left, y, h) if only_left else y

    def rope(h):
        swapped = jnp.where(first_half, pltpu.roll(h, LANES - HALF, 1), pltpu.roll(h, HALF, 1))
        return h * cos + swapped * sin

    for c in range(4):
        q = rope(head_norm(chunk(_C_NQ + c), _C_NQ + c)) * QK_SCALE
        qn_ref[:, c * LANES:(c + 1) * LANES] = q.astype(jnp.bfloat16)

    kc = rope(head_norm(chunk(_C_NKV + 0), _C_NKV + 0))
    vc = chunk(_C_NKV + 1)
    ks = rope(head_norm(chunk(_C_NKV + 2), _C_NKV + 2))
    vs = chunk(_C_NKV + 3)
    kw = rope(head_norm(chunk(_C_NKV + 4), _C_NKV + 4))
    vw = chunk(_C_NKV + 5)
    nsa_rows_ref[:, 0 * LANES:1 * LANES] = kc
    nsa_rows_ref[:, 1 * LANES:2 * LANES] = vc
    nsa_rows_ref[:, 2 * LANES:3 * LANES] = ks
    nsa_rows_ref[:, 3 * LANES:4 * LANES] = vs
    win_rows_ref[:, 0:LANES] = kw
    win_rows_ref[:, LANES:2 * LANES] = vw
    slc_kv_ref[:, 0:LANES] = ks.astype(jnp.bfloat16)
    slc_kv_ref[:, LANES:2 * LANES] = vs.astype(jnp.bfloat16)
    win_kv_ref[:, 0:LANES] = kw.astype(jnp.bfloat16)
    win_kv_ref[:, LANES:2 * LANES] = vw.astype(jnp.bfloat16)
    if with_blocks:
        tm = kc.shape[0]
        nblk = tm // NSA_BLOCK
        cmp_ref[:, 0:LANES] = jnp.mean(kc.reshape(nblk, NSA_BLOCK, LANES), axis=1)
        cmp_ref[:, LANES:2 * LANES] = jnp.mean(vc.reshape(nblk, NSA_BLOCK, LANES), axis=1)
    else:
        cmp_ref[...] = jnp.zeros_like(cmp_ref)

    for c in range(2):
        qf = head_norm(chunk(_C_FQ + c), _C_FQ + c) * QK_SCALE
        qf_ref[:, c * LANES:(c + 1) * LANES] = qf.astype(jnp.bfloat16)
        fk = head_norm(chunk(_C_FK + c), _C_FK + c)
        fv = chunk(_C_FV + c)
        fox_rows_ref[:, c * LANES:(c + 1) * LANES] = fk
        fox_rows_ref[:, (2 + c) * LANES:(3 + c) * LANES] = fv
        fox_kv_ref[:, c * LANES:(c + 1) * LANES] = fk.astype(jnp.bfloat16)
        fox_kv_ref[:, (2 + c) * LANES:(3 + c) * LANES] = fv.astype(jnp.bfloat16)

    for c in range(4):
        dqi = rope(head_norm(chunk(_C_DQI + c), _C_DQI + c, only_left=True)) * QK_SCALE
        dqi_ref[:, c * LANES:(c + 1) * LANES] = dqi.astype(jnp.bfloat16)

    dki = rope(head_norm(chunk(_C_DKI), _C_DKI, only_left=True))
    dki_ref[...] = dki
    dki_bf_ref[...] = dki.astype(jnp.bfloat16)

    m = chunk(_C_MISC)
    zf = m + bf_ref[...]
    logf = jnp.minimum(zf, 0.0) - jnp.log(1.0 + jnp.exp(-jnp.abs(zf)))
    is_gate = (lane >= _M_GATE) & (lane < _M_FF)
    is_ff = (lane >= _M_FF) & (lane < _M_IW)
    is_iw = (lane >= _M_IW) & (lane < _M_END)
    out = jnp.where(is_gate, jax.nn.sigmoid(m), m)
    out = jnp.where(is_ff, logf, out)
    out = jnp.where(is_iw, m * (DSA_IDX_HEADS ** -0.5), out)
    misc_ref[...] = out
    misc_bf_ref[...] = out.astype(jnp.bfloat16)

    @pl.when(i % tiles_per_batch == 0)
    def _():
        carry_ref[...] = jnp.zeros_like(carry_ref)

    lf = jnp.where(is_ff, logf, 0.0)
    cum = jnp.dot(tri_ref[...], lf, preferred_element_type=jnp.float32,
                  precision=lax.Precision.HIGHEST) + carry_ref[...]
    cum_ref[...] = cum
    carry_ref[...] = cum[cum.shape[0] - 1:, :]


def _project(x2d, norm_g, w_perm, gain_row, cos, sin, bf_row, *, tm, tiles_per_batch, with_blocks):
    n = x2d.shape[0]
    nt = n // tm
    ncmp = max(tm // NSA_BLOCK, 8) if not with_blocks else tm // NSA_BLOCK
    bd = jnp.asarray(np.kron(np.eye(2), np.full((HEAD_DIM, HEAD_DIM), 1.0 / HEAD_DIM)), jnp.bfloat16)
    tri = jnp.asarray(np.tril(np.ones((tm, tm))), jnp.float32)
    f32, bf16 = jnp.float32, jnp.bfloat16
    row = lambda w: pl.BlockSpec((tm, w), lambda i: (i, 0))
    full = lambda a: pl.BlockSpec(a.shape, lambda i: (0,) * a.ndim)
    pos_spec = pl.BlockSpec((tm, LANES), lambda i: (i % tiles_per_batch, 0))
    outs = [
        (512, bf16),
        (512, f32),
        (256, f32),
        (256, bf16),
        (256, bf16),
        None,
        (256, bf16),
        (512, f32),
        (512, bf16),
        (512, bf16),
        (128, f32),
        (128, bf16),
        (128, f32),
        (128, bf16),
        (128, f32),
    ]
    out_shape, out_specs = [], []
    for o in outs:
        if o is None:
            out_shape.append(jax.ShapeDtypeStruct((nt * ncmp, 2 * LANES), f32))
            out_specs.append(pl.BlockSpec((ncmp, 2 * LANES), lambda i: (i, 0)))
        else:
            out_shape.append(jax.ShapeDtypeStruct((n, o[0]), o[1]))
            out_specs.append(row(o[0]))
    return pl.pallas_call(
        functools.partial(_proj_kernel, tiles_per_batch=tiles_per_batch, with_blocks=with_blocks),
        grid=(nt,),
        in_specs=[row(D_MODEL), full(norm_g), full(w_perm), full(gain_row), pos_spec, pos_spec,
                  full(bd), full(bf_row), full(tri)],
        out_specs=out_specs,
        out_shape=out_shape,
        scratch_shapes=[pltpu.VMEM((1, LANES), f32)],
        compiler_params=pltpu.CompilerParams(dimension_semantics=("arbitrary",),
                                             vmem_limit_bytes=VMEM_LIMIT),
        name="proj",
    )(x2d, norm_g, w_perm, gain_row, cos, sin, bd, bf_row, tri)


def _layer_tables(w_in_l, nsa_qk_g_l, fox_qk_g_l, fox_b_f_l, dsa_qk_g_l):
    perm = _proj_column_permutation()
    w_ext = jnp.concatenate([w_in_l, jnp.zeros((D_MODEL, 1), w_in_l.dtype)], axis=1)
    w_perm = jnp.take(w_ext, perm, axis=1).astype(jnp.bfloat16)
    w_mg = w_in_l[:, _O_MG:].astype(jnp.bfloat16)
    one = jnp.ones((HEAD_DIM,), jnp.float32)
    two = lambda g: jnp.concatenate([g, g])
    ones2 = two(one)
    parts = [two(nsa_qk_g_l[0])] * 4
    parts += [two(nsa_qk_g_l[1]), ones2, two(nsa_qk_g_l[2]), ones2, two(nsa_qk_g_l[3]), ones2]
    parts += [two(fox_qk_g_l[0])] * 2 + [two(fox_qk_g_l[1])] * 2 + [ones2] * 2
    parts += [jnp.concatenate([dsa_qk_g_l[0], one])] * 4
    parts += [jnp.concatenate([dsa_qk_g_l[1], one]), ones2]
    gain_row = jnp.concatenate(parts).astype(jnp.float32)[None, :]
    bf_row = jnp.zeros((LANES,), jnp.float32).at[_M_FF:_M_IW].set(fox_b_f_l.astype(jnp.float32))[None, :]
    return w_perm, w_mg, gain_row, bf_row


_NT = (((1,), (1,)), ((), ()))


def _half_masks():
    lane = lax.broadcasted_iota(jnp.int32, (1, LANES), 1)
    return lane < HEAD_DIM, lane >= HEAD_DIM


def _online_update(s, valid, v, m_prev, l_prev, acc_prev):
    s = jnp.where(valid, s, NEG_BIG)
    m_new = jnp.maximum(m_prev, jnp.max(s, axis=-1, keepdims=True))
    p = jnp.where(valid, jnp.exp(s - m_new), 0.0)
    alpha = jnp.exp(m_prev - m_new)
    l_new = alpha * l_prev + jnp.sum(p, axis=-1, keepdims=True)
    pv = jnp.dot(p.reshape(-1, p.shape[-1]).astype(jnp.bfloat16), v, preferred_element_type=jnp.float32)
    acc_new = alpha * acc_prev + pv.reshape(acc_prev.shape)
    return m_new, l_new, acc_new


def _finish(l, acc):
    return acc / jnp.where(l > 0, l, 1.0)


def _fox_kernel(q_ref, kv_ref, cq_ref, ck_ref, o_ref, m_ref, l_ref, acc_ref, *, tq):
    i = pl.program_id(1)
    j = pl.program_id(2)
    left, right = _half_masks()

    @pl.when(j == 0)
    def _():
        m_ref[...] = jnp.full_like(m_ref, NEG_BIG)
        l_ref[...] = jnp.zeros_like(l_ref)
        acc_ref[...] = jnp.zeros_like(acc_ref)

    @pl.when(j <= i)
    def _():
        rows = i * tq + lax.broadcasted_iota(jnp.int32, (tq, 1), 0)
        cols = j * tq + lax.broadcasted_iota(jnp.int32, (1, tq), 1)
        valid = cols <= rows
        for h in range(FOX_HEADS):
            c = h // 2
            qc = q_ref[:, c * LANES:(c + 1) * LANES]
            qh = jnp.where(left if h % 2 == 0 else right, qc, jnp.zeros_like(qc))
            k = kv_ref[:, c * LANES:(c + 1) * LANES]
            v = kv_ref[:, (2 + c) * LANES:(3 + c) * LANES]
            s = lax.dot_general(qh, k, _NT, preferred_element_type=jnp.float32)
            s = s + cq_ref[:, _M_FF + h:_M_FF + h + 1] - ck_ref[h:h + 1, :]
            m_ref[h], l_ref[h], acc_ref[h] = _online_update(s, valid, v, m_ref[h], l_ref[h], acc_ref[h])

    @pl.when(j == pl.num_programs(2) - 1)
    def _():
        for c in range(FOX_HEADS // 2):
            o = jnp.where(left, _finish(l_ref[2 * c], acc_ref[2 * c]), _finish(l_ref[2 * c + 1], acc_ref[2 * c + 1]))
            o_ref[:, c * LANES:(c + 1) * LANES] = o.astype(o_ref.dtype)


def _fox_prompt(qf, fox_kv, cum, cum_t, *, batch, seq, tq):
    nq = seq // tq
    f32 = jnp.float32
    return pl.pallas_call(
        functools.partial(_fox_kernel, tq=tq),
        grid=(batch, nq, nq),
        in_specs=[pl.BlockSpec((tq, 2 * LANES), lambda b, i, j: (b * nq + i, 0)),
                  pl.BlockSpec((tq, 4 * LANES), lambda b, i, j: (b * nq + jnp.minimum(i, j), 0)),
                  pl.BlockSpec((tq, LANES), lambda b, i, j: (b * nq + i, 0)),
                  pl.BlockSpec((None, FOX_HEADS, tq), lambda b, i, j: (b, 0, jnp.minimum(i, j)))],
        out_specs=pl.BlockSpec((tq, 2 * LANES), lambda b, i, j: (b * nq + i, 0)),
        out_shape=jax.ShapeDtypeStruct((batch * seq, 2 * LANES), jnp.bfloat16),
        scratch_shapes=[pltpu.VMEM((FOX_HEADS, tq, 1), f32), pltpu.VMEM((FOX_HEADS, tq, 1), f32),
                        pltpu.VMEM((FOX_HEADS, tq, LANES), f32)],
        compiler_params=pltpu.CompilerParams(dimension_semantics=("arbitrary", "arbitrary", "arbitrary")),
        name="fox_prompt",
    )(qf, fox_kv, cum, cum_t)


def _select_top_blocks(score, n_top):
    nb = score.shape[-1]
    blk = lax.broadcasted_iota(jnp.int32, (1, nb), 1).astype(jnp.float32)

    def body(_, carry):
        sc, sel = carry
        mx = jnp.max(sc, axis=-1, keepdims=True)
        first = jnp.min(jnp.where(sc == mx, blk, float(nb)), axis=-1, keepdims=True)
        pick = blk == first
        return jnp.where(pick, -jnp.inf, sc), jnp.where(pick, 1.0, sel)

    _, sel = lax.fori_loop(0, n_top, body, (score, jnp.zeros_like(score)))
    return sel


def _block_importance_scores(p_sum, qpos, nb):
    blk = lax.broadcasted_iota(jnp.int32, (1, nb), 1)
    cur = qpos // NSA_BLOCK
    forced = (blk == 0) | (blk == cur) | (blk == cur - 1)
    score = jnp.where(forced, p_sum + NSA_FORCE_BONUS, p_sum)
    return jnp.where(blk <= cur, score, -1.0)


def _masked_softmax(s, valid):
    s = jnp.where(valid, s, NEG_BIG)
    m = jnp.max(s, axis=-1, keepdims=True)
    e = jnp.where(valid, jnp.exp(s - m), 0.0)
    d = jnp.sum(e, axis=-1, keepdims=True)
    return e / jnp.where(d > 0, d, 1.0)


def _nsa_kernel(q_ref, misc_ref, cmp_ref, slc_ref, win_ref, o_ref, m_sc, l_sc, acc_sc, *, tq, seq):
    i = pl.program_id(1)
    nb = seq // NSA_BLOCK
    g4 = NSA_GROUP
    left, right = _half_masks()
    qpos = i * tq + lax.broadcasted_iota(jnp.int32, (tq, 1), 0)
    blk = lax.broadcasted_iota(jnp.int32, (1, nb), 1)
    done = (blk + 1) * NSA_BLOCK <= qpos + 1
    kcm = cmp_ref[:, 0:LANES]
    vcm = cmp_ref[:, LANES:2 * LANES]
    lane_pos = lax.broadcasted_iota(jnp.int32, (1, tq), 1)
    n_top = min(NSA_TOP_BLOCKS, nb)
    outs = []
    for g in range(NSA_KV_HEADS):
        hm = left if g == 0 else right
        qs = [q_ref[:, c * LANES:(c + 1) * LANES] for c in range(g4)]
        qg = jnp.concatenate([jnp.where(hm, q, jnp.zeros_like(q)) for q in qs], axis=0)

        s_c = lax.dot_general(qg, kcm, _NT, preferred_element_type=jnp.float32).reshape(g4, tq, nb)
        p_c = _masked_softmax(s_c, done[None])
        o_c = jnp.dot(p_c.reshape(g4 * tq, nb).astype(jnp.bfloat16), vcm,
                      preferred_element_type=jnp.float32).reshape(g4, tq, LANES)
        sel = _select_top_blocks(_block_importance_scores(jnp.sum(p_c, axis=0), qpos, nb), n_top)
        sel = sel.astype(jnp.bfloat16)

        m_sc[...] = jnp.full_like(m_sc, NEG_BIG)
        l_sc[...] = jnp.zeros_like(l_sc)
        acc_sc[...] = jnp.zeros_like(acc_sc)

        def slc_body(c, _):
            start = pl.multiple_of(c * tq, tq)
            k = slc_ref[pl.ds(start, tq), 0:LANES]
            v = slc_ref[pl.ds(start, tq), LANES:2 * LANES]
            kpos = c * tq + lane_pos
            expand = (lax.broadcasted_iota(jnp.int32, (nb, 1), 0) == kpos // NSA_BLOCK)
            chosen = jnp.dot(sel, jnp.where(expand, 1.0, 0.0).astype(jnp.bfloat16),
                             preferred_element_type=jnp.float32) > 0.5
            valid = chosen & (kpos <= qpos)
            s = lax.dot_general(qg, k, _NT, preferred_element_type=jnp.float32).reshape(g4, tq, tq)
            m_sc[...], l_sc[...], acc_sc[...] = _online_update(s, valid[None], v, m_sc[...], l_sc[...], acc_sc[...])
            return 0

        lax.fori_loop(0, i + 1, slc_body, 0)
        o_s = _finish(l_sc[...], acc_sc[...])

        m_w = jnp.full((g4, tq, 1), NEG_BIG, jnp.float32)
        l_w = jnp.zeros((g4, tq, 1), jnp.float32)
        acc_w = jnp.zeros((g4, tq, LANES), jnp.float32)
        n_back = -(-NSA_WINDOW // tq)
        for w in range(n_back + 1):
            c = i - n_back + w
            start = pl.multiple_of(jnp.maximum(c, 0) * tq, tq)
            k = win_ref[pl.ds(start, tq), 0:LANES]
            v = win_ref[pl.ds(start, tq), LANES:2 * LANES]
            kpos = c * tq + lane_pos
            dpos = qpos - kpos
            valid = (dpos >= 0) & (dpos <= NSA_WINDOW) & (kpos >= 0)
            s = lax.dot_general(qg, k, _NT, preferred_element_type=jnp.float32).reshape(g4, tq, tq)
            m_w, l_w, acc_w = _online_update(s, valid[None], v, m_w, l_w, acc_w)
        o_w = _finish(l_w, acc_w)

        heads = []
        for c in range(g4):
            h = g * g4 + c
            gate = lambda br: misc_ref[:, _M_GATE + br * NSA_HEADS + h:_M_GATE + br * NSA_HEADS + h + 1]
            heads.append(gate(0) * o_c[c] + gate(1) * o_s[c] + gate(2) * o_w[c])
        outs.append(heads)
    for c in range(g4):
        o_ref[:, c * LANES:(c + 1) * LANES] = jnp.where(left, outs[0][c], outs[1][c]).astype(o_ref.dtype)


def _nsa_prompt(qn, misc, cmp_bf, slc_kv, win_kv, *, batch, seq, tq):
    nq = seq // tq
    nb = seq // NSA_BLOCK
    f32 = jnp.float32
    g4 = NSA_GROUP
    return pl.pallas_call(
        functools.partial(_nsa_kernel, tq=tq, seq=seq),
        grid=(batch, nq),
        in_specs=[pl.BlockSpec((tq, 4 * LANES), lambda b, i: (b * nq + i, 0)),
                  pl.BlockSpec((tq, LANES), lambda b, i: (b * nq + i, 0)),
                  pl.BlockSpec((nb, 2 * LANES), lambda b, i: (b, 0)),
                  pl.BlockSpec((seq, 2 * LANES), lambda b, i: (b, 0)),
                  pl.BlockSpec((seq, 2 * LANES), lambda b, i: (b, 0))],
        out_specs=pl.BlockSpec((tq, 4 * LANES), lambda b, i: (b * nq + i, 0)),
        out_shape=jax.ShapeDtypeStruct((batch * seq, 4 * LANES), jnp.bfloat16),
        scratch_shapes=[pltpu.VMEM((g4, tq, 1), f32), pltpu.VMEM((g4, tq, 1), f32),
                        pltpu.VMEM((g4, tq, LANES), f32)],
        compiler_params=pltpu.CompilerParams(dimension_semantics=("arbitrary", "arbitrary"),
                                             vmem_limit_bytes=VMEM_LIMIT),
        name="nsa_prompt",
    )(qn, misc, cmp_bf, slc_kv, win_kv)


_INT_MIN = -2 ** 31


def _sortable_key(x):
    bits = lax.bitcast_convert_type(x, jnp.int32)
    return jnp.where(bits < 0, bits ^ jnp.int32(0x7FFFFFFF), bits)


def _lane_fold(x):
    acc = x[:, 0:LANES]
    for c in range(1, x.shape[-1] // LANES):
        acc = acc + x[:, c * LANES:(c + 1) * LANES]
    return acc


def _dsa_kernel(dqi_ref, misc_ref, dki_ref, v_ref, o_ref, key_sc, tie_sc, m_sc, l_sc, acc_sc,
                *, tq, seq, n_keep):
    i = pl.program_id(1)
    nh = DSA_HEADS
    left, right = _half_masks()
    qpos = i * tq + lax.broadcasted_iota(jnp.int32, (tq, 1), 0)
    lane_pos = lax.broadcasted_iota(jnp.int32, (1, tq), 1)
    chunks = [dqi_ref[:, h * LANES:(h + 1) * LANES] for h in range(nh)]
    q_att = jnp.concatenate([jnp.where(left, q, jnp.zeros_like(q)) for q in chunks], axis=0)
    q_idx = jnp.concatenate([jnp.where(right, q, jnp.zeros_like(q)) for q in chunks], axis=0)
    n_chunks = i + 1

    def score_body(c, _):
        start = pl.multiple_of(c * tq, tq)
        kk = dki_ref[pl.ds(start, tq), :]
        a = lax.dot_general(q_idx, kk, _NT, preferred_element_type=jnp.float32).reshape(nh, tq, tq)
        a = jnp.maximum(a, 0.0)
        sc = a[0] * misc_ref[:, _M_IW:_M_IW + 1]
        for h in range(1, nh):
            sc = sc + a[h] * misc_ref[:, _M_IW + h:_M_IW + h + 1]
        kpos = c * tq + lane_pos
        sc = jnp.where(kpos <= qpos, sc, -jnp.inf)
        key_sc[c] = _sortable_key(sc)
        return 0

    lax.fori_loop(0, n_chunks, score_body, 0)

    def count(pred):
        def body(c, acc):
            return acc + _lane_fold(jnp.where(pred(c), 1.0, 0.0))
        part = lax.fori_loop(0, n_chunks, body, jnp.zeros((tq, LANES), jnp.float32))
        return jnp.sum(part, axis=-1, keepdims=True)

    def bit_body(it, lo):
        cand = lo + lax.shift_left(jnp.int32(1), jnp.int32(31) - it)
        cnt = count(lambda c: key_sc[c] >= cand)
        return jnp.where(cnt >= n_keep, cand, lo)

    thr = lax.fori_loop(0, 32, bit_body, jnp.full((tq, 1), _INT_MIN, jnp.int32))
    need = n_keep - count(lambda c: key_sc[c] > thr)

    big = jnp.int32(2 ** 30)

    def tie_body(c, _):
        kpos = c * tq + lane_pos
        tie_sc[c] = jnp.where(key_sc[c] == thr, kpos, big)
        return 0

    lax.fori_loop(0, n_chunks, tie_body, 0)
    idx_bits = max(1, (seq - 1).bit_length())

    def idx_body(it, bound):
        step = lax.shift_left(jnp.int32(1), jnp.int32(idx_bits - 1) - it)
        cand = bound + step - 1
        cnt = count(lambda c: tie_sc[c] <= cand)
        return jnp.where(cnt < need, bound + step, bound)

    bound = lax.fori_loop(0, idx_bits, idx_body, jnp.zeros((tq, 1), jnp.int32))

    m_sc[...] = jnp.full_like(m_sc, NEG_BIG)
    l_sc[...] = jnp.zeros_like(l_sc)
    acc_sc[...] = jnp.zeros_like(acc_sc)

    def att_body(c, _):
        start = pl.multiple_of(c * tq, tq)
        kk = dki_ref[pl.ds(start, tq), :]
        vv = v_ref[pl.ds(start, tq), :]
        kpos = c * tq + lane_pos
        valid = ((key_sc[c] > thr) | (tie_sc[c] <= bound)) & (kpos <= qpos)
        s = lax.dot_general(q_att, kk, _NT, preferred_element_type=jnp.float32).reshape(nh, tq, tq)
        m_sc[...], l_sc[...], acc_sc[...] = _online_update(s, valid[None], vv, m_sc[...], l_sc[...], acc_sc[...])
        return 0

    lax.fori_loop(0, n_chunks, att_body, 0)
    o = _finish(l_sc[...], acc_sc[...])
    for h in range(nh):
        o_ref[:, h * LANES:(h + 1) * LANES] = jnp.where(left, o[h], 0.0).astype(o_ref.dtype)


def _dsa_prompt(dqi, misc, dki_bf, misc_bf, *, batch, seq, tq, n_keep):
    nq = seq // tq
    f32 = jnp.float32
    nh = DSA_HEADS
    return pl.pallas_call(
        functools.partial(_dsa_kernel, tq=tq, seq=seq, n_keep=n_keep),
        grid=(batch, nq),
        in_specs=[pl.BlockSpec((tq, 4 * LANES), lambda b, i: (b * nq + i, 0)),
                  pl.BlockSpec((tq, LANES), lambda b, i: (b * nq + i, 0)),
                  pl.BlockSpec((seq, LANES), lambda b, i: (b, 0)),
                  pl.BlockSpec((seq, LANES), lambda b, i: (b, 0))],
        out_specs=pl.BlockSpec((tq, 4 * LANES), lambda b, i: (b * nq + i, 0)),
        out_shape=jax.ShapeDtypeStruct((batch * seq, 4 * LANES), jnp.bfloat16),
        scratch_shapes=[pltpu.VMEM((nq, tq, tq), jnp.int32), pltpu.VMEM((nq, tq, tq), jnp.int32),
                        pltpu.VMEM((nh, tq, 1), f32), pltpu.VMEM((nh, tq, 1), f32),
                        pltpu.VMEM((nh, tq, LANES), f32)],
        compiler_params=pltpu.CompilerParams(dimension_semantics=("arbitrary", "arbitrary"),
                                             vmem_limit_bytes=VMEM_LIMIT),
        name="dsa_prompt",
    )(dqi, misc, dki_bf, misc_bf)


def _merge_kernel(x_ref, g_ref, on_ref, of_ref, od_ref, wmg_ref, wbn_ref, wbf_ref, wbd_ref, wo_ref, out_ref):
    x = x_ref[...]
    xn = _rms_rows(x, g_ref[...]).astype(jnp.bfloat16)
    y = None
    for br, (o_ref, wb_ref) in enumerate(((on_ref, wbn_ref), (of_ref, wbf_ref), (od_ref, wbd_ref))):
        gate = jax.nn.sigmoid(jnp.dot(xn, wmg_ref[:, br * D_MODEL:(br + 1) * D_MODEL],
                                      preferred_element_type=jnp.float32))
        term = gate * jnp.dot(o_ref[...], wb_ref[...], preferred_element_type=jnp.float32)
        y = term if y is None else y + term
    out_ref[...] = x + jnp.dot(y.astype(jnp.bfloat16), wo_ref[...], preferred_element_type=jnp.float32)


def _merge(x2d, norm_g, o_nsa, o_fox, o_dsa, w_mg, w_bn, w_bf, w_bd, w_o, *, tm):
    n = x2d.shape[0]
    row = lambda a: pl.BlockSpec((tm, a.shape[1]), lambda i: (i, 0))
    full = lambda a: pl.BlockSpec(a.shape, lambda i: (0,) * a.ndim)
    args = (x2d, norm_g, o_nsa, o_fox, o_dsa, w_mg, w_bn, w_bf, w_bd, w_o)
    return pl.pallas_call(
        _merge_kernel,
        grid=(n // tm,),
        in_specs=[row(x2d), full(norm_g), row(o_nsa), row(o_fox), row(o_dsa)] + [full(a) for a in args[5:]],
        out_specs=row(x2d),
        out_shape=jax.ShapeDtypeStruct(x2d.shape, x2d.dtype),
        compiler_params=pltpu.CompilerParams(dimension_semantics=("arbitrary",), vmem_limit_bytes=VMEM_LIMIT),
        name="merge",
    )(*args)


_R_GROUP = N_EXPERTS


def _route(logits):
    lane = lax.broadcasted_iota(jnp.int32, (1, LANES), 1)
    lanef = lane.astype(jnp.float32)
    is_grp = (lane >= _R_GROUP) & (lane < _R_GROUP + N_GROUPS)
    lg = jnp.where(is_grp, logits, -jnp.inf)
    gmax = jnp.max(lg, axis=-1, keepdims=True)
    grp = jnp.min(jnp.where(lg == gmax, lanef, float(LANES)), axis=-1, keepdims=True) - _R_GROUP
    g1 = 1.0 / jnp.sum(jnp.where(is_grp, jnp.exp(lg - gmax), 0.0), axis=-1, keepdims=True)
    in_grp = (lane < N_EXPERTS) & ((lane // EXPERTS_PER_GROUP).astype(jnp.float32) == grp)
    le = jnp.where(in_grp, logits, -jnp.inf)
    m1 = jnp.max(le, axis=-1, keepdims=True)
    i1 = jnp.min(jnp.where(le == m1, lanef, float(LANES)), axis=-1, keepdims=True)
    le2 = jnp.where(lanef == i1, -jnp.inf, le)
    m2 = jnp.max(le2, axis=-1, keepdims=True)
    i2 = jnp.min(jnp.where(le2 == m2, lanef, float(LANES)), axis=-1, keepdims=True)
    t = jnp.exp(m2 - m1)
    w1 = g1 * (1.0 / (1.0 + t))
    w2 = g1 * (t / (1.0 + t))
    return jnp.where(lanef == i1, w1, jnp.where(lanef == i2, w2, 0.0))


def _moe_kernel(x_ref, g_ref, wr_ref, br_ref, wg_ref, wu_ref, wd_ref, out_ref, xn_sc, comb_sc, acc_sc):
    e = pl.program_id(1)

    @pl.when(e == 0)
    def _():
        x = x_ref[...]
        xn = _rms_rows(x, g_ref[...])
        logits = jnp.dot(xn, wr_ref[...], preferred_element_type=jnp.float32,
                         precision=lax.Precision.HIGHEST) + br_ref[...]
        comb_sc[...] = _route(logits)
        xn_sc[...] = xn.astype(jnp.bfloat16)
        acc_sc[...] = x

    xn = xn_sc[...]
    h = (jax.nn.silu(jnp.dot(xn, wg_ref[...], preferred_element_type=jnp.float32))
         * jnp.dot(xn, wu_ref[...], preferred_element_type=jnp.float32))
    y = jnp.dot(h.astype(jnp.bfloat16), wd_ref[...], preferred_element_type=jnp.float32)
    lane = lax.broadcasted_iota(jnp.int32, (1, LANES), 1)
    ce = jnp.sum(jnp.where(lane == e, comb_sc[...], 0.0), axis=-1, keepdims=True)
    acc_sc[...] += ce * y

    @pl.when(e == pl.num_programs(1) - 1)
    def _():
        out_ref[...] = acc_sc[...]


def _moe(x2d, norm_g, w_router, b_router, w_gate, w_up, w_down, *, tm):
    n = x2d.shape[0]
    f32 = jnp.float32
    return pl.pallas_call(
        _moe_kernel,
        grid=(n // tm, N_EXPERTS),
        in_specs=[pl.BlockSpec((tm, D_MODEL), lambda i, e: (i, 0)),
                  pl.BlockSpec((1, D_MODEL), lambda i, e: (0, 0)),
                  pl.BlockSpec((D_MODEL, LANES), lambda i, e: (0, 0)),
                  pl.BlockSpec((1, LANES), lambda i, e: (0, 0)),
                  pl.BlockSpec((None, D_MODEL, D_EXPERT), lambda i, e: (e, 0, 0)),
                  pl.BlockSpec((None, D_MODEL, D_EXPERT), lambda i, e: (e, 0, 0)),
                  pl.BlockSpec((None, D_EXPERT, D_MODEL), lambda i, e: (e, 0, 0))],
        out_specs=pl.BlockSpec((tm, D_MODEL), lambda i, e: (i, 0)),
        out_shape=jax.ShapeDtypeStruct(x2d.shape, x2d.dtype),
        scratch_shapes=[pltpu.VMEM((tm, D_MODEL), jnp.bfloat16), pltpu.VMEM((tm, LANES), f32),
                        pltpu.VMEM((tm, D_MODEL), f32)],
        compiler_params=pltpu.CompilerParams(dimension_semantics=("arbitrary", "arbitrary"),
                                             vmem_limit_bytes=VMEM_LIMIT),
        name="moe",
    )(x2d, norm_g, w_router, b_router, w_gate, w_up, w_down)


def _new_page(row, dtype):
    first = lax.broadcasted_iota(jnp.int32, (PAGE_SIZE, 1), 0) == 0
    return jnp.where(first, row, 0.0).astype(dtype)


def _nsa_sample_kernel(pt_ref, qbd_ref, gate_ref, new_ref, neww_ref, cw_ref, *rest, pg, n_pages):
    page_refs = rest[:pg]
    o_ref, ks_sc, vs_sc, cmpk_sc, cmpv_sc = rest[pg:]
    s = pl.program_id(1)
    n_tok = n_pages * PAGE_SIZE
    nb_s = n_tok // NSA_BLOCK + 1
    nbpad = cmpk_sc.shape[0]
    l_pad = ks_sc.shape[0]
    bpp = PAGE_SIZE // NSA_BLOCK
    bf16 = jnp.bfloat16

    for k in range(pg):
        page = page_refs[k][...]
        p = s * pg + k
        row0 = pl.multiple_of(p * PAGE_SIZE, PAGE_SIZE)
        ks_sc[pl.ds(row0, PAGE_SIZE), :] = page[:, 2 * LANES:3 * LANES].astype(bf16)
        vs_sc[pl.ds(row0, PAGE_SIZE), :] = page[:, 3 * LANES:4 * LANES].astype(bf16)
        cmpk_sc[pl.ds(p * bpp, bpp), :] = jnp.mean(page[:, 0:LANES].reshape(bpp, NSA_BLOCK, LANES), axis=1)
        cmpv_sc[pl.ds(p * bpp, bpp), :] = jnp.mean(page[:, LANES:2 * LANES].reshape(bpp, NSA_BLOCK, LANES), axis=1)

    @pl.when(s == pl.num_programs(1) - 1)
    def _():
        new = new_ref[...]
        ks_sc[n_tok:l_pad, :] = _new_page(new[:, 2 * LANES:3 * LANES], bf16)
        vs_sc[n_tok:l_pad, :] = _new_page(new[:, 3 * LANES:4 * LANES], bf16)
        tail = nbpad - (nb_s - 1)
        first = lax.broadcasted_iota(jnp.int32, (tail, 1), 0) == 0
        cmpk_sc[nb_s - 1:nbpad, :] = jnp.where(first, new[:, 0:LANES] * (1.0 / NSA_BLOCK), 0.0)
        cmpv_sc[nb_s - 1:nbpad, :] = jnp.where(first, new[:, LANES:2 * LANES] * (1.0 / NSA_BLOCK), 0.0)

        qbd = qbd_ref[...]
        nh = qbd.shape[0]
        qpos = n_tok
        blk = lax.broadcasted_iota(jnp.int32, (1, nbpad), 1)
        done = ((blk + 1) * NSA_BLOCK <= qpos + 1) & (blk < nb_s)
        s_c = lax.dot_general(qbd, cmpk_sc[...].astype(bf16), _NT, preferred_element_type=jnp.float32)
        p_c = _masked_softmax(s_c, done)
        o_c = jnp.dot(p_c.astype(bf16), cmpv_sc[...].astype(bf16), preferred_element_type=jnp.float32)
        imp = jnp.sum(p_c.reshape(NSA_KV_HEADS, NSA_GROUP, nbpad), axis=1)
        qpos_col = jnp.full((NSA_KV_HEADS, 1), qpos, jnp.int32)
        score = _block_importance_scores(imp, qpos_col, nbpad)
        score = jnp.where(blk < nb_s, score, -2.0)
        sel = _select_top_blocks(score, min(NSA_TOP_BLOCKS, nb_s))
        sel8 = jnp.concatenate([jnp.broadcast_to(sel[g:g + 1], (NSA_GROUP, nbpad)) for g in range(NSA_KV_HEADS)],
                               axis=0).astype(bf16)
        kpos = lax.broadcasted_iota(jnp.int32, (1, l_pad), 1)
        expand = lax.broadcasted_iota(jnp.int32, (nbpad, 1), 0) == kpos // NSA_BLOCK
        chosen = jnp.dot(sel8, jnp.where(expand, 1.0, 0.0).astype(bf16), preferred_element_type=jnp.float32) > 0.5
        s_s = lax.dot_general(qbd, ks_sc[...], _NT, preferred_element_type=jnp.float32)
        p_s = _masked_softmax(s_s, chosen & (kpos <= qpos))
        o_s = jnp.dot(p_s.astype(bf16), vs_sc[...], preferred_element_type=jnp.float32)

        cw = cw_ref[...].astype(bf16)
        nw = neww_ref[...].astype(bf16).astype(jnp.float32)
        s1 = lax.dot_general(qbd, cw[:, 0:LANES], _NT, preferred_element_type=jnp.float32)
        s2 = jnp.sum(qbd.astype(jnp.float32) * nw[:, 0:LANES], axis=-1, keepdims=True)
        m = jnp.maximum(jnp.max(s1, axis=-1, keepdims=True), s2)
        e1 = jnp.exp(s1 - m)
        e2 = jnp.exp(s2 - m)
        d = jnp.sum(e1, axis=-1, keepdims=True) + e2
        p1 = e1 / d
        p2 = (e2 / d).astype(bf16).astype(jnp.float32)
        o_w = jnp.dot(p1.astype(bf16), cw[:, LANES:2 * LANES], preferred_element_type=jnp.float32) + p2 * nw[:, LANES:2 * LANES]

        g = gate_ref[...]
        o_ref[...] = g[:, 0:1] * o_c + g[:, 1:2] * o_s + g[:, 2:3] * o_w


def _page_specs(cache, layer, pg, width):
    return [pl.BlockSpec((None, None, PAGE_SIZE, width),
                         functools.partial(lambda b, s, pt, k: (layer, pt[b, s * pg + k], 0, 0), k=k))
            for k in range(pg)]


def _seq_spec(shape):
    nd = len(shape)
    return pl.BlockSpec((None,) + tuple(shape[1:]), lambda b, s, pt: (b,) + (0,) * (nd - 1))


def _nsa_sample(page_table, qbd, gates, new_rows, new_win, cache_win_l, cache, layer, *, pg):
    bs, n_pages = page_table.shape
    n_tok = n_pages * PAGE_SIZE
    l_pad = n_tok + PAGE_SIZE
    nbpad = -(-(n_tok // NSA_BLOCK + 1) // 8) * 8
    f32, bf16 = jnp.float32, jnp.bfloat16
    fixed = (qbd, gates, new_rows, new_win, cache_win_l)
    grid_spec = pltpu.PrefetchScalarGridSpec(
        num_scalar_prefetch=1,
        grid=(bs, n_pages // pg),
        in_specs=[_seq_spec(a.shape) for a in fixed] + _page_specs(cache, layer, pg, 4 * LANES),
        out_specs=_seq_spec((bs, NSA_HEADS, LANES)),
        scratch_shapes=[pltpu.VMEM((l_pad, LANES), bf16), pltpu.VMEM((l_pad, LANES), bf16),
                        pltpu.VMEM((nbpad, LANES), f32), pltpu.VMEM((nbpad, LANES), f32)])
    return pl.pallas_call(
        functools.partial(_nsa_sample_kernel, pg=pg, n_pages=n_pages),
        grid_spec=grid_spec,
        out_shape=jax.ShapeDtypeStruct((bs, NSA_HEADS, LANES), f32),
        compiler_params=pltpu.CompilerParams(dimension_semantics=("arbitrary", "arbitrary"),
                                             vmem_limit_bytes=VMEM_LIMIT),
        name="nsa_sample",
    )(page_table, *fixed, *([cache] * pg))


def _fox_sample_kernel(pt_ref, qbd_ref, new_ref, newlf_ref, du_ref, *rest, pg, n_pages):
    page_refs = rest[:pg]
    lf_refs = rest[pg:2 * pg]
    o_ref, k_sc, v_sc, lf_sc, cum_sc, bias_sc = rest[2 * pg:]
    s = pl.program_id(1)
    n_tok = n_pages * PAGE_SIZE
    l_pad = k_sc.shape[0]
    bf16 = jnp.bfloat16
    hi = lax.Precision.HIGHEST
    w = FOX_HEADS * HEAD_DIM

    for k in range(pg):
        page = page_refs[k][...]
        p = s * pg + k
        row0 = pl.multiple_of(p * PAGE_SIZE, PAGE_SIZE)
        k_sc[pl.ds(row0, PAGE_SIZE), :] = page[:, 0:w].astype(bf16)
        v_sc[pl.ds(row0, PAGE_SIZE), :] = page[:, w:2 * w].astype(bf16)
        lf_sc[pl.ds(p, 1), :] = lf_refs[k][...]

    @pl.when(s == pl.num_programs(1) - 1)
    def _():
        new = new_ref[...]
        k_sc[n_tok:l_pad, :] = _new_page(new[:, 0:w], bf16)
        v_sc[n_tok:l_pad, :] = _new_page(new[:, w:2 * w], bf16)
        lf = lf_sc[...]
        rowi = lax.broadcasted_iota(jnp.int32, (n_pages, n_pages), 0)
        coli = lax.broadcasted_iota(jnp.int32, (n_pages, n_pages), 1)
        strict_lower = jnp.where(coli < rowi, 1.0, 0.0)
        kpos = lax.broadcasted_iota(jnp.int32, (1, l_pad), 1)
        for h in range(FOX_HEADS):
            within = jnp.dot(lf, du_ref[h], preferred_element_type=jnp.float32, precision=hi)
            before = jnp.dot(strict_lower, within, preferred_element_type=jnp.float32, precision=hi)
            cum = within + before[:, PAGE_SIZE - 1:PAGE_SIZE]
            cum_sc[...] = cum
            total = cum[n_pages - 1:n_pages, PAGE_SIZE - 1:PAGE_SIZE]
            cq = total + newlf_ref[:, h:h + 1]
            for p in range(n_pages):
                bias_sc[h:h + 1, p * PAGE_SIZE:(p + 1) * PAGE_SIZE] = cq - cum_sc[p:p + 1, :]
            bias_sc[h:h + 1, n_tok:l_pad] = jnp.zeros((1, l_pad - n_tok), jnp.float32)
        qbd = qbd_ref[...]
        sc = lax.dot_general(qbd, k_sc[...], _NT, preferred_element_type=jnp.float32) + bias_sc[...]
        pr = _masked_softmax(sc, kpos <= n_tok)
        o_ref[...] = jnp.dot(pr.astype(bf16), v_sc[...], preferred_element_type=jnp.float32)


def _fox_sample(page_table, qbd, new_rows, new_lf, cache, cache_lf, layer, *, pg):
    bs, n_pages = page_table.shape
    n_tok = n_pages * PAGE_SIZE
    l_pad = n_tok + PAGE_SIZE
    w = FOX_HEADS * HEAD_DIM
    f32, bf16 = jnp.float32, jnp.bfloat16
    lane = np.arange(PAGE_SIZE * FOX_HEADS)
    du = np.stack([((lane % FOX_HEADS == h)[:, None] & ((lane // FOX_HEADS)[:, None] <= np.arange(PAGE_SIZE)[None, :]))
                   for h in range(FOX_HEADS)]).astype(np.float32)
    du = jnp.asarray(du)
    fixed = (qbd, new_rows, new_lf)
    lf_specs = [pl.BlockSpec((None, None, 1, PAGE_SIZE * FOX_HEADS),
                             functools.partial(lambda b, s, pt, k: (layer, pt[b, s * pg + k], 0, 0), k=k))
                for k in range(pg)]
    grid_spec = pltpu.PrefetchScalarGridSpec(
        num_scalar_prefetch=1,
        grid=(bs, n_pages // pg),
        in_specs=[_seq_spec(a.shape) for a in fixed] + [pl.BlockSpec(du.shape, lambda b, s, pt: (0, 0, 0))]
        + _page_specs(cache, layer, pg, 2 * w) + lf_specs,
        out_specs=_seq_spec((bs, FOX_HEADS, w)),
        scratch_shapes=[pltpu.VMEM((l_pad, w), bf16), pltpu.VMEM((l_pad, w), bf16),
                        pltpu.VMEM((n_pages, PAGE_SIZE * FOX_HEADS), f32),
                        pltpu.VMEM((n_pages, PAGE_SIZE), f32), pltpu.VMEM((FOX_HEADS, l_pad), f32)])
    return pl.pallas_call(
        functools.partial(_fox_sample_kernel, pg=pg, n_pages=n_pages),
        grid_spec=grid_spec,
        out_shape=jax.ShapeDtypeStruct((bs, FOX_HEADS, w), f32),
        compiler_params=pltpu.CompilerParams(dimension_semantics=("arbitrary", "arbitrary"),
                                             vmem_limit_bytes=VMEM_LIMIT),
        name="fox_sample",
    )(page_table, *fixed, du, *([cache] * pg), *([cache_lf] * pg))


def _dsa_sample_kernel(pt_ref, qatt_ref, qidx_ref, iw_ref, new_ref, *rest, pg, n_pages, n_keep):
    page_refs = rest[:pg]
    o_ref, kv_sc, ik_sc = rest[pg:]
    s = pl.program_id(1)
    n_tok = n_pages * PAGE_SIZE
    l_pad = kv_sc.shape[0]
    bf16 = jnp.bfloat16
    hd = HEAD_DIM

    def split(page):
        ik = jnp.concatenate([page[:, 2 * hd:3 * hd], jnp.zeros((page.shape[0], hd), page.dtype)], axis=1)
        return page[:, 0:2 * hd].astype(bf16), ik.astype(bf16)

    for k in range(pg):
        p = s * pg + k
        row0 = pl.multiple_of(p * PAGE_SIZE, PAGE_SIZE)
        kv, ik = split(page_refs[k][...])
        kv_sc[pl.ds(row0, PAGE_SIZE), :] = kv
        ik_sc[pl.ds(row0, PAGE_SIZE), :] = ik

    @pl.when(s == pl.num_programs(1) - 1)
    def _():
        first = lax.broadcasted_iota(jnp.int32, (PAGE_SIZE, 1), 0) == 0
        kv, ik = split(jnp.where(first, new_ref[...], 0.0))
        kv_sc[n_tok:l_pad, :] = kv
        ik_sc[n_tok:l_pad, :] = ik
        kpos = lax.broadcasted_iota(jnp.int32, (1, l_pad), 1)
        causal = kpos <= n_tok
        a = lax.dot_general(qidx_ref[...], ik_sc[...], _NT, preferred_element_type=jnp.float32)
        a = jnp.maximum(a, 0.0) * iw_ref[...]
        sc = a[0:1]
        for h in range(1, DSA_IDX_HEADS):
            sc = sc + a[h:h + 1]
        key = _sortable_key(jnp.where(causal, sc, -jnp.inf))

        def count(mask):
            return jnp.sum(jnp.where(mask, 1.0, 0.0), axis=-1, keepdims=True)

        def bit_body(it, lo):
            cand = lo + lax.shift_left(jnp.int32(1), jnp.int32(31) - it)
            return jnp.where(count(key >= cand) >= n_keep, cand, lo)

        thr = lax.fori_loop(0, 32, bit_body, jnp.full((1, 1), _INT_MIN, jnp.int32))
        need = n_keep - count(key > thr)
        tie = jnp.where(key == thr, kpos, jnp.int32(2 ** 30))
        idx_bits = max(1, (l_pad - 1).bit_length())

        def idx_body(it, bound):
            step = lax.shift_left(jnp.int32(1), jnp.int32(idx_bits - 1) - it)
            return jnp.where(count(tie <= bound + step - 1) < need, bound + step, bound)

        bound = lax.fori_loop(0, idx_bits, idx_body, jnp.zeros((1, 1), jnp.int32))
        valid = ((key > thr) | (tie <= bound)) & causal
        s_att = lax.dot_general(qatt_ref[...], kv_sc[...], _NT, preferred_element_type=jnp.float32)
        pr = _masked_softmax(s_att, valid)
        o_ref[...] = jnp.dot(pr.astype(bf16), kv_sc[...], preferred_element_type=jnp.float32)


def _dsa_sample(page_table, q_att, q_idx, iw_col, new_rows, cache, layer, *, pg, n_keep):
    bs, n_pages = page_table.shape
    n_tok = n_pages * PAGE_SIZE
    l_pad = n_tok + PAGE_SIZE
    f32, bf16 = jnp.float32, jnp.bfloat16
    fixed = (q_att, q_idx, iw_col, new_rows)
    grid_spec = pltpu.PrefetchScalarGridSpec(
        num_scalar_prefetch=1,
        grid=(bs, n_pages // pg),
        in_specs=[_seq_spec(a.shape) for a in fixed] + _page_specs(cache, layer, pg, 3 * HEAD_DIM),
        out_specs=_seq_spec((bs, DSA_HEADS, LANES)),
        scratch_shapes=[pltpu.VMEM((l_pad, LANES), bf16), pltpu.VMEM((l_pad, LANES), bf16)])
    return pl.pallas_call(
        functools.partial(_dsa_sample_kernel, pg=pg, n_pages=n_pages, n_keep=n_keep),
        grid_spec=grid_spec,
        out_shape=jax.ShapeDtypeStruct((bs, DSA_HEADS, LANES), f32),
        compiler_params=pltpu.CompilerParams(dimension_semantics=("arbitrary", "arbitrary"),
                                             vmem_limit_bytes=VMEM_LIMIT),
        name="dsa_sample",
    )(page_table, *fixed, *([cache] * pg))


def _merge_weights(w_bn, w_bf, w_bd, w_o):
    hd = HEAD_DIM
    rows = []
    for c in range(NSA_GROUP):
        rows += list(range(hd * c, hd * c + hd)) + list(range(hd * (c + NSA_GROUP), hd * (c + NSA_GROUP) + hd))
    w_bn_p = jnp.take(w_bn, np.asarray(rows, np.int32), axis=0).astype(jnp.bfloat16)
    w_bd_p = jnp.pad(w_bd.reshape(DSA_HEADS, hd, D_MODEL), ((0, 0), (0, hd), (0, 0)))
    w_bd_p = w_bd_p.reshape(DSA_HEADS * LANES, D_MODEL).astype(jnp.bfloat16)
    return w_bn_p, w_bf.astype(jnp.bfloat16), w_bd_p, w_o.astype(jnp.bfloat16)


def _router_weights(w_rg, b_rg, w_re, b_re):
    pad = LANES - N_EXPERTS - N_GROUPS
    w = jnp.concatenate([w_re, w_rg, jnp.zeros((D_MODEL, pad), w_re.dtype)], axis=1).astype(jnp.float32)
    b = jnp.concatenate([b_re, b_rg, jnp.zeros((pad,), b_re.dtype)]).astype(jnp.float32)[None, :]
    return w, b


TM_PROJ = 512
TQ_FOX = 512
TQ_NSA = 256
TQ_DSA = 256
TM_MERGE = 512
TM_MOE = 1024
PAGES_PER_STEP = 16


def _dsa_rows(dki, misc):
    return jnp.concatenate([dki[:, :HEAD_DIM], misc[:, :HEAD_DIM], dki[:, HEAD_DIM:]], axis=-1)


def kernel(x_prompt, x_sample, cache_nsa, cache_fox, cache_fox_logf, cache_dsa, cache_win, page_table, norm_attn_g, w_in, nsa_qk_g, fox_qk_g, fox_b_f, dsa_qk_g, w_branch_nsa, w_branch_fox, w_branch_dsa, w_out, norm_ffn_g, w_router_group, b_router_group, w_router_expert, b_router_expert, w_exp_gate, w_exp_up, w_exp_down):
    depth = w_in.shape[0]
    B, S, D = x_prompt.shape
    Bs, T, _ = x_sample.shape
    assert T == 1 and D == D_MODEL
    n_pages = page_table.shape[1]
    past_len = n_pages * PAGE_SIZE
    win_keep = cache_win.shape[2]
    n_phys = cache_nsa.shape[1]
    bf16 = jnp.bfloat16
    hd = HEAD_DIM

    tm_proj = min(TM_PROJ, S)
    cos_p, sin_p = _rope_tables(jnp.arange(S))
    cos_s, sin_s = _rope_tables(jnp.full((Bs,), past_len))
    pg = min(PAGES_PER_STEP, n_pages)
    c_nsa = cache_nsa.reshape(depth, n_phys, PAGE_SIZE, 4 * LANES)
    c_fox = cache_fox.reshape(depth, n_phys, PAGE_SIZE, 2 * FOX_HEADS * hd)
    c_lf = cache_fox_logf.astype(jnp.float32).reshape(depth, n_phys, 1, PAGE_SIZE * FOX_HEADS)
    c_dsa = cache_dsa.reshape(depth, n_phys, PAGE_SIZE, 3 * hd)
    c_win = cache_win.reshape(depth, Bs, win_keep, 2 * LANES)
    lane = jnp.arange(LANES)
    left = lane < hd

    xp = x_prompt.reshape(B * S, D)
    xs = x_sample.reshape(Bs, D)
    outs = [[] for _ in range(10)]
    for l in range(depth):
        w_perm, w_mg, gain_row, bf_row = _layer_tables(w_in[l], nsa_qk_g[l], fox_qk_g[l], fox_b_f[l], dsa_qk_g[l])
        w_b = _merge_weights(w_branch_nsa[l], w_branch_fox[l], w_branch_dsa[l], w_out[l])
        w_r, b_r = _router_weights(w_router_group[l], b_router_group[l], w_router_expert[l], b_router_expert[l])
        w_e = (w_exp_gate[l].astype(bf16), w_exp_up[l].astype(bf16), w_exp_down[l].astype(bf16))
        g_attn = norm_attn_g[l][None, :]
        g_ffn = norm_ffn_g[l][None, :]

        (qn, nsa_rows, win_rows, slc_kv, win_kv, cmp, qf, fox_rows, fox_kv, dqi, dki, dki_bf, misc, misc_bf,
         cum) = _project(xp, g_attn, w_perm, gain_row, cos_p, sin_p, bf_row,
                         tm=tm_proj, tiles_per_batch=S // tm_proj, with_blocks=True)
        cum_t = cum[:, _M_FF:_M_IW].reshape(B, S, FOX_HEADS).transpose(0, 2, 1)
        o_fox = _fox_prompt(qf, fox_kv, cum, cum_t, batch=B, seq=S, tq=min(TQ_FOX, S))
        o_nsa = _nsa_prompt(qn, misc, cmp.astype(bf16), slc_kv, win_kv, batch=B, seq=S, tq=min(TQ_NSA, S))
        o_dsa = _dsa_prompt(dqi, misc, dki_bf, misc_bf, batch=B, seq=S, tq=min(TQ_DSA, S),
                            n_keep=min(DSA_TOPK, S // 4))
        xp = _merge(xp, g_attn, o_nsa, o_fox, o_dsa, w_mg, *w_b, tm=min(TM_MERGE, B * S))
        xp = _moe(xp, g_ffn, w_r, b_r, *w_e, tm=min(TM_MOE, B * S))
        outs[0].append(nsa_rows.reshape(B, S, 4, NSA_KV_HEADS, hd))
        outs[2].append(fox_rows.reshape(B, S, 2, FOX_HEADS, hd))
        outs[4].append(misc[:, _M_FF:_M_IW].reshape(B, S, FOX_HEADS))
        outs[6].append(_dsa_rows(dki, misc).reshape(B, S, 3, hd))
        outs[8].append(win_rows.reshape(B, S, 2, NSA_KV_HEADS, hd)[:, S - min(NSA_WINDOW, S):])

        (qn, nsa_rows, win_rows, _, _, _, qf, fox_rows, _, dqi, dki, _, misc, _, _) = _project(
            xs, g_attn, w_perm, gain_row, cos_s, sin_s, bf_row, tm=Bs, tiles_per_batch=1, with_blocks=False)
        chunks = qn.reshape(Bs, NSA_GROUP, LANES)
        qbd = jnp.concatenate([jnp.where(left, chunks, 0), jnp.where(left, 0, chunks)], axis=1)
        gates = misc[:, _M_GATE:_M_FF].reshape(Bs, 3, NSA_HEADS).transpose(0, 2, 1)
        gates = jnp.pad(gates, ((0, 0), (0, 0), (0, LANES - 3)))
        o = _nsa_sample(page_table, qbd, gates, nsa_rows[:, None, :], win_rows[:, None, :], c_win[l], c_nsa, l, pg=pg)
        o_nsa = jnp.where(left, o[:, :NSA_GROUP], o[:, NSA_GROUP:]).reshape(Bs, 4 * LANES).astype(bf16)

        head_of_lane = jnp.arange(FOX_HEADS * hd) // hd
        qbd_f = jnp.where(head_of_lane[None, None, :] == jnp.arange(FOX_HEADS)[None, :, None], qf[:, None, :], 0)
        o = _fox_sample(page_table, qbd_f, fox_rows[:, None, :], misc[:, None, _M_FF:_M_IW], c_fox, c_lf, l, pg=pg)
        o_fox = jnp.einsum('bhhd->bhd', o.reshape(Bs, FOX_HEADS, FOX_HEADS, hd)).reshape(Bs, FOX_HEADS * hd).astype(bf16)

        chunks = dqi.reshape(Bs, DSA_HEADS, LANES)
        q_att = jnp.where(left, chunks, 0)
        q_idx = jnp.concatenate([chunks[..., hd:], jnp.zeros_like(chunks[..., hd:])], axis=-1)
        dsa_new = _dsa_rows(dki, misc)
        o = _dsa_sample(page_table, q_att, q_idx, misc[:, _M_IW:_M_END, None], dsa_new[:, None, :], c_dsa, l,
                        pg=pg, n_keep=min(DSA_TOPK, (past_len + 1) // 4))
        o_dsa = jnp.concatenate([o[..., hd:], jnp.zeros_like(o[..., hd:])], axis=-1).reshape(Bs, 4 * LANES).astype(bf16)

        xs = _merge(xs, g_attn, o_nsa, o_fox, o_dsa, w_mg, *w_b, tm=Bs)
        xs = _moe(xs, g_ffn, w_r, b_r, *w_e, tm=Bs)
        outs[1].append(nsa_rows.reshape(Bs, 1, 4, NSA_KV_HEADS, hd))
        outs[3].append(fox_rows.reshape(Bs, 1, 2, FOX_HEADS, hd))
        outs[5].append(misc[:, _M_FF:_M_IW].reshape(Bs, 1, FOX_HEADS))
        outs[7].append(dsa_new.reshape(Bs, 1, 3, hd))
        win_all = jnp.concatenate([cache_win[l], win_rows.reshape(Bs, 1, 2, NSA_KV_HEADS, hd)], axis=1)
        outs[9].append(win_all[:, 1:])
    return (xp.reshape(B, S, D), xs.reshape(Bs, 1, D)) + tuple(jnp.stack(o) for o in outs)
```

```python
import functools

import numpy as np
import jax
import jax.numpy as jnp
from jax import lax
from jax.experimental import pallas as pl
from jax.experimental.pallas import tpu as pltpu

D_MODEL = 1024
HEAD_DIM = 64
HALF = HEAD_DIM // 2
NSA_HEADS = 8
NSA_KV_HEADS = 2
NSA_GROUP = NSA_HEADS // NSA_KV_HEADS
NSA_BLOCK = 64
NSA_TOP_BLOCKS = 16
NSA_WINDOW = 512
NSA_FORCE_BONUS = 8.0
FOX_HEADS = 4
DSA_HEADS = 4
DSA_IDX_HEADS = 4
DSA_TOPK = 256
N_GROUPS = 4
EXPERTS_PER_GROUP = 4
N_EXPERTS = N_GROUPS * EXPERTS_PER_GROUP
D_EXPERT = 512
ROPE_THETA = 10000.0
NORM_EPS = 1e-6
NEG_BIG = -1e30
PAGE_SIZE = 128
QK_SCALE = HEAD_DIM ** -0.5

LANES = 128
VMEM_LIMIT = 56 * 1024 * 1024

_O_NQ = 0
_O_NKV = _O_NQ + NSA_HEADS * HEAD_DIM
_O_NGATE = _O_NKV + 6 * NSA_KV_HEADS * HEAD_DIM
_O_FQKV = _O_NGATE + 3 * NSA_HEADS
_O_FF = _O_FQKV + 3 * FOX_HEADS * HEAD_DIM
_O_DQ = _O_FF + FOX_HEADS
_O_DKV = _O_DQ + DSA_HEADS * HEAD_DIM
_O_DIQ = _O_DKV + 2 * HEAD_DIM
_O_DIK = _O_DIQ + DSA_IDX_HEADS * HEAD_DIM
_O_DIW = _O_DIK + HEAD_DIM
_O_MG = _O_DIW + DSA_IDX_HEADS
D_IN = _O_MG + 3 * D_MODEL

_M_GATE = HEAD_DIM
_M_FF = _M_GATE + 3 * NSA_HEADS
_M_IW = _M_FF + FOX_HEADS
_M_END = _M_IW + DSA_IDX_HEADS

_C_NQ = 0
_C_NKV = 4
_C_FQ = 10
_C_FK = 12
_C_FV = 14
_C_DQI = 16
_C_DKI = 20
_C_MISC = 21
N_CHUNKS = 22
D_PROJ = N_CHUNKS * LANES


def _proj_column_permutation():
    idx = []
    for c in range(4):
        idx += list(range(_O_NQ + 64 * c, _O_NQ + 64 * c + 64))
        idx += list(range(_O_NQ + 64 * (c + 4), _O_NQ + 64 * (c + 4) + 64))
    idx += list(range(_O_NKV, _O_NKV + 768))
    idx += list(range(_O_FQKV, _O_FQKV + 768))
    for h in range(4):
        idx += list(range(_O_DQ + 64 * h, _O_DQ + 64 * h + 64))
        idx += list(range(_O_DIQ + 64 * h, _O_DIQ + 64 * h + 64))
    idx += list(range(_O_DKV, _O_DKV + 64)) + list(range(_O_DIK, _O_DIK + 64))
    idx += list(range(_O_DKV + 64, _O_DKV + 128))
    idx += list(range(_O_NGATE, _O_NGATE + 24)) + list(range(_O_FF, _O_FF + 4)) + list(range(_O_DIW, _O_DIW + 4))
    idx += [D_IN] * (LANES - _M_END)
    assert len(idx) == D_PROJ
    return np.asarray(idx, np.int32)


def _rope_tables(pos):
    inv = ROPE_THETA ** (-jnp.arange(HALF, dtype=jnp.float32) * 2.0 / HEAD_DIM)
    ang = pos.astype(jnp.float32)[:, None] * inv[None, :]
    cos, sin = jnp.cos(ang), jnp.sin(ang)
    cos = jnp.concatenate([cos, cos, cos, cos], axis=-1)
    sin = jnp.concatenate([-sin, sin, -sin, sin], axis=-1)
    return cos, sin


def _rms_rows(x, g):
    return x * lax.rsqrt(jnp.mean(jnp.square(x), axis=-1, keepdims=True) + NORM_EPS) * g


def _split_dot(a, b_bf16):
    hi = a.astype(jnp.bfloat16)
    lo = (a - hi.astype(jnp.float32)).astype(jnp.bfloat16)
    return (jnp.dot(hi, b_bf16, preferred_element_type=jnp.float32)
            + jnp.dot(lo, b_bf16, preferred_element_type=jnp.float32))


def _proj_kernel(x_ref, g_ref, w_ref, gain_ref, cos_ref, sin_ref, bd_ref, bf_ref, tri_ref,
                 qn_ref, nsa_rows_ref, win_rows_ref, slc_kv_ref, win_kv_ref, cmp_ref,
                 qf_ref, fox_rows_ref, fox_kv_ref, dqi_ref, dki_ref, dki_bf_ref,
                 misc_ref, misc_bf_ref, cum_ref, carry_ref, *, tiles_per_batch, with_blocks):
    i = pl.program_id(0)
    precise = w_ref.dtype == jnp.float32
    xn = _rms_rows(x_ref[...], g_ref[...])
    if not precise:
        xn = xn.astype(jnp.bfloat16)
    lane = lax.broadcasted_iota(jnp.int32, (1, LANES), 1)
    left = lane < HEAD_DIM
    first_half = (lane % HEAD_DIM) < HALF
    cos = cos_ref[...]
    sin = sin_ref[...]
    bd = bd_ref[...]

    def chunk(c):
        return jnp.dot(xn, w_ref[:, c * LANES:(c + 1) * LANES], preferred_element_type=jnp.float32,
                       precision=lax.Precision.HIGHEST if precise else None)

    def head_norm(h, c, only_left=False):
        ms = _split_dot(h * h, bd)
        y = h * lax.rsqrt(ms + NORM_EPS) * gain_ref[:, c * LANES:(c + 1) * LANES]
        return jnp.where(left, y, h) if only_left else y

    def rope(h):
        swapped = jnp.where(first_half, pltpu.roll(h, LANES - HALF, 1), pltpu.roll(h, HALF, 1))
        return h * cos + swapped * sin

    for c in range(4):
        q = rope(head_norm(chunk(_C_NQ + c), _C_NQ + c)) * QK_SCALE
        qn_ref[:, c * LANES:(c + 1) * LANES] = q.astype(qn_ref.dtype)

    kc = rope(head_norm(chunk(_C_NKV + 0), _C_NKV + 0))
    vc = chunk(_C_NKV + 1)
    ks = rope(head_norm(chunk(_C_NKV + 2), _C_NKV + 2))
    vs = chunk(_C_NKV + 3)
    kw = rope(head_norm(chunk(_C_NKV + 4), _C_NKV + 4))
    vw = chunk(_C_NKV + 5)
    nsa_rows_ref[:, 0 * LANES:1 * LANES] = kc
    nsa_rows_ref[:, 1 * LANES:2 * LANES] = vc
    nsa_rows_ref[:, 2 * LANES:3 * LANES] = ks
    nsa_rows_ref[:, 3 * LANES:4 * LANES] = vs
    win_rows_ref[:, 0:LANES] = kw
    win_rows_ref[:, LANES:2 * LANES] = vw
    slc_kv_ref[:, 0:LANES] = ks.astype(jnp.bfloat16)
    slc_kv_ref[:, LANES:2 * LANES] = vs.astype(jnp.bfloat16)
    win_kv_ref[:, 0:LANES] = kw.astype(jnp.bfloat16)
    win_kv_ref[:, LANES:2 * LANES] = vw.astype(jnp.bfloat16)
    if with_blocks:
        tm = kc.shape[0]
        nblk = tm // NSA_BLOCK
        cmp_ref[:, 0:LANES] = jnp.mean(kc.reshape(nblk, NSA_BLOCK, LANES), axis=1)
        cmp_ref[:, LANES:2 * LANES] = jnp.mean(vc.reshape(nblk, NSA_BLOCK, LANES), axis=1)
    else:
        cmp_ref[...] = jnp.zeros_like(cmp_ref)

    for c in range(2):
        qf = head_norm(chunk(_C_FQ + c), _C_FQ + c) * QK_SCALE
        qf_ref[:, c * LANES:(c + 1) * LANES] = qf.astype(qf_ref.dtype)
        fk = head_norm(chunk(_C_FK + c), _C_FK + c)
        fv = chunk(_C_FV + c)
        fox_rows_ref[:, c * LANES:(c + 1) * LANES] = fk
        fox_rows_ref[:, (2 + c) * LANES:(3 + c) * LANES] = fv
        fox_kv_ref[:, c * LANES:(c + 1) * LANES] = fk.astype(jnp.bfloat16)
        fox_kv_ref[:, (2 + c) * LANES:(3 + c) * LANES] = fv.astype(jnp.bfloat16)

    for c in range(4):
        dqi = rope(head_norm(chunk(_C_DQI + c), _C_DQI + c, only_left=True)) * QK_SCALE
        dqi_ref[:, c * LANES:(c + 1) * LANES] = dqi.astype(dqi_ref.dtype)

    dki = rope(head_norm(chunk(_C_DKI), _C_DKI, only_left=True))
    dki_ref[...] = dki
    dki_bf_ref[...] = dki.astype(jnp.bfloat16)

    m = chunk(_C_MISC)
    zf = m + bf_ref[...]
    logf = jnp.minimum(zf, 0.0) - jnp.log(1.0 + jnp.exp(-jnp.abs(zf)))
    is_gate = (lane >= _M_GATE) & (lane < _M_FF)
    is_ff = (lane >= _M_FF) & (lane < _M_IW)
    is_iw = (lane >= _M_IW) & (lane < _M_END)
    out = jnp.where(is_gate, jax.nn.sigmoid(m), m)
    out = jnp.where(is_ff, logf, out)
    out = jnp.where(is_iw, m * (DSA_IDX_HEADS ** -0.5), out)
    misc_ref[...] = out
    misc_bf_ref[...] = out.astype(jnp.bfloat16)

    @pl.when(i % tiles_per_batch == 0)
    def _():
        carry_ref[...] = jnp.zeros_like(carry_ref)

    lf = jnp.where(is_ff, logf, 0.0)
    cum = jnp.dot(tri_ref[...], lf, preferred_element_type=jnp.float32,
                  precision=lax.Precision.HIGHEST) + carry_ref[...]
    cum_ref[...] = cum
    carry_ref[...] = cum[cum.shape[0] - 1:, :]


def _project(x2d, norm_g, w_perm, gain_row, cos, sin, bf_row, *, tm, tiles_per_batch, with_blocks):
    n = x2d.shape[0]
    nt = n // tm
    ncmp = max(tm // NSA_BLOCK, 8) if not with_blocks else tm // NSA_BLOCK
    bd = jnp.asarray(np.kron(np.eye(2), np.full((HEAD_DIM, HEAD_DIM), 1.0 / HEAD_DIM)), jnp.bfloat16)
    tri = jnp.asarray(np.tril(np.ones((tm, tm))), jnp.float32)
    f32, bf16 = jnp.float32, jnp.bfloat16
    row = lambda w: pl.BlockSpec((tm, w), lambda i: (i, 0))
    full = lambda a: pl.BlockSpec(a.shape, lambda i: (0,) * a.ndim)
    pos_spec = pl.BlockSpec((tm, LANES), lambda i: (i % tiles_per_batch, 0))
    qdt = w_perm.dtype
    outs = [
        (512, qdt),
        (512, f32),
        (256, f32),
        (256, bf16),
        (256, bf16),
        None,
        (256, qdt),
        (512, f32),
        (512, bf16),
        (512, qdt),
        (128, f32),
        (128, bf16),
        (128, f32),
        (128, bf16),
        (128, f32),
    ]
    out_shape, out_specs = [], []
    for o in outs:
        if o is None:
            out_shape.append(jax.ShapeDtypeStruct((nt * ncmp, 2 * LANES), f32))
            out_specs.append(pl.BlockSpec((ncmp, 2 * LANES), lambda i: (i, 0)))
        else:
            out_shape.append(jax.ShapeDtypeStruct((n, o[0]), o[1]))
            out_specs.append(row(o[0]))
    return pl.pallas_call(
        functools.partial(_proj_kernel, tiles_per_batch=tiles_per_batch, with_blocks=with_blocks),
        grid=(nt,),
        in_specs=[row(D_MODEL), full(norm_g), full(w_perm), full(gain_row), pos_spec, pos_spec,
                  full(bd), full(bf_row), full(tri)],
        out_specs=out_specs,
        out_shape=out_shape,
        scratch_shapes=[pltpu.VMEM((1, LANES), f32)],
        compiler_params=pltpu.CompilerParams(dimension_semantics=("arbitrary",),
                                             vmem_limit_bytes=VMEM_LIMIT),
        name="proj",
    )(x2d, norm_g, w_perm, gain_row, cos, sin, bd, bf_row, tri)


def _layer_tables(w_in_l, nsa_qk_g_l, fox_qk_g_l, fox_b_f_l, dsa_qk_g_l):
    perm = _proj_column_permutation()
    w_ext = jnp.concatenate([w_in_l, jnp.zeros((D_MODEL, 1), w_in_l.dtype)], axis=1)
    w_perm = jnp.take(w_ext, perm, axis=1).astype(jnp.float32)
    w_mg = w_in_l[:, _O_MG:].astype(jnp.bfloat16)
    one = jnp.ones((HEAD_DIM,), jnp.float32)
    two = lambda g: jnp.concatenate([g, g])
    ones2 = two(one)
    parts = [two(nsa_qk_g_l[0])] * 4
    parts += [two(nsa_qk_g_l[1]), ones2, two(nsa_qk_g_l[2]), ones2, two(nsa_qk_g_l[3]), ones2]
    parts += [two(fox_qk_g_l[0])] * 2 + [two(fox_qk_g_l[1])] * 2 + [ones2] * 2
    parts += [jnp.concatenate([dsa_qk_g_l[0], one])] * 4
    parts += [jnp.concatenate([dsa_qk_g_l[1], one]), ones2]
    gain_row = jnp.concatenate(parts).astype(jnp.float32)[None, :]
    bf_row = jnp.zeros((LANES,), jnp.float32).at[_M_FF:_M_IW].set(fox_b_f_l.astype(jnp.float32))[None, :]
    return w_perm, w_mg, gain_row, bf_row


_NT = (((1,), (1,)), ((), ()))


def _half_masks():
    lane = lax.broadcasted_iota(jnp.int32, (1, LANES), 1)
    return lane < HEAD_DIM, lane >= HEAD_DIM


def _online_update(s, valid, v, m_prev, l_prev, acc_prev):
    s = jnp.where(valid, s, -jnp.inf)
    m_new = jnp.maximum(m_prev, jnp.max(s, axis=-1, keepdims=True))
    p = jnp.exp(s - m_new)
    alpha = jnp.exp(m_prev - m_new)
    l_new = alpha * l_prev + jnp.sum(p, axis=-1, keepdims=True)
    pv = jnp.dot(p.reshape(-1, p.shape[-1]).astype(jnp.bfloat16), v, preferred_element_type=jnp.float32)
    acc_new = alpha * acc_prev + pv.reshape(acc_prev.shape)
    return m_new, l_new, acc_new


def _finish(l, acc):
    return acc / jnp.where(l > 0, l, 1.0)


def _fox_kernel(q_ref, kv_ref, cq_ref, ck_ref, o_ref, m_ref, l_ref, acc_ref, *, tq, tk):
    i = pl.program_id(1)
    j = pl.program_id(2)
    left, right = _half_masks()

    @pl.when(j == 0)
    def _():
        m_ref[...] = jnp.full_like(m_ref, NEG_BIG)
        l_ref[...] = jnp.zeros_like(l_ref)
        acc_ref[...] = jnp.zeros_like(acc_ref)

    @pl.when(j * tk < (i + 1) * tq)
    def _():
        rows = i * tq + lax.broadcasted_iota(jnp.int32, (tq, 1), 0)
        cols = j * tk + lax.broadcasted_iota(jnp.int32, (1, tk), 1)
        valid = cols <= rows
        for h in range(FOX_HEADS):
            c = h // 2
            qc = q_ref[:, c * LANES:(c + 1) * LANES]
            qh = jnp.where(left if h % 2 == 0 else right, qc, jnp.zeros_like(qc))
            k = kv_ref[:, c * LANES:(c + 1) * LANES]
            v = kv_ref[:, (2 + c) * LANES:(3 + c) * LANES]
            s = lax.dot_general(qh, k, _NT, preferred_element_type=jnp.float32)
            s = s + cq_ref[:, _M_FF + h:_M_FF + h + 1] - ck_ref[h:h + 1, :]
            m_ref[h], l_ref[h], acc_ref[h] = _online_update(s, valid, v, m_ref[h], l_ref[h], acc_ref[h])

    @pl.when(j == pl.num_programs(2) - 1)
    def _():
        for c in range(FOX_HEADS // 2):
            o = jnp.where(left, _finish(l_ref[2 * c], acc_ref[2 * c]), _finish(l_ref[2 * c + 1], acc_ref[2 * c + 1]))
            o_ref[:, c * LANES:(c + 1) * LANES] = o.astype(o_ref.dtype)


def _fox_prompt(qf, fox_kv, cum, cum_t, *, batch, seq, tq, tk):
    nq = seq // tq
    nk = seq // tk
    f32 = jnp.float32
    last = lambda i, j: jnp.minimum(j, ((i + 1) * tq - 1) // tk)
    return pl.pallas_call(
        functools.partial(_fox_kernel, tq=tq, tk=tk),
        grid=(batch, nq, nk),
        in_specs=[pl.BlockSpec((tq, 2 * LANES), lambda b, i, j: (b * nq + i, 0)),
                  pl.BlockSpec((tk, 4 * LANES), lambda b, i, j: (b * nk + last(i, j), 0)),
                  pl.BlockSpec((tq, LANES), lambda b, i, j: (b * nq + i, 0)),
                  pl.BlockSpec((None, FOX_HEADS, tk), lambda b, i, j: (b, 0, last(i, j)))],
        out_specs=pl.BlockSpec((tq, 2 * LANES), lambda b, i, j: (b * nq + i, 0)),
        out_shape=jax.ShapeDtypeStruct((batch * seq, 2 * LANES), jnp.bfloat16),
        scratch_shapes=[pltpu.VMEM((FOX_HEADS, tq, 1), f32), pltpu.VMEM((FOX_HEADS, tq, 1), f32),
                        pltpu.VMEM((FOX_HEADS, tq, LANES), f32)],
        compiler_params=pltpu.CompilerParams(dimension_semantics=("arbitrary", "arbitrary", "arbitrary")),
        name="fox_prompt",
    )(qf, fox_kv, cum, cum_t)


def _select_top_blocks(score, n_top):
    nb = score.shape[-1]
    blk = lax.broadcasted_iota(jnp.int32, (1, nb), 1).astype(jnp.float32)

    def body(_, carry):
        sc, sel = carry
        mx = jnp.max(sc, axis=-1, keepdims=True)
        first = jnp.min(jnp.where(sc == mx, blk, float(nb)), axis=-1, keepdims=True)
        pick = blk == first
        return jnp.where(pick, -jnp.inf, sc), jnp.where(pick, 1.0, sel)

    _, sel = lax.fori_loop(0, n_top, body, (score, jnp.zeros_like(score)))
    return sel


def _block_importance_scores(p_sum, qpos, nb):
    blk = lax.broadcasted_iota(jnp.int32, (1, nb), 1)
    cur = qpos // NSA_BLOCK
    forced = (blk == 0) | (blk == cur) | (blk == cur - 1)
    score = jnp.where(forced, p_sum + NSA_FORCE_BONUS, p_sum)
    return jnp.where(blk <= cur, score, -1.0)


def _masked_softmax(s, valid):
    s = jnp.where(valid, s, NEG_BIG)
    m = jnp.max(s, axis=-1, keepdims=True)
    e = jnp.where(valid, jnp.exp(s - m), 0.0)
    d = jnp.sum(e, axis=-1, keepdims=True)
    return e / jnp.where(d > 0, d, 1.0)


def _nsa_kernel(q_ref, misc_ref, cmp_ref, slc_ref, win_ref, o_ref, m_sc, l_sc, acc_sc, *, tq, tk, seq):
    i = pl.program_id(1)
    nb = seq // NSA_BLOCK
    g4 = NSA_GROUP
    left, right = _half_masks()
    qpos = i * tq + lax.broadcasted_iota(jnp.int32, (tq, 1), 0)
    blk = lax.broadcasted_iota(jnp.int32, (1, nb), 1)
    done = (blk + 1) * NSA_BLOCK <= qpos + 1
    kcm = cmp_ref[:, 0:LANES]
    vcm = cmp_ref[:, LANES:2 * LANES]
    lane_pos = lax.broadcasted_iota(jnp.int32, (1, tk), 1)
    n_top = min(NSA_TOP_BLOCKS, nb)
    n_chunks = ((i + 1) * tq + tk - 1) // tk
    tw = min(tq + NSA_WINDOW, seq)
    win_start = pl.multiple_of(jnp.clip(i * tq - NSA_WINDOW, 0, seq - tw), 16)
    win_pos = win_start + lax.broadcasted_iota(jnp.int32, (1, tw), 1)
    win_dist = qpos - win_pos
    win_valid = (win_dist >= 0) & (win_dist <= NSA_WINDOW)
    outs = []
    for g in range(NSA_KV_HEADS):
        hm = left if g == 0 else right
        qs = [q_ref[:, c * LANES:(c + 1) * LANES] for c in range(g4)]
        qg = jnp.concatenate([jnp.where(hm, q, jnp.zeros_like(q)) for q in qs], axis=0)

        s_c = lax.dot_general(qg, kcm, _NT, preferred_element_type=jnp.float32).reshape(g4, tq, nb)
        p_c = _masked_softmax(s_c, done[None])
        o_c = jnp.dot(p_c.reshape(g4 * tq, nb).astype(jnp.bfloat16), vcm,
                      preferred_element_type=jnp.float32).reshape(g4, tq, LANES)
        sel = _select_top_blocks(_block_importance_scores(jnp.sum(p_c, axis=0), qpos, nb), n_top)
        sel = sel.astype(jnp.bfloat16)

        m_sc[...] = jnp.full_like(m_sc, NEG_BIG)
        l_sc[...] = jnp.zeros_like(l_sc)
        acc_sc[...] = jnp.zeros_like(acc_sc)

        def slc_body(c, _):
            start = pl.multiple_of(c * tk, tk)
            k = slc_ref[pl.ds(start, tk), 0:LANES]
            v = slc_ref[pl.ds(start, tk), LANES:2 * LANES]
            kpos = c * tk + lane_pos
            expand = (lax.broadcasted_iota(jnp.int32, (nb, 1), 0) == kpos // NSA_BLOCK)
            chosen = jnp.dot(sel, jnp.where(expand, 1.0, 0.0).astype(jnp.bfloat16),
                             preferred_element_type=jnp.float32) > 0.5
            valid = chosen & (kpos <= qpos)
            s = lax.dot_general(qg, k, _NT, preferred_element_type=jnp.float32).reshape(g4, tq, tk)
            m_sc[...], l_sc[...], acc_sc[...] = _online_update(s, valid[None], v, m_sc[...], l_sc[...], acc_sc[...])
            return 0

        lax.fori_loop(0, n_chunks, slc_body, 0)
        o_s = _finish(l_sc[...], acc_sc[...])

        k = win_ref[pl.ds(win_start, tw), 0:LANES]
        v = win_ref[pl.ds(win_start, tw), LANES:2 * LANES]
        s_w = lax.dot_general(qg, k, _NT, preferred_element_type=jnp.float32).reshape(g4, tq, tw)
        p_w = _masked_softmax(s_w, win_valid[None])
        o_w = jnp.dot(p_w.reshape(g4 * tq, tw).astype(jnp.bfloat16), v,
                      preferred_element_type=jnp.float32).reshape(g4, tq, LANES)

        heads = []
        for c in range(g4):
            h = g * g4 + c
            gate = lambda br: misc_ref[:, _M_GATE + br * NSA_HEADS + h:_M_GATE + br * NSA_HEADS + h + 1]
            heads.append(gate(0) * o_c[c] + gate(1) * o_s[c] + gate(2) * o_w[c])
        outs.append(heads)
    for c in range(g4):
        o_ref[:, c * LANES:(c + 1) * LANES] = jnp.where(left, outs[0][c], outs[1][c]).astype(o_ref.dtype)


def _nsa_prompt(qn, misc, cmp_bf, slc_kv, win_kv, *, batch, seq, tq, tk):
    nq = seq // tq
    nb = seq // NSA_BLOCK
    f32 = jnp.float32
    g4 = NSA_GROUP
    return pl.pallas_call(
        functools.partial(_nsa_kernel, tq=tq, tk=tk, seq=seq),
        grid=(batch, nq),
        in_specs=[pl.BlockSpec((tq, 4 * LANES), lambda b, i: (b * nq + i, 0)),
                  pl.BlockSpec((tq, LANES), lambda b, i: (b * nq + i, 0)),
                  pl.BlockSpec((nb, 2 * LANES), lambda b, i: (b, 0)),
                  pl.BlockSpec((seq, 2 * LANES), lambda b, i: (b, 0)),
                  pl.BlockSpec((seq, 2 * LANES), lambda b, i: (b, 0))],
        out_specs=pl.BlockSpec((tq, 4 * LANES), lambda b, i: (b * nq + i, 0)),
        out_shape=jax.ShapeDtypeStruct((batch * seq, 4 * LANES), jnp.bfloat16),
        scratch_shapes=[pltpu.VMEM((g4, tq, 1), f32), pltpu.VMEM((g4, tq, 1), f32),
                        pltpu.VMEM((g4, tq, LANES), f32)],
        compiler_params=pltpu.CompilerParams(dimension_semantics=("arbitrary", "arbitrary"),
                                             vmem_limit_bytes=VMEM_LIMIT),
        name="nsa_prompt",
    )(qn, misc, cmp_bf, slc_kv, win_kv)


_INT_MIN = -2 ** 31


def _sortable_key(x):
    bits = lax.bitcast_convert_type(x, jnp.int32)
    return jnp.where(bits < 0, bits ^ jnp.int32(0x7FFFFFFF), bits)


def _lane_fold(x):
    acc = x[:, 0:LANES]
    for c in range(1, x.shape[-1] // LANES):
        acc = acc + x[:, c * LANES:(c + 1) * LANES]
    return acc


def _dsa_kernel(dqi_ref, misc_ref, dki_ref, v_ref, o_ref, key_sc, tie_sc, bound_sc, m_sc, l_sc, acc_sc,
                *, tq, tk, seq, n_keep):
    i = pl.program_id(1)
    nh = DSA_HEADS
    left, right = _half_masks()
    qpos = i * tq + lax.broadcasted_iota(jnp.int32, (tq, 1), 0)
    lane_pos = lax.broadcasted_iota(jnp.int32, (1, tk), 1)
    chunks = [dqi_ref[:, h * LANES:(h + 1) * LANES] for h in range(nh)]
    q_att = jnp.concatenate([jnp.where(left, q, jnp.zeros_like(q)) for q in chunks], axis=0)
    q_idx = jnp.concatenate([jnp.where(right, q, jnp.zeros_like(q)) for q in chunks], axis=0)
    n_chunks = ((i + 1) * tq + tk - 1) // tk

    def score_body(c, _):
        start = pl.multiple_of(c * tk, tk)
        kk = dki_ref[pl.ds(start, tk), :]
        a = lax.dot_general(q_idx, kk, _NT, preferred_element_type=jnp.float32).reshape(nh, tq, tk)
        a = jnp.maximum(a, 0.0)
        sc = a[0] * misc_ref[:, _M_IW:_M_IW + 1]
        for h in range(1, nh):
            sc = sc + a[h] * misc_ref[:, _M_IW + h:_M_IW + h + 1]
        kpos = c * tk + lane_pos
        sc = jnp.where(kpos <= qpos, sc, -jnp.inf)
        key_sc[c] = _sortable_key(sc)
        return 0

    lax.fori_loop(0, n_chunks, score_body, 0)

    def count(pred):
        def body(c, acc):
            return acc + _lane_fold(jnp.where(pred(c), 1.0, 0.0))
        part = lax.fori_loop(0, n_chunks, body, jnp.zeros((tq, LANES), jnp.float32))
        return jnp.sum(part, axis=-1, keepdims=True)

    def bit_body(it, lo):
        cand = lo + lax.shift_left(jnp.int32(1), jnp.int32(31) - it)
        cnt = count(lambda c: key_sc[c] >= cand)
        return jnp.where(cnt >= n_keep, cand, lo)

    thr = lax.fori_loop(0, 32, bit_body, jnp.full((tq, 1), _INT_MIN, jnp.int32))

    big = jnp.int32(2 ** 30)
    bound_sc[...] = jnp.full_like(bound_sc, big)
    surplus = count(lambda c: key_sc[c] >= thr) - n_keep

    @pl.when(jnp.max(surplus) > 0)
    def _():
        need = n_keep - count(lambda c: key_sc[c] > thr)

        def tie_body(c, _):
            kpos = c * tk + lane_pos
            tie_sc[c] = jnp.where(key_sc[c] == thr, kpos, big)
            return 0

        lax.fori_loop(0, n_chunks, tie_body, 0)
        idx_bits = max(1, (seq - 1).bit_length())

        def idx_body(it, bound):
            step = lax.shift_left(jnp.int32(1), jnp.int32(idx_bits - 1) - it)
            cand = bound + step - 1
            cnt = count(lambda c: tie_sc[c] <= cand)
            return jnp.where(cnt < need, bound + step, bound)

        bound_sc[...] = lax.fori_loop(0, idx_bits, idx_body, jnp.zeros((tq, 1), jnp.int32))

    bound = bound_sc[...]
    m_sc[...] = jnp.full_like(m_sc, NEG_BIG)
    l_sc[...] = jnp.zeros_like(l_sc)
    acc_sc[...] = jnp.zeros_like(acc_sc)

    def att_body(c, _):
        start = pl.multiple_of(c * tk, tk)
        kk = dki_ref[pl.ds(start, tk), :]
        vv = v_ref[pl.ds(start, tk), :]
        kpos = c * tk + lane_pos
        key = key_sc[c]
        valid = ((key > thr) | ((key == thr) & (kpos <= bound))) & (kpos <= qpos)
        s = lax.dot_general(q_att, kk, _NT, preferred_element_type=jnp.float32).reshape(nh, tq, tk)
        m_sc[...], l_sc[...], acc_sc[...] = _online_update(s, valid[None], vv, m_sc[...], l_sc[...], acc_sc[...])
        return 0

    lax.fori_loop(0, n_chunks, att_body, 0)
    o = _finish(l_sc[...], acc_sc[...])
    for h in range(nh):
        o_ref[:, h * LANES:(h + 1) * LANES] = jnp.where(left, o[h], 0.0).astype(o_ref.dtype)


def _dsa_prompt(dqi, misc, dki_bf, misc_bf, *, batch, seq, tq, tk, n_keep):
    nq = seq // tq
    nk = seq // tk
    f32 = jnp.float32
    nh = DSA_HEADS
    return pl.pallas_call(
        functools.partial(_dsa_kernel, tq=tq, tk=tk, seq=seq, n_keep=n_keep),
        grid=(batch, nq),
        in_specs=[pl.BlockSpec((tq, 4 * LANES), lambda b, i: (b * nq + i, 0)),
                  pl.BlockSpec((tq, LANES), lambda b, i: (b * nq + i, 0)),
                  pl.BlockSpec((seq, LANES), lambda b, i: (b, 0)),
                  pl.BlockSpec((seq, LANES), lambda b, i: (b, 0))],
        out_specs=pl.BlockSpec((tq, 4 * LANES), lambda b, i: (b * nq + i, 0)),
        out_shape=jax.ShapeDtypeStruct((batch * seq, 4 * LANES), jnp.bfloat16),
        scratch_shapes=[pltpu.VMEM((nk, tq, tk), jnp.int32), pltpu.VMEM((nk, tq, tk), jnp.int32),
                        pltpu.VMEM((tq, 1), jnp.int32),
                        pltpu.VMEM((nh, tq, 1), f32), pltpu.VMEM((nh, tq, 1), f32),
                        pltpu.VMEM((nh, tq, LANES), f32)],
        compiler_params=pltpu.CompilerParams(dimension_semantics=("arbitrary", "arbitrary"),
                                             vmem_limit_bytes=VMEM_LIMIT),
        name="dsa_prompt",
    )(dqi, misc, dki_bf, misc_bf)


def _merge_kernel(x_ref, g_ref, on_ref, of_ref, od_ref, wmg_ref, wbn_ref, wbf_ref, wbd_ref, wo_ref, out_ref):
    x = x_ref[...]
    xn = _rms_rows(x, g_ref[...]).astype(jnp.bfloat16)
    y = None
    for br, (o_ref, wb_ref) in enumerate(((on_ref, wbn_ref), (of_ref, wbf_ref), (od_ref, wbd_ref))):
        gate = jax.nn.sigmoid(jnp.dot(xn, wmg_ref[:, br * D_MODEL:(br + 1) * D_MODEL],
                                      preferred_element_type=jnp.float32))
        term = gate * jnp.dot(o_ref[...], wb_ref[...], preferred_element_type=jnp.float32)
        y = term if y is None else y + term
    out_ref[...] = x + jnp.dot(y.astype(jnp.bfloat16), wo_ref[...], preferred_element_type=jnp.float32)


def _merge(x2d, norm_g, o_nsa, o_fox, o_dsa, w_mg, w_bn, w_bf, w_bd, w_o, *, tm):
    n = x2d.shape[0]
    row = lambda a: pl.BlockSpec((tm, a.shape[1]), lambda i: (i, 0))
    full = lambda a: pl.BlockSpec(a.shape, lambda i: (0,) * a.ndim)
    args = (x2d, norm_g, o_nsa, o_fox, o_dsa, w_mg, w_bn, w_bf, w_bd, w_o)
    return pl.pallas_call(
        _merge_kernel,
        grid=(n // tm,),
        in_specs=[row(x2d), full(norm_g), row(o_nsa), row(o_fox), row(o_dsa)] + [full(a) for a in args[5:]],
        out_specs=row(x2d),
        out_shape=jax.ShapeDtypeStruct(x2d.shape, x2d.dtype),
        compiler_params=pltpu.CompilerParams(dimension_semantics=("arbitrary",), vmem_limit_bytes=VMEM_LIMIT),
        name="merge",
    )(*args)


_R_GROUP = N_EXPERTS


def _route(logits):
    lane = lax.broadcasted_iota(jnp.int32, (1, LANES), 1)
    lanef = lane.astype(jnp.float32)
    is_grp = (lane >= _R_GROUP) & (lane < _R_GROUP + N_GROUPS)
    lg = jnp.where(is_grp, logits, -jnp.inf)
    gmax = jnp.max(lg, axis=-1, keepdims=True)
    grp = jnp.min(jnp.where(lg == gmax, lanef, float(LANES)), axis=-1, keepdims=True) - _R_GROUP
    g1 = 1.0 / jnp.sum(jnp.where(is_grp, jnp.exp(lg - gmax), 0.0), axis=-1, keepdims=True)
    in_grp = (lane < N_EXPERTS) & ((lane // EXPERTS_PER_GROUP).astype(jnp.float32) == grp)
    le = jnp.where(in_grp, logits, -jnp.inf)
    m1 = jnp.max(le, axis=-1, keepdims=True)
    i1 = jnp.min(jnp.where(le == m1, lanef, float(LANES)), axis=-1, keepdims=True)
    le2 = jnp.where(lanef == i1, -jnp.inf, le)
    m2 = jnp.max(le2, axis=-1, keepdims=True)
    i2 = jnp.min(jnp.where(le2 == m2, lanef, float(LANES)), axis=-1, keepdims=True)
    t = jnp.exp(m2 - m1)
    w1 = g1 * (1.0 / (1.0 + t))
    w2 = g1 * (t / (1.0 + t))
    return jnp.where(lanef == i1, w1, jnp.where(lanef == i2, w2, 0.0))


def _moe_kernel(x_ref, g_ref, wr_ref, br_ref, wg_ref, wu_ref, wd_ref, out_ref, xn_sc, comb_sc, acc_sc):
    e = pl.program_id(1)

    @pl.when(e == 0)
    def _():
        x = x_ref[...]
        xn = _rms_rows(x, g_ref[...])
        logits = jnp.dot(xn, wr_ref[...], preferred_element_type=jnp.float32,
                         precision=lax.Precision.HIGHEST) + br_ref[...]
        comb_sc[...] = _route(logits)
        xn_sc[...] = xn.astype(jnp.bfloat16)
        acc_sc[...] = x

    xn = xn_sc[...]
    h = (jax.nn.silu(jnp.dot(xn, wg_ref[...], preferred_element_type=jnp.float32))
         * jnp.dot(xn, wu_ref[...], preferred_element_type=jnp.float32))
    y = jnp.dot(h.astype(jnp.bfloat16), wd_ref[...], preferred_element_type=jnp.float32)
    lane = lax.broadcasted_iota(jnp.int32, (1, LANES), 1)
    ce = jnp.sum(jnp.where(lane == e, comb_sc[...], 0.0), axis=-1, keepdims=True)
    acc_sc[...] += ce * y

    @pl.when(e == pl.num_programs(1) - 1)
    def _():
        out_ref[...] = acc_sc[...]


def _moe(x2d, norm_g, w_router, b_router, w_gate, w_up, w_down, *, tm):
    n = x2d.shape[0]
    f32 = jnp.float32
    return pl.pallas_call(
        _moe_kernel,
        grid=(n // tm, N_EXPERTS),
        in_specs=[pl.BlockSpec((tm, D_MODEL), lambda i, e: (i, 0)),
                  pl.BlockSpec((1, D_MODEL), lambda i, e: (0, 0)),
                  pl.BlockSpec((D_MODEL, LANES), lambda i, e: (0, 0)),
                  pl.BlockSpec((1, LANES), lambda i, e: (0, 0)),
                  pl.BlockSpec((None, D_MODEL, D_EXPERT), lambda i, e: (e, 0, 0)),
                  pl.BlockSpec((None, D_MODEL, D_EXPERT), lambda i, e: (e, 0, 0)),
                  pl.BlockSpec((None, D_EXPERT, D_MODEL), lambda i, e: (e, 0, 0))],
        out_specs=pl.BlockSpec((tm, D_MODEL), lambda i, e: (i, 0)),
        out_shape=jax.ShapeDtypeStruct(x2d.shape, x2d.dtype),
        scratch_shapes=[pltpu.VMEM((tm, D_MODEL), jnp.bfloat16), pltpu.VMEM((tm, LANES), f32),
                        pltpu.VMEM((tm, D_MODEL), f32)],
        compiler_params=pltpu.CompilerParams(dimension_semantics=("arbitrary", "arbitrary"),
                                             vmem_limit_bytes=VMEM_LIMIT),
        name="moe",
    )(x2d, norm_g, w_router, b_router, w_gate, w_up, w_down)


def _new_page(row, dtype):
    first = lax.broadcasted_iota(jnp.int32, (PAGE_SIZE, 1), 0) == 0
    return jnp.where(first, row, 0.0).astype(dtype)


def _nsa_sample_kernel(pt_ref, qbd_ref, gate_ref, new_ref, neww_ref, cw_ref, *rest, pg, n_pages):
    page_refs = rest[:pg]
    o_ref, ks_sc, vs_sc, cmpk_sc, cmpv_sc = rest[pg:]
    s = pl.program_id(1)
    n_tok = n_pages * PAGE_SIZE
    nb_s = n_tok // NSA_BLOCK + 1
    nbpad = cmpk_sc.shape[0]
    l_pad = ks_sc.shape[0]
    bpp = PAGE_SIZE // NSA_BLOCK
    bf16 = jnp.bfloat16

    for k in range(pg):
        page = page_refs[k][...]
        p = s * pg + k
        row0 = pl.multiple_of(p * PAGE_SIZE, PAGE_SIZE)
        ks_sc[pl.ds(row0, PAGE_SIZE), :] = page[:, 2 * LANES:3 * LANES].astype(bf16)
        vs_sc[pl.ds(row0, PAGE_SIZE), :] = page[:, 3 * LANES:4 * LANES].astype(bf16)
        cmpk_sc[pl.ds(p * bpp, bpp), :] = jnp.mean(page[:, 0:LANES].reshape(bpp, NSA_BLOCK, LANES), axis=1)
        cmpv_sc[pl.ds(p * bpp, bpp), :] = jnp.mean(page[:, LANES:2 * LANES].reshape(bpp, NSA_BLOCK, LANES), axis=1)

    @pl.when(s == pl.num_programs(1) - 1)
    def _():
        new = new_ref[...]
        ks_sc[n_tok:l_pad, :] = _new_page(new[:, 2 * LANES:3 * LANES], bf16)
        vs_sc[n_tok:l_pad, :] = _new_page(new[:, 3 * LANES:4 * LANES], bf16)
        tail = nbpad - (nb_s - 1)
        first = lax.broadcasted_iota(jnp.int32, (tail, 1), 0) == 0
        cmpk_sc[nb_s - 1:nbpad, :] = jnp.where(first, new[:, 0:LANES] * (1.0 / NSA_BLOCK), 0.0)
        cmpv_sc[nb_s - 1:nbpad, :] = jnp.where(first, new[:, LANES:2 * LANES] * (1.0 / NSA_BLOCK), 0.0)

        qbd = qbd_ref[...]
        nh = qbd.shape[0]
        qpos = n_tok
        blk = lax.broadcasted_iota(jnp.int32, (1, nbpad), 1)
        done = ((blk + 1) * NSA_BLOCK <= qpos + 1) & (blk < nb_s)
        s_c = lax.dot_general(qbd, cmpk_sc[...].astype(bf16), _NT, preferred_element_type=jnp.float32)
        p_c = _masked_softmax(s_c, done)
        o_c = jnp.dot(p_c.astype(bf16), cmpv_sc[...].astype(bf16), preferred_element_type=jnp.float32)
        imp = jnp.sum(p_c.reshape(NSA_KV_HEADS, NSA_GROUP, nbpad), axis=1)
        qpos_col = jnp.full((NSA_KV_HEADS, 1), qpos, jnp.int32)
        score = _block_importance_scores(imp, qpos_col, nbpad)
        score = jnp.where(blk < nb_s, score, -2.0)
        sel = _select_top_blocks(score, min(NSA_TOP_BLOCKS, nb_s))
        sel8 = jnp.concatenate([jnp.broadcast_to(sel[g:g + 1], (NSA_GROUP, nbpad)) for g in range(NSA_KV_HEADS)],
                               axis=0).astype(bf16)
        kpos = lax.broadcasted_iota(jnp.int32, (1, l_pad), 1)
        expand = lax.broadcasted_iota(jnp.int32, (nbpad, 1), 0) == kpos // NSA_BLOCK
        chosen = jnp.dot(sel8, jnp.where(expand, 1.0, 0.0).astype(bf16), preferred_element_type=jnp.float32) > 0.5
        s_s = lax.dot_general(qbd, ks_sc[...], _NT, preferred_element_type=jnp.float32)
        p_s = _masked_softmax(s_s, chosen & (kpos <= qpos))
        o_s = jnp.dot(p_s.astype(bf16), vs_sc[...], preferred_element_type=jnp.float32)

        cw = cw_ref[...].astype(bf16)
        nw = neww_ref[...].astype(bf16).astype(jnp.float32)
        s1 = lax.dot_general(qbd, cw[:, 0:LANES], _NT, preferred_element_type=jnp.float32)
        s2 = jnp.sum(qbd.astype(jnp.float32) * nw[:, 0:LANES], axis=-1, keepdims=True)
        m = jnp.maximum(jnp.max(s1, axis=-1, keepdims=True), s2)
        e1 = jnp.exp(s1 - m)
        e2 = jnp.exp(s2 - m)
        d = jnp.sum(e1, axis=-1, keepdims=True) + e2
        p1 = e1 / d
        p2 = (e2 / d).astype(bf16).astype(jnp.float32)
        o_w = jnp.dot(p1.astype(bf16), cw[:, LANES:2 * LANES], preferred_element_type=jnp.float32) + p2 * nw[:, LANES:2 * LANES]

        g = gate_ref[...]
        o_ref[...] = g[:, 0:1] * o_c + g[:, 1:2] * o_s + g[:, 2:3] * o_w


def _page_specs(cache, layer, pg, width):
    return [pl.BlockSpec((None, None, PAGE_SIZE, width),
                         functools.partial(lambda b, s, pt, k: (layer, pt[b, s * pg + k], 0, 0), k=k))
            for k in range(pg)]


def _seq_spec(shape):
    nd = len(shape)
    return pl.BlockSpec((None,) + tuple(shape[1:]), lambda b, s, pt: (b,) + (0,) * (nd - 1))


def _nsa_sample(page_table, qbd, gates, new_rows, new_win, cache_win_l, cache, layer, *, pg):
    bs, n_pages = page_table.shape
    n_tok = n_pages * PAGE_SIZE
    l_pad = n_tok + PAGE_SIZE
    nbpad = -(-(n_tok // NSA_BLOCK + 1) // 8) * 8
    f32, bf16 = jnp.float32, jnp.bfloat16
    fixed = (qbd, gates, new_rows, new_win, cache_win_l)
    grid_spec = pltpu.PrefetchScalarGridSpec(
        num_scalar_prefetch=1,
        grid=(bs, n_pages // pg),
        in_specs=[_seq_spec(a.shape) for a in fixed] + _page_specs(cache, layer, pg, 4 * LANES),
        out_specs=_seq_spec((bs, NSA_HEADS, LANES)),
        scratch_shapes=[pltpu.VMEM((l_pad, LANES), bf16), pltpu.VMEM((l_pad, LANES), bf16),
                        pltpu.VMEM((nbpad, LANES), f32), pltpu.VMEM((nbpad, LANES), f32)])
    return pl.pallas_call(
        functools.partial(_nsa_sample_kernel, pg=pg, n_pages=n_pages),
        grid_spec=grid_spec,
        out_shape=jax.ShapeDtypeStruct((bs, NSA_HEADS, LANES), f32),
        compiler_params=pltpu.CompilerParams(dimension_semantics=("arbitrary", "arbitrary"),
                                             vmem_limit_bytes=VMEM_LIMIT),
        name="nsa_sample",
    )(page_table, *fixed, *([cache] * pg))


def _fox_sample_kernel(pt_ref, qbd_ref, new_ref, newlf_ref, du_ref, *rest, pg, n_pages):
    page_refs = rest[:pg]
    lf_refs = rest[pg:2 * pg]
    o_ref, k_sc, v_sc, lf_sc, cum_sc, bias_sc = rest[2 * pg:]
    s = pl.program_id(1)
    n_tok = n_pages * PAGE_SIZE
    l_pad = k_sc.shape[0]
    bf16 = jnp.bfloat16
    hi = lax.Precision.HIGHEST
    w = FOX_HEADS * HEAD_DIM

    for k in range(pg):
        page = page_refs[k][...]
        p = s * pg + k
        row0 = pl.multiple_of(p * PAGE_SIZE, PAGE_SIZE)
        k_sc[pl.ds(row0, PAGE_SIZE), :] = page[:, 0:w].astype(bf16)
        v_sc[pl.ds(row0, PAGE_SIZE), :] = page[:, w:2 * w].astype(bf16)
        lf_sc[pl.ds(p, 1), :] = lf_refs[k][...]

    @pl.when(s == pl.num_programs(1) - 1)
    def _():
        new = new_ref[...]
        k_sc[n_tok:l_pad, :] = _new_page(new[:, 0:w], bf16)
        v_sc[n_tok:l_pad, :] = _new_page(new[:, w:2 * w], bf16)
        lf = lf_sc[...]
        rowi = lax.broadcasted_iota(jnp.int32, (n_pages, n_pages), 0)
        coli = lax.broadcasted_iota(jnp.int32, (n_pages, n_pages), 1)
        strict_lower = jnp.where(coli < rowi, 1.0, 0.0)
        kpos = lax.broadcasted_iota(jnp.int32, (1, l_pad), 1)
        for h in range(FOX_HEADS):
            within = jnp.dot(lf, du_ref[h], preferred_element_type=jnp.float32, precision=hi)
            before = jnp.dot(strict_lower, within, preferred_element_type=jnp.float32, precision=hi)
            cum = within + before[:, PAGE_SIZE - 1:PAGE_SIZE]
            cum_sc[...] = cum
            total = cum[n_pages - 1:n_pages, PAGE_SIZE - 1:PAGE_SIZE]
            cq = total + newlf_ref[:, h:h + 1]
            for p in range(n_pages):
                bias_sc[h:h + 1, p * PAGE_SIZE:(p + 1) * PAGE_SIZE] = cq - cum_sc[p:p + 1, :]
            bias_sc[h:h + 1, n_tok:l_pad] = jnp.zeros((1, l_pad - n_tok), jnp.float32)
        qbd = qbd_ref[...]
        sc = lax.dot_general(qbd, k_sc[...], _NT, preferred_element_type=jnp.float32) + bias_sc[...]
        pr = _masked_softmax(sc, kpos <= n_tok)
        o_ref[...] = jnp.dot(pr.astype(bf16), v_sc[...], preferred_element_type=jnp.float32)


def _fox_sample(page_table, qbd, new_rows, new_lf, cache, cache_lf, layer, *, pg):
    bs, n_pages = page_table.shape
    n_tok = n_pages * PAGE_SIZE
    l_pad = n_tok + PAGE_SIZE
    w = FOX_HEADS * HEAD_DIM
    f32, bf16 = jnp.float32, jnp.bfloat16
    lane = np.arange(PAGE_SIZE * FOX_HEADS)
    du = np.stack([((lane % FOX_HEADS == h)[:, None] & ((lane // FOX_HEADS)[:, None] <= np.arange(PAGE_SIZE)[None, :]))
                   for h in range(FOX_HEADS)]).astype(np.float32)
    du = jnp.asarray(du)
    fixed = (qbd, new_rows, new_lf)
    lf_specs = [pl.BlockSpec((None, None, 1, PAGE_SIZE * FOX_HEADS),
                             functools.partial(lambda b, s, pt, k: (layer, pt[b, s * pg + k], 0, 0), k=k))
                for k in range(pg)]
    grid_spec = pltpu.PrefetchScalarGridSpec(
        num_scalar_prefetch=1,
        grid=(bs, n_pages // pg),
        in_specs=[_seq_spec(a.shape) for a in fixed] + [pl.BlockSpec(du.shape, lambda b, s, pt: (0, 0, 0))]
        + _page_specs(cache, layer, pg, 2 * w) + lf_specs,
        out_specs=_seq_spec((bs, FOX_HEADS, w)),
        scratch_shapes=[pltpu.VMEM((l_pad, w), bf16), pltpu.VMEM((l_pad, w), bf16),
                        pltpu.VMEM((n_pages, PAGE_SIZE * FOX_HEADS), f32),
                        pltpu.VMEM((n_pages, PAGE_SIZE), f32), pltpu.VMEM((FOX_HEADS, l_pad), f32)])
    return pl.pallas_call(
        functools.partial(_fox_sample_kernel, pg=pg, n_pages=n_pages),
        grid_spec=grid_spec,
        out_shape=jax.ShapeDtypeStruct((bs, FOX_HEADS, w), f32),
        compiler_params=pltpu.CompilerParams(dimension_semantics=("arbitrary", "arbitrary"),
                                             vmem_limit_bytes=VMEM_LIMIT),
        name="fox_sample",
    )(page_table, *fixed, du, *([cache] * pg), *([cache_lf] * pg))


def _dsa_sample_kernel(pt_ref, qatt_ref, qidx_ref, iw_ref, new_ref, *rest, pg, n_pages, n_keep):
    page_refs = rest[:pg]
    o_ref, kv_sc, ik_sc = rest[pg:]
    s = pl.program_id(1)
    n_tok = n_pages * PAGE_SIZE
    l_pad = kv_sc.shape[0]
    bf16 = jnp.bfloat16
    hd = HEAD_DIM

    def split(page):
        ik = jnp.concatenate([page[:, 2 * hd:3 * hd], jnp.zeros((page.shape[0], hd), page.dtype)], axis=1)
        return page[:, 0:2 * hd].astype(bf16), ik.astype(bf16)

    for k in range(pg):
        p = s * pg + k
        row0 = pl.multiple_of(p * PAGE_SIZE, PAGE_SIZE)
        kv, ik = split(page_refs[k][...])
        kv_sc[pl.ds(row0, PAGE_SIZE), :] = kv
        ik_sc[pl.ds(row0, PAGE_SIZE), :] = ik

    @pl.when(s == pl.num_programs(1) - 1)
    def _():
        first = lax.broadcasted_iota(jnp.int32, (PAGE_SIZE, 1), 0) == 0
        kv, ik = split(jnp.where(first, new_ref[...], 0.0))
        kv_sc[n_tok:l_pad, :] = kv
        ik_sc[n_tok:l_pad, :] = ik
        kpos = lax.broadcasted_iota(jnp.int32, (1, l_pad), 1)
        causal = kpos <= n_tok
        a = lax.dot_general(qidx_ref[...], ik_sc[...], _NT, preferred_element_type=jnp.float32)
        a = jnp.maximum(a, 0.0) * iw_ref[...]
        sc = a[0:1]
        for h in range(1, DSA_IDX_HEADS):
            sc = sc + a[h:h + 1]
        key = _sortable_key(jnp.where(causal, sc, -jnp.inf))

        def count(mask):
            return jnp.sum(jnp.where(mask, 1.0, 0.0), axis=-1, keepdims=True)

        def bit_body(it, lo):
            cand = lo + lax.shift_left(jnp.int32(1), jnp.int32(31) - it)
            return jnp.where(count(key >= cand) >= n_keep, cand, lo)

        thr = lax.fori_loop(0, 32, bit_body, jnp.full((1, 1), _INT_MIN, jnp.int32))
        need = n_keep - count(key > thr)
        tie = jnp.where(key == thr, kpos, jnp.int32(2 ** 30))
        idx_bits = max(1, (l_pad - 1).bit_length())

        def idx_body(it, bound):
            step = lax.shift_left(jnp.int32(1), jnp.int32(idx_bits - 1) - it)
            return jnp.where(count(tie <= bound + step - 1) < need, bound + step, bound)

        bound = lax.fori_loop(0, idx_bits, idx_body, jnp.zeros((1, 1), jnp.int32))
        valid = ((key > thr) | (tie <= bound)) & causal
        s_att = lax.dot_general(qatt_ref[...], kv_sc[...], _NT, preferred_element_type=jnp.float32)
        pr = _masked_softmax(s_att, valid)
        o_ref[...] = jnp.dot(pr.astype(bf16), kv_sc[...], preferred_element_type=jnp.float32)


def _dsa_sample(page_table, q_att, q_idx, iw_col, new_rows, cache, layer, *, pg, n_keep):
    bs, n_pages = page_table.shape
    n_tok = n_pages * PAGE_SIZE
    l_pad = n_tok + PAGE_SIZE
    f32, bf16 = jnp.float32, jnp.bfloat16
    fixed = (q_att, q_idx, iw_col, new_rows)
    grid_spec = pltpu.PrefetchScalarGridSpec(
        num_scalar_prefetch=1,
        grid=(bs, n_pages // pg),
        in_specs=[_seq_spec(a.shape) for a in fixed] + _page_specs(cache, layer, pg, 3 * HEAD_DIM),
        out_specs=_seq_spec((bs, DSA_HEADS, LANES)),
        scratch_shapes=[pltpu.VMEM((l_pad, LANES), bf16), pltpu.VMEM((l_pad, LANES), bf16)])
    return pl.pallas_call(
        functools.partial(_dsa_sample_kernel, pg=pg, n_pages=n_pages, n_keep=n_keep),
        grid_spec=grid_spec,
        out_shape=jax.ShapeDtypeStruct((bs, DSA_HEADS, LANES), f32),
        compiler_params=pltpu.CompilerParams(dimension_semantics=("arbitrary", "arbitrary"),
                                             vmem_limit_bytes=VMEM_LIMIT),
        name="dsa_sample",
    )(page_table, *fixed, *([cache] * pg))


def _token0_page(col, dtype):
    first = lax.broadcasted_iota(jnp.int32, (1, PAGE_SIZE), 1) == 0
    return jnp.where(first, col, 0.0).astype(dtype)


def _hi_lo(x):
    hi = x.astype(jnp.bfloat16)
    return hi, (x - hi.astype(jnp.float32)).astype(jnp.bfloat16)


def _scores(q, pages_sc, n):
    return jnp.concatenate([jnp.dot(q, pages_sc[p], preferred_element_type=jnp.float32) for p in range(n)], axis=1)


def _weighted_values(p, pages_sc, n):
    p = p.astype(jnp.bfloat16)
    out = None
    for k in range(n):
        term = lax.dot_general(p[:, k * PAGE_SIZE:(k + 1) * PAGE_SIZE], pages_sc[k], _NT,
                               preferred_element_type=jnp.float32)
        out = term if out is None else out + term
    return out


def _split_dot_nt(a, b_bf16):
    hi = a.astype(jnp.bfloat16)
    lo = (a - hi.astype(jnp.float32)).astype(jnp.bfloat16)
    return (lax.dot_general(hi, b_bf16, _NT, preferred_element_type=jnp.float32)
            + lax.dot_general(lo, b_bf16, _NT, preferred_element_type=jnp.float32))


def _nsa_decode_kernel(pt_ref, qbd_ref, gate_ref, new_ref, neww_ref, cw_ref, *rest, pg, n_pages):
    page_refs = rest[:pg]
    o_ref, kc_sc, kcl_sc, vc_sc, ks_sc, vs_sc = rest[pg:]
    s = pl.program_id(1)
    n_tok = n_pages * PAGE_SIZE
    l_pad = n_tok + PAGE_SIZE
    nb_s = n_tok // NSA_BLOCK + 1
    nbpad = -(-nb_s // 8) * 8
    bf16 = jnp.bfloat16
    nh = NSA_HEADS

    for k in range(pg):
        page = page_refs[k][...]
        p = s * pg + k
        kc_sc[p], kcl_sc[p] = _hi_lo(page[0 * LANES:1 * LANES])
        vc_sc[p] = page[1 * LANES:2 * LANES].astype(bf16)
        ks_sc[p] = page[2 * LANES:3 * LANES].astype(bf16)
        vs_sc[p] = page[3 * LANES:4 * LANES].astype(bf16)

    @pl.when(s == pl.num_programs(1) - 1)
    def _():
        new = new_ref[...]
        kc_sc[n_pages], kcl_sc[n_pages] = _hi_lo(_token0_page(new[0 * LANES:1 * LANES], jnp.float32))
        vc_sc[n_pages] = _token0_page(new[1 * LANES:2 * LANES], bf16)
        ks_sc[n_pages] = _token0_page(new[2 * LANES:3 * LANES], bf16)
        vs_sc[n_pages] = _token0_page(new[3 * LANES:4 * LANES], bf16)
        n_all = n_pages + 1
        q_hl = qbd_ref[...]
        qbd = q_hl[0:nh]
        qpos = n_tok
        kpos = lax.broadcasted_iota(jnp.int32, (1, l_pad), 1)
        blk = lax.broadcasted_iota(jnp.int32, (1, nbpad), 1)
        incid = jnp.where(lax.broadcasted_iota(jnp.int32, (nbpad, 1), 0) == kpos // NSA_BLOCK, 1.0, 0.0).astype(bf16)
        inv = 1.0 / NSA_BLOCK

        s_hl = _scores(q_hl, kc_sc, n_all)
        s_tok = s_hl[0:nh] + s_hl[nh:2 * nh] + _scores(qbd, kcl_sc, n_all)
        s_c = _split_dot_nt(s_tok, incid) * inv
        done = ((blk + 1) * NSA_BLOCK <= qpos + 1) & (blk < nb_s)
        p_c = _masked_softmax(s_c, done)
        p_tok = jnp.dot(p_c.astype(bf16), incid, preferred_element_type=jnp.float32) * inv
        o_c = _weighted_values(p_tok, vc_sc, n_all)

        imp = jnp.sum(p_c.reshape(NSA_KV_HEADS, NSA_GROUP, nbpad), axis=1)
        score = _block_importance_scores(imp, jnp.full((NSA_KV_HEADS, 1), qpos, jnp.int32), nbpad)
        score = jnp.where(blk < nb_s, score, -2.0)
        sel = _select_top_blocks(score, min(NSA_TOP_BLOCKS, nb_s))
        sel8 = jnp.concatenate([jnp.broadcast_to(sel[g:g + 1], (NSA_GROUP, nbpad)) for g in range(NSA_KV_HEADS)],
                               axis=0).astype(bf16)
        chosen = jnp.dot(sel8, incid, preferred_element_type=jnp.float32) > 0.5
        p_s = _masked_softmax(_scores(qbd, ks_sc, n_all), chosen & (kpos <= qpos))
        o_s = _weighted_values(p_s, vs_sc, n_all)

        cw = cw_ref[...].astype(bf16)
        nw = neww_ref[...]
        win_keep = cw.shape[1]
        s_w = jnp.concatenate([jnp.dot(qbd, cw[0:LANES], preferred_element_type=jnp.float32),
                               jnp.dot(qbd, _token0_page(nw[0:LANES], bf16), preferred_element_type=jnp.float32)],
                              axis=1)
        wpos = lax.broadcasted_iota(jnp.int32, (1, win_keep + PAGE_SIZE), 1)
        p_w = _masked_softmax(s_w, wpos <= win_keep).astype(bf16)
        o_w = (lax.dot_general(p_w[:, :win_keep], cw[LANES:2 * LANES], _NT, preferred_element_type=jnp.float32)
               + lax.dot_general(p_w[:, win_keep:], _token0_page(nw[LANES:2 * LANES], bf16), _NT,
                                 preferred_element_type=jnp.float32))

        g = gate_ref[...]
        o_ref[...] = g[:, 0:1] * o_c + g[:, 1:2] * o_s + g[:, 2:3] * o_w


def _page_specs_t(layer, pg, rows):
    return [pl.BlockSpec((None, None, rows, PAGE_SIZE),
                         functools.partial(lambda b, s, pt, k: (layer, pt[b, s * pg + k], 0, 0), k=k))
            for k in range(pg)]


def _decode_call(kernel_fn, name, page_table, fixed, const, caches, layer, out_tail, scratch, *, pg):
    bs, n_pages = page_table.shape
    const_specs = [pl.BlockSpec(a.shape, functools.partial(lambda b, s, pt, nd: (0,) * nd, nd=a.ndim)) for a in const]
    page_specs = []
    for cache in caches:
        page_specs += _page_specs_t(layer, pg, cache.shape[2])
    grid_spec = pltpu.PrefetchScalarGridSpec(
        num_scalar_prefetch=1,
        grid=(bs, n_pages // pg),
        in_specs=[_seq_spec(a.shape) for a in fixed] + const_specs + page_specs,
        out_specs=_seq_spec((bs,) + out_tail),
        scratch_shapes=scratch)
    operands = list(fixed) + list(const)
    for cache in caches:
        operands += [cache] * pg
    return pl.pallas_call(
        kernel_fn,
        grid_spec=grid_spec,
        out_shape=jax.ShapeDtypeStruct((bs,) + out_tail, jnp.float32),
        compiler_params=pltpu.CompilerParams(dimension_semantics=("arbitrary", "arbitrary"),
                                             vmem_limit_bytes=VMEM_LIMIT),
        name=name,
    )(page_table, *operands)


def _nsa_decode(page_table, qbd, gates, new_t, neww_t, cache_win_t, cache_t, layer, *, pg):
    n_pages = page_table.shape[1]
    buf = pltpu.VMEM((n_pages + 1, LANES, PAGE_SIZE), jnp.bfloat16)
    return _decode_call(functools.partial(_nsa_decode_kernel, pg=pg, n_pages=n_pages), "nsa_sample", page_table,
                        (qbd, gates, new_t, neww_t, cache_win_t), (), (cache_t,), layer, (NSA_HEADS, LANES),
                        [buf, buf, buf, buf, buf], pg=pg)


def _fox_decode_kernel(pt_ref, qbd_ref, new_ref, newlf_ref, *rest, pg, n_pages):
    page_refs = rest[:pg]
    lf_refs = rest[pg:2 * pg]
    o_ref, k_sc, v_sc, lf_sc, cum_sc = rest[2 * pg:]
    s = pl.program_id(1)
    n_tok = n_pages * PAGE_SIZE
    l_pad = n_tok + PAGE_SIZE
    bf16 = jnp.bfloat16
    hi = lax.Precision.HIGHEST
    w = FOX_HEADS * HEAD_DIM

    for k in range(pg):
        page = page_refs[k][...]
        p = s * pg + k
        k_sc[p] = page[0:w].astype(bf16)
        v_sc[p] = page[w:2 * w].astype(bf16)
        lf = lf_refs[k][...]
        for h in range(FOX_HEADS):
            lf_sc[h, pl.ds(p, 1), :] = lf[h:h + 1, :]

    @pl.when(s == pl.num_programs(1) - 1)
    def _():
        new = new_ref[...]
        k_sc[n_pages] = _token0_page(new[0:w], bf16)
        v_sc[n_pages] = _token0_page(new[w:2 * w], bf16)
        n_all = n_pages + 1
        iota2 = lambda n, axis: lax.broadcasted_iota(jnp.int32, (n, n), axis)
        upper_incl = jnp.where(iota2(PAGE_SIZE, 0) <= iota2(PAGE_SIZE, 1), 1.0, 0.0)
        lower_strict = jnp.where(iota2(n_pages, 1) < iota2(n_pages, 0), 1.0, 0.0)
        cqs = []
        for h in range(FOX_HEADS):
            within = jnp.dot(lf_sc[h], upper_incl, preferred_element_type=jnp.float32, precision=hi)
            before = jnp.dot(lower_strict, within, preferred_element_type=jnp.float32, precision=hi)
            cum = within + before[:, PAGE_SIZE - 1:PAGE_SIZE]
            cum_sc[h] = cum
            cqs.append(cum[n_pages - 1:n_pages, PAGE_SIZE - 1:PAGE_SIZE] + newlf_ref[:, h:h + 1])
        cq = jnp.concatenate(cqs, axis=0)
        qbd = qbd_ref[...]
        parts = []
        for p in range(n_pages):
            ck = jnp.concatenate([cum_sc[h, p:p + 1, :] for h in range(FOX_HEADS)], axis=0)
            parts.append(jnp.dot(qbd, k_sc[p], preferred_element_type=jnp.float32) + (cq - ck))
        parts.append(jnp.dot(qbd, k_sc[n_pages], preferred_element_type=jnp.float32))
        sc = jnp.concatenate(parts, axis=1)
        kpos = lax.broadcasted_iota(jnp.int32, (1, l_pad), 1)
        o_ref[...] = _weighted_values(_masked_softmax(sc, kpos <= n_tok), v_sc, n_all)


def _fox_decode(page_table, qbd, new_t, new_lf, cache_t, cache_lf_t, layer, *, pg):
    n_pages = page_table.shape[1]
    w = FOX_HEADS * HEAD_DIM
    buf = pltpu.VMEM((n_pages + 1, w, PAGE_SIZE), jnp.bfloat16)
    lfbuf = pltpu.VMEM((FOX_HEADS, n_pages, PAGE_SIZE), jnp.float32)
    return _decode_call(functools.partial(_fox_decode_kernel, pg=pg, n_pages=n_pages), "fox_sample", page_table,
                        (qbd, new_t, new_lf), (), (cache_t, cache_lf_t), layer, (FOX_HEADS, w),
                        [buf, buf, lfbuf, lfbuf], pg=pg)


def _dsa_decode_kernel(pt_ref, qatt_ref, qidx_ref, iw_ref, new_ref, *rest, pg, n_pages, n_keep):
    page_refs = rest[:pg]
    o_ref, kv_sc, ik_sc, ikl_sc = rest[pg:]
    s = pl.program_id(1)
    n_tok = n_pages * PAGE_SIZE
    l_pad = n_tok + PAGE_SIZE
    bf16 = jnp.bfloat16
    hd = HEAD_DIM
    nh = DSA_IDX_HEADS

    for k in range(pg):
        page = page_refs[k][...]
        p = s * pg + k
        kv_sc[p] = page[0:2 * hd].astype(bf16)
        ik_sc[p], ikl_sc[p] = _hi_lo(page[2 * hd:3 * hd])

    @pl.when(s == pl.num_programs(1) - 1)
    def _():
        new = new_ref[...]
        kv_sc[n_pages] = _token0_page(new[0:2 * hd], bf16)
        ik_sc[n_pages], ikl_sc[n_pages] = _hi_lo(_token0_page(new[2 * hd:3 * hd], jnp.float32))
        n_all = n_pages + 1
        kpos = lax.broadcasted_iota(jnp.int32, (1, l_pad), 1)
        causal = kpos <= n_tok
        q_hl = qidx_ref[...]
        a_hl = _scores(q_hl, ik_sc, n_all)
        a = a_hl[0:nh] + a_hl[nh:2 * nh] + _scores(q_hl[0:nh], ikl_sc, n_all)
        a = jnp.maximum(a, 0.0) * iw_ref[...]
        sc = a[0:1]
        for h in range(1, DSA_IDX_HEADS):
            sc = sc + a[h:h + 1]
        key = _sortable_key(jnp.where(causal, sc, -jnp.inf))

        def count(mask):
            return jnp.sum(jnp.where(mask, 1.0, 0.0), axis=-1, keepdims=True)

        def bit_body(it, lo):
            cand = lo + lax.shift_left(jnp.int32(1), jnp.int32(31) - it)
            return jnp.where(count(key >= cand) >= n_keep, cand, lo)

        thr = lax.fori_loop(0, 32, bit_body, jnp.full((1, 1), _INT_MIN, jnp.int32))
        need = n_keep - count(key > thr)
        tie = jnp.where(key == thr, kpos, jnp.int32(2 ** 30))
        idx_bits = max(1, (l_pad - 1).bit_length())

        def idx_body(it, bound):
            step = lax.shift_left(jnp.int32(1), jnp.int32(idx_bits - 1) - it)
            return jnp.where(count(tie <= bound + step - 1) < need, bound + step, bound)

        bound = lax.fori_loop(0, idx_bits, idx_body, jnp.zeros((1, 1), jnp.int32))
        valid = ((key > thr) | (tie <= bound)) & causal
        pr = _masked_softmax(_scores(qatt_ref[...], kv_sc, n_all), valid)
        o_ref[...] = _weighted_values(pr, kv_sc, n_all)


def _dsa_decode(page_table, q_att, q_idx, iw_col, new_t, cache_t, layer, *, pg, n_keep):
    n_pages = page_table.shape[1]
    hd = HEAD_DIM
    return _decode_call(functools.partial(_dsa_decode_kernel, pg=pg, n_pages=n_pages, n_keep=n_keep), "dsa_sample",
                        page_table, (q_att, q_idx, iw_col, new_t), (), (cache_t,), layer, (DSA_HEADS, LANES),
                        [pltpu.VMEM((n_pages + 1, 2 * hd, PAGE_SIZE), jnp.bfloat16),
                         pltpu.VMEM((n_pages + 1, hd, PAGE_SIZE), jnp.bfloat16),
                         pltpu.VMEM((n_pages + 1, hd, PAGE_SIZE), jnp.bfloat16)], pg=pg)


def _merge_weights(w_bn, w_bf, w_bd, w_o):
    hd = HEAD_DIM
    rows = []
    for c in range(NSA_GROUP):
        rows += list(range(hd * c, hd * c + hd)) + list(range(hd * (c + NSA_GROUP), hd * (c + NSA_GROUP) + hd))
    w_bn_p = jnp.take(w_bn, np.asarray(rows, np.int32), axis=0).astype(jnp.bfloat16)
    w_bd_p = jnp.pad(w_bd.reshape(DSA_HEADS, hd, D_MODEL), ((0, 0), (0, hd), (0, 0)))
    w_bd_p = w_bd_p.reshape(DSA_HEADS * LANES, D_MODEL).astype(jnp.bfloat16)
    return w_bn_p, w_bf.astype(jnp.bfloat16), w_bd_p, w_o.astype(jnp.bfloat16)


def _router_weights(w_rg, b_rg, w_re, b_re):
    pad = LANES - N_EXPERTS - N_GROUPS
    w = jnp.concatenate([w_re, w_rg, jnp.zeros((D_MODEL, pad), w_re.dtype)], axis=1).astype(jnp.float32)
    b = jnp.concatenate([b_re, b_rg, jnp.zeros((pad,), b_re.dtype)]).astype(jnp.float32)[None, :]
    return w, b


TM_PROJ = 512
TQ_FOX = 512
TQ_NSA = 256
TQ_DSA = 256
TK_ATTN = 1024
TM_MERGE = 512
TM_MOE = 1024
PAGES_PER_STEP = 16


def _hi_lo_rows(q):
    hi = q.astype(jnp.bfloat16)
    lo = (q - hi.astype(jnp.float32)).astype(jnp.bfloat16)
    return jnp.concatenate([hi, lo], axis=1)


def _dsa_rows(dki, misc):
    return jnp.concatenate([dki[:, :HEAD_DIM], misc[:, :HEAD_DIM], dki[:, HEAD_DIM:]], axis=-1)


def kernel(x_prompt, x_sample, cache_nsa, cache_fox, cache_fox_logf, cache_dsa, cache_win, page_table, norm_attn_g, w_in, nsa_qk_g, fox_qk_g, fox_b_f, dsa_qk_g, w_branch_nsa, w_branch_fox, w_branch_dsa, w_out, norm_ffn_g, w_router_group, b_router_group, w_router_expert, b_router_expert, w_exp_gate, w_exp_up, w_exp_down):
    depth = w_in.shape[0]
    B, S, D = x_prompt.shape
    Bs, T, _ = x_sample.shape
    assert T == 1 and D == D_MODEL
    n_pages = page_table.shape[1]
    past_len = n_pages * PAGE_SIZE
    win_keep = cache_win.shape[2]
    n_phys = cache_nsa.shape[1]
    bf16 = jnp.bfloat16
    hd = HEAD_DIM

    tm_proj = min(TM_PROJ, S)
    cos_p, sin_p = _rope_tables(jnp.arange(S))
    cos_s, sin_s = _rope_tables(jnp.full((Bs,), past_len))
    pg = min(PAGES_PER_STEP, n_pages)
    c_nsa = jnp.moveaxis(cache_nsa.reshape(depth, n_phys, PAGE_SIZE, 4 * LANES), 2, 3)
    c_fox = jnp.moveaxis(cache_fox.reshape(depth, n_phys, PAGE_SIZE, 2 * FOX_HEADS * hd), 2, 3)
    c_lf = jnp.moveaxis(cache_fox_logf.astype(jnp.float32), 2, 3)
    c_dsa = jnp.moveaxis(cache_dsa.reshape(depth, n_phys, PAGE_SIZE, 3 * hd), 2, 3)
    c_win = jnp.moveaxis(cache_win.reshape(depth, Bs, win_keep, 2 * LANES), 2, 3)
    lane = jnp.arange(LANES)
    left = lane < hd

    xp = x_prompt.reshape(B * S, D)
    xs = x_sample.reshape(Bs, D)
    outs = [[] for _ in range(10)]
    for l in range(depth):
        w_perm, w_mg, gain_row, bf_row = _layer_tables(w_in[l], nsa_qk_g[l], fox_qk_g[l], fox_b_f[l], dsa_qk_g[l])
        w_b = _merge_weights(w_branch_nsa[l], w_branch_fox[l], w_branch_dsa[l], w_out[l])
        w_r, b_r = _router_weights(w_router_group[l], b_router_group[l], w_router_expert[l], b_router_expert[l])
        w_e = (w_exp_gate[l].astype(bf16), w_exp_up[l].astype(bf16), w_exp_down[l].astype(bf16))
        g_attn = norm_attn_g[l][None, :]
        g_ffn = norm_ffn_g[l][None, :]

        (qn, nsa_rows, win_rows, slc_kv, win_kv, cmp, qf, fox_rows, fox_kv, dqi, dki, dki_bf, misc, misc_bf,
         cum) = _project(xp, g_attn, w_perm.astype(bf16), gain_row, cos_p, sin_p, bf_row,
                         tm=tm_proj, tiles_per_batch=S // tm_proj, with_blocks=True)
        cum_t = cum[:, _M_FF:_M_IW].reshape(B, S, FOX_HEADS).transpose(0, 2, 1)
        o_fox = _fox_prompt(qf, fox_kv, cum, cum_t, batch=B, seq=S, tq=min(TQ_FOX, S), tk=min(TK_ATTN, S))
        o_nsa = _nsa_prompt(qn, misc, cmp.astype(bf16), slc_kv, win_kv, batch=B, seq=S, tq=min(TQ_NSA, S),
                            tk=min(TK_ATTN, S))
        o_dsa = _dsa_prompt(dqi, misc, dki_bf, misc_bf, batch=B, seq=S, tq=min(TQ_DSA, S), tk=min(TK_ATTN, S),
                            n_keep=min(DSA_TOPK, S // 4))
        xp = _merge(xp, g_attn, o_nsa, o_fox, o_dsa, w_mg, *w_b, tm=min(TM_MERGE, B * S))
        xp = _moe(xp, g_ffn, w_r, b_r, *w_e, tm=min(TM_MOE, B * S))
        outs[0].append(nsa_rows.reshape(B, S, 4, NSA_KV_HEADS, hd))
        outs[2].append(fox_rows.reshape(B, S, 2, FOX_HEADS, hd))
        outs[4].append(misc[:, _M_FF:_M_IW].reshape(B, S, FOX_HEADS))
        outs[6].append(_dsa_rows(dki, misc).reshape(B, S, 3, hd))
        outs[8].append(win_rows.reshape(B, S, 2, NSA_KV_HEADS, hd)[:, S - min(NSA_WINDOW, S):])

        (qn, nsa_rows, win_rows, _, _, _, qf, fox_rows, _, dqi, dki, _, misc, _, _) = _project(
            xs, g_attn, w_perm, gain_row, cos_s, sin_s, bf_row, tm=Bs, tiles_per_batch=1, with_blocks=False)
        chunks = qn.reshape(Bs, NSA_GROUP, LANES)
        qbd = _hi_lo_rows(jnp.concatenate([jnp.where(left, chunks, 0), jnp.where(left, 0, chunks)], axis=1))
        gates = misc[:, _M_GATE:_M_FF].reshape(Bs, 3, NSA_HEADS).transpose(0, 2, 1)
        gates = jnp.pad(gates, ((0, 0), (0, 0), (0, LANES - 3)))
        o = _nsa_decode(page_table, qbd, gates, nsa_rows[:, :, None], win_rows[:, :, None], c_win[l], c_nsa, l, pg=pg)
        o_nsa = jnp.where(left, o[:, :NSA_GROUP], o[:, NSA_GROUP:]).reshape(Bs, 4 * LANES).astype(bf16)

        head_of_lane = jnp.arange(FOX_HEADS * hd) // hd
        qbd_f = jnp.where(head_of_lane[None, None, :] == jnp.arange(FOX_HEADS)[None, :, None], qf[:, None, :], 0)
        qbd_f = qbd_f.astype(bf16)
        o = _fox_decode(page_table, qbd_f, fox_rows[:, :, None], misc[:, None, _M_FF:_M_IW], c_fox, c_lf, l, pg=pg)
        o_fox = jnp.einsum('bhhd->bhd', o.reshape(Bs, FOX_HEADS, FOX_HEADS, hd)).reshape(Bs, FOX_HEADS * hd).astype(bf16)

        chunks = dqi.reshape(Bs, DSA_HEADS, LANES)
        q_att = jnp.where(left, chunks, 0).astype(bf16)
        q_idx = _hi_lo_rows(chunks[..., hd:])
        dsa_new = _dsa_rows(dki, misc)
        o = _dsa_decode(page_table, q_att, q_idx, misc[:, _M_IW:_M_END, None], dsa_new[:, :, None], c_dsa, l,
                        pg=pg, n_keep=min(DSA_TOPK, (past_len + 1) // 4))
        o_dsa = jnp.concatenate([o[..., hd:], jnp.zeros_like(o[..., hd:])], axis=-1).reshape(Bs, 4 * LANES).astype(bf16)

        xs = _merge(xs, g_attn, o_nsa, o_fox, o_dsa, w_mg, *w_b, tm=Bs)
        xs = _moe(xs, g_ffn, w_r, b_r, *w_e, tm=Bs)
        outs[1].append(nsa_rows.reshape(Bs, 1, 4, NSA_KV_HEADS, hd))
        outs[3].append(fox_rows.reshape(Bs, 1, 2, FOX_HEADS, hd))
        outs[5].append(misc[:, _M_FF:_M_IW].reshape(Bs, 1, FOX_HEADS))
        outs[7].append(dsa_new.reshape(Bs, 1, 3, hd))
        win_all = jnp.concatenate([cache_win[l], win_rows.reshape(Bs, 1, 2, NSA_KV_HEADS, hd)], axis=1)
        outs[9].append(win_all[:, 1:])
    return (xp.reshape(B, S, D), xs.reshape(Bs, 1, D)) + tuple(jnp.stack(o) for o in outs)
```

```python
import functools

import numpy as np
import jax
import jax.numpy as jnp
from jax import lax
from jax.experimental import pallas as pl
from jax.experimental.pallas import tpu as pltpu

D_MODEL = 1024
HEAD_DIM = 64
HALF = HEAD_DIM // 2
NSA_HEADS = 8
NSA_KV_HEADS = 2
NSA_GROUP = NSA_HEADS // NSA_KV_HEADS
NSA_BLOCK = 64
NSA_TOP_BLOCKS = 16
NSA_WINDOW = 512
NSA_FORCE_BONUS = 8.0
FOX_HEADS = 4
DSA_HEADS = 4
DSA_IDX_HEADS = 4
DSA_TOPK = 256
N_GROUPS = 4
EXPERTS_PER_GROUP = 4
N_EXPERTS = N_GROUPS * EXPERTS_PER_GROUP
D_EXPERT = 512
ROPE_THETA = 10000.0
NORM_EPS = 1e-6
NEG_BIG = -1e30
PAGE_SIZE = 128
QK_SCALE = HEAD_DIM ** -0.5

LANES = 128
VMEM_LIMIT = 56 * 1024 * 1024

_O_NQ = 0
_O_NKV = _O_NQ + NSA_HEADS * HEAD_DIM
_O_NGATE = _O_NKV + 6 * NSA_KV_HEADS * HEAD_DIM
_O_FQKV = _O_NGATE + 3 * NSA_HEADS
_O_FF = _O_FQKV + 3 * FOX_HEADS * HEAD_DIM
_O_DQ = _O_FF + FOX_HEADS
_O_DKV = _O_DQ + DSA_HEADS * HEAD_DIM
_O_DIQ = _O_DKV + 2 * HEAD_DIM
_O_DIK = _O_DIQ + DSA_IDX_HEADS * HEAD_DIM
_O_DIW = _O_DIK + HEAD_DIM
_O_MG = _O_DIW + DSA_IDX_HEADS
D_IN = _O_MG + 3 * D_MODEL

_M_GATE = HEAD_DIM
_M_FF = _M_GATE + 3 * NSA_HEADS
_M_IW = _M_FF + FOX_HEADS
_M_END = _M_IW + DSA_IDX_HEADS

_C_NQ = 0
_C_NKV = 4
_C_FQ = 10
_C_FK = 12
_C_FV = 14
_C_DQI = 16
_C_DKI = 20
_C_MISC = 21
N_CHUNKS = 22
D_PROJ = N_CHUNKS * LANES


def _proj_column_permutation():
    idx = []
    for c in range(4):
        idx += list(range(_O_NQ + 64 * c, _O_NQ + 64 * c + 64))
        idx += list(range(_O_NQ + 64 * (c + 4), _O_NQ + 64 * (c + 4) + 64))
    idx += list(range(_O_NKV, _O_NKV + 768))
    idx += list(range(_O_FQKV, _O_FQKV + 768))
    for h in range(4):
        idx += list(range(_O_DQ + 64 * h, _O_DQ + 64 * h + 64))
        idx += list(range(_O_DIQ + 64 * h, _O_DIQ + 64 * h + 64))
    idx += list(range(_O_DKV, _O_DKV + 64)) + list(range(_O_DIK, _O_DIK + 64))
    idx += list(range(_O_DKV + 64, _O_DKV + 128))
    idx += list(range(_O_NGATE, _O_NGATE + 24)) + list(range(_O_FF, _O_FF + 4)) + list(range(_O_DIW, _O_DIW + 4))
    idx += [D_IN] * (LANES - _M_END)
    assert len(idx) == D_PROJ
    return np.asarray(idx, np.int32)


def _rope_tables(pos):
    inv = ROPE_THETA ** (-jnp.arange(HALF, dtype=jnp.float32) * 2.0 / HEAD_DIM)
    ang = pos.astype(jnp.float32)[:, None] * inv[None, :]
    cos, sin = jnp.cos(ang), jnp.sin(ang)
    cos = jnp.concatenate([cos, cos, cos, cos], axis=-1)
    sin = jnp.concatenate([-sin, sin, -sin, sin], axis=-1)
    return cos, sin


def _rms_rows(x, g):
    return x * lax.rsqrt(jnp.mean(jnp.square(x), axis=-1, keepdims=True) + NORM_EPS) * g


def _split_dot(a, b_bf16):
    hi = a.astype(jnp.bfloat16)
    lo = (a - hi.astype(jnp.float32)).astype(jnp.bfloat16)
    return (jnp.dot(hi, b_bf16, preferred_element_type=jnp.float32)
            + jnp.dot(lo, b_bf16, preferred_element_type=jnp.float32))


def _proj_kernel(x_ref, g_ref, w_ref, gain_ref, cos_ref, sin_ref, bd_ref, bf_ref, tri_ref,
                 qn_ref, nsa_rows_ref, win_rows_ref, slc_kv_ref, win_kv_ref, cmp_ref,
                 qf_ref, fox_rows_ref, fox_kv_ref, dqi_ref, dki_ref, dki_bf_ref,
                 misc_ref, misc_bf_ref, cum_ref, carry_ref, *, tiles_per_batch, with_blocks):
    i = pl.program_id(0)
    precise = w_ref.dtype == jnp.float32
    xn = _rms_rows(x_ref[...], g_ref[...])
    if not precise:
        xn = xn.astype(jnp.bfloat16)
    lane = lax.broadcasted_iota(jnp.int32, (1, LANES), 1)
    left = lane < HEAD_DIM
    first_half = (lane % HEAD_DIM) < HALF
    cos = cos_ref[...]
    sin = sin_ref[...]
    bd = bd_ref[...]

    def chunk(c):
        return jnp.dot(xn, w_ref[:, c * LANES:(c + 1) * LANES], preferred_element_type=jnp.float32,
                       precision=lax.Precision.HIGHEST if precise else None)

    def head_norm(h, c, only_left=False):
        ms = _split_dot(h * h, bd)
        y = h * lax.rsqrt(ms + NORM_EPS) * gain_ref[:, c * LANES:(c + 1) * LANES]
        return jnp.where(left, y, h) if only_left else y

    def rope(h):
        swapped = jnp.where(first_half, pltpu.roll(h, LANES - HALF, 1), pltpu.roll(h, HALF, 1))
        return h * cos + swapped * sin

    for c in range(4):
        q = rope(head_norm(chunk(_C_NQ + c), _C_NQ + c)) * QK_SCALE
        qn_ref[:, c * LANES:(c + 1) * LANES] = q.astype(qn_ref.dtype)

    kc = rope(head_norm(chunk(_C_NKV + 0), _C_NKV + 0))
    vc = chunk(_C_NKV + 1)
    ks = rope(head_norm(chunk(_C_NKV + 2), _C_NKV + 2))
    vs = chunk(_C_NKV + 3)
    kw = rope(head_norm(chunk(_C_NKV + 4), _C_NKV + 4))
    vw = chunk(_C_NKV + 5)
    nsa_rows_ref[:, 0 * LANES:1 * LANES] = kc
    nsa_rows_ref[:, 1 * LANES:2 * LANES] = vc
    nsa_rows_ref[:, 2 * LANES:3 * LANES] = ks
    nsa_rows_ref[:, 3 * LANES:4 * LANES] = vs
    win_rows_ref[:, 0:LANES] = kw
    win_rows_ref[:, LANES:2 * LANES] = vw
    slc_kv_ref[:, 0:LANES] = ks.astype(jnp.bfloat16)
    slc_kv_ref[:, LANES:2 * LANES] = vs.astype(jnp.bfloat16)
    win_kv_ref[:, 0:LANES] = kw.astype(jnp.bfloat16)
    win_kv_ref[:, LANES:2 * LANES] = vw.astype(jnp.bfloat16)
    if with_blocks:
        tm = kc.shape[0]
        nblk = tm // NSA_BLOCK
        cmp_ref[:, 0:LANES] = jnp.mean(kc.reshape(nblk, NSA_BLOCK, LANES), axis=1)
        cmp_ref[:, LANES:2 * LANES] = jnp.mean(vc.reshape(nblk, NSA_BLOCK, LANES), axis=1)
    else:
        cmp_ref[...] = jnp.zeros_like(cmp_ref)

    for c in range(2):
        qf = head_norm(chunk(_C_FQ + c), _C_FQ + c) * QK_SCALE
        qf_ref[:, c * LANES:(c + 1) * LANES] = qf.astype(qf_ref.dtype)
        fk = head_norm(chunk(_C_FK + c), _C_FK + c)
        fv = chunk(_C_FV + c)
        fox_rows_ref[:, c * LANES:(c + 1) * LANES] = fk
        fox_rows_ref[:, (2 + c) * LANES:(3 + c) * LANES] = fv
        fox_kv_ref[:, c * LANES:(c + 1) * LANES] = fk.astype(jnp.bfloat16)
        fox_kv_ref[:, (2 + c) * LANES:(3 + c) * LANES] = fv.astype(jnp.bfloat16)

    for c in range(4):
        dqi = rope(head_norm(chunk(_C_DQI + c), _C_DQI + c, only_left=True)) * QK_SCALE
        dqi_ref[:, c * LANES:(c + 1) * LANES] = dqi.astype(dqi_ref.dtype)

    dki = rope(head_norm(chunk(_C_DKI), _C_DKI, only_left=True))
    dki_ref[...] = dki
    dki_bf_ref[...] = dki.astype(jnp.bfloat16)

    m = chunk(_C_MISC)
    zf = m + bf_ref[...]
    logf = jnp.minimum(zf, 0.0) - jnp.log(1.0 + jnp.exp(-jnp.abs(zf)))
    is_gate = (lane >= _M_GATE) & (lane < _M_FF)
    is_ff = (lane >= _M_FF) & (lane < _M_IW)
    is_iw = (lane >= _M_IW) & (lane < _M_END)
    out = jnp.where(is_gate, jax.nn.sigmoid(m), m)
    out = jnp.where(is_ff, logf, out)
    out = jnp.where(is_iw, m * (DSA_IDX_HEADS ** -0.5), out)
    misc_ref[...] = out
    misc_bf_ref[...] = out.astype(jnp.bfloat16)

    @pl.when(i % tiles_per_batch == 0)
    def _():
        carry_ref[...] = jnp.zeros_like(carry_ref)

    lf = jnp.where(is_ff, logf, 0.0)
    cum = jnp.dot(tri_ref[...], lf, preferred_element_type=jnp.float32,
                  precision=lax.Precision.HIGHEST) + carry_ref[...]
    cum_ref[...] = cum
    carry_ref[...] = cum[cum.shape[0] - 1:, :]


def _project(x2d, norm_g, w_perm, gain_row, cos, sin, bf_row, *, tm, tiles_per_batch, with_blocks):
    n = x2d.shape[0]
    nt = n // tm
    ncmp = max(tm // NSA_BLOCK, 8) if not with_blocks else tm // NSA_BLOCK
    bd = jnp.asarray(np.kron(np.eye(2), np.full((HEAD_DIM, HEAD_DIM), 1.0 / HEAD_DIM)), jnp.bfloat16)
    tri = jnp.asarray(np.tril(np.ones((tm, tm))), jnp.float32)
    f32, bf16 = jnp.float32, jnp.bfloat16
    row = lambda w: pl.BlockSpec((tm, w), lambda i: (i, 0))
    full = lambda a: pl.BlockSpec(a.shape, lambda i: (0,) * a.ndim)
    pos_spec = pl.BlockSpec((tm, LANES), lambda i: (i % tiles_per_batch, 0))
    qdt = w_perm.dtype
    outs = [
        (512, qdt),
        (512, f32),
        (256, f32),
        (256, bf16),
        (256, bf16),
        None,
        (256, qdt),
        (512, f32),
        (512, bf16),
        (512, qdt),
        (128, f32),
        (128, bf16),
        (128, f32),
        (128, bf16),
        (128, f32),
    ]
    out_shape, out_specs = [], []
    for o in outs:
        if o is None:
            out_shape.append(jax.ShapeDtypeStruct((nt * ncmp, 2 * LANES), f32))
            out_specs.append(pl.BlockSpec((ncmp, 2 * LANES), lambda i: (i, 0)))
        else:
            out_shape.append(jax.ShapeDtypeStruct((n, o[0]), o[1]))
            out_specs.append(row(o[0]))
    return pl.pallas_call(
        functools.partial(_proj_kernel, tiles_per_batch=tiles_per_batch, with_blocks=with_blocks),
        grid=(nt,),
        in_specs=[row(D_MODEL), full(norm_g), full(w_perm), full(gain_row), pos_spec, pos_spec,
                  full(bd), full(bf_row), full(tri)],
        out_specs=out_specs,
        out_shape=out_shape,
        scratch_shapes=[pltpu.VMEM((1, LANES), f32)],
        compiler_params=pltpu.CompilerParams(dimension_semantics=("arbitrary",),
                                             vmem_limit_bytes=VMEM_LIMIT),
        name="proj",
    )(x2d, norm_g, w_perm, gain_row, cos, sin, bd, bf_row, tri)


def _layer_tables(w_in_l, nsa_qk_g_l, fox_qk_g_l, fox_b_f_l, dsa_qk_g_l):
    perm = _proj_column_permutation()
    w_ext = jnp.concatenate([w_in_l, jnp.zeros((D_MODEL, 1), w_in_l.dtype)], axis=1)
    w_perm = jnp.take(w_ext, perm, axis=1).astype(jnp.float32)
    w_mg = w_in_l[:, _O_MG:].astype(jnp.bfloat16)
    one = jnp.ones((HEAD_DIM,), jnp.float32)
    two = lambda g: jnp.concatenate([g, g])
    ones2 = two(one)
    parts = [two(nsa_qk_g_l[0])] * 4
    parts += [two(nsa_qk_g_l[1]), ones2, two(nsa_qk_g_l[2]), ones2, two(nsa_qk_g_l[3]), ones2]
    parts += [two(fox_qk_g_l[0])] * 2 + [two(fox_qk_g_l[1])] * 2 + [ones2] * 2
    parts += [jnp.concatenate([dsa_qk_g_l[0], one])] * 4
    parts += [jnp.concatenate([dsa_qk_g_l[1], one]), ones2]
    gain_row = jnp.concatenate(parts).astype(jnp.float32)[None, :]
    bf_row = jnp.zeros((LANES,), jnp.float32).at[_M_FF:_M_IW].set(fox_b_f_l.astype(jnp.float32))[None, :]
    return w_perm, w_mg, gain_row, bf_row


_NT = (((1,), (1,)), ((), ()))


def _half_masks():
    lane = lax.broadcasted_iota(jnp.int32, (1, LANES), 1)
    return lane < HEAD_DIM, lane >= HEAD_DIM


def _online_update(s, valid, v, m_prev, l_prev, acc_prev):
    s = jnp.where(valid, s, -jnp.inf)
    m_new = jnp.maximum(m_prev, jnp.max(s, axis=-1, keepdims=True))
    p = jnp.exp(s - m_new)
    alpha = jnp.exp(m_prev - m_new)
    l_new = alpha * l_prev + jnp.sum(p, axis=-1, keepdims=True)
    pv = jnp.dot(p.reshape(-1, p.shape[-1]).astype(jnp.bfloat16), v, preferred_element_type=jnp.float32)
    acc_new = alpha * acc_prev + pv.reshape(acc_prev.shape)
    return m_new, l_new, acc_new


def _finish(l, acc):
    return acc / jnp.where(l > 0, l, 1.0)


def _fox_kernel(q_ref, kv_ref, cq_ref, ck_ref, o_ref, m_ref, l_ref, acc_ref, *, tq, tk):
    i = pl.program_id(1)
    j = pl.program_id(2)
    left, right = _half_masks()

    @pl.when(j == 0)
    def _():
        m_ref[...] = jnp.full_like(m_ref, NEG_BIG)
        l_ref[...] = jnp.zeros_like(l_ref)
        acc_ref[...] = jnp.zeros_like(acc_ref)

    @pl.when(j * tk < (i + 1) * tq)
    def _():
        rows = i * tq + lax.broadcasted_iota(jnp.int32, (tq, 1), 0)
        cols = j * tk + lax.broadcasted_iota(jnp.int32, (1, tk), 1)
        valid = cols <= rows
        for h in range(FOX_HEADS):
            c = h // 2
            qc = q_ref[:, c * LANES:(c + 1) * LANES]
            qh = jnp.where(left if h % 2 == 0 else right, qc, jnp.zeros_like(qc))
            k = kv_ref[:, c * LANES:(c + 1) * LANES]
            v = kv_ref[:, (2 + c) * LANES:(3 + c) * LANES]
            s = lax.dot_general(qh, k, _NT, preferred_element_type=jnp.float32)
            s = s + cq_ref[:, _M_FF + h:_M_FF + h + 1] - ck_ref[h:h + 1, :]
            m_ref[h], l_ref[h], acc_ref[h] = _online_update(s, valid, v, m_ref[h], l_ref[h], acc_ref[h])

    @pl.when(j == pl.num_programs(2) - 1)
    def _():
        for c in range(FOX_HEADS // 2):
            o = jnp.where(left, _finish(l_ref[2 * c], acc_ref[2 * c]), _finish(l_ref[2 * c + 1], acc_ref[2 * c + 1]))
            o_ref[:, c * LANES:(c + 1) * LANES] = o.astype(o_ref.dtype)


def _fox_prompt(qf, fox_kv, cum, cum_t, *, batch, seq, tq, tk):
    nq = seq // tq
    nk = seq // tk
    f32 = jnp.float32
    last = lambda i, j: jnp.minimum(j, ((i + 1) * tq - 1) // tk)
    return pl.pallas_call(
        functools.partial(_fox_kernel, tq=tq, tk=tk),
        grid=(batch, nq, nk),
        in_specs=[pl.BlockSpec((tq, 2 * LANES), lambda b, i, j: (b * nq + i, 0)),
                  pl.BlockSpec((tk, 4 * LANES), lambda b, i, j: (b * nk + last(i, j), 0)),
                  pl.BlockSpec((tq, LANES), lambda b, i, j: (b * nq + i, 0)),
                  pl.BlockSpec((None, FOX_HEADS, tk), lambda b, i, j: (b, 0, last(i, j)))],
        out_specs=pl.BlockSpec((tq, 2 * LANES), lambda b, i, j: (b * nq + i, 0)),
        out_shape=jax.ShapeDtypeStruct((batch * seq, 2 * LANES), jnp.bfloat16),
        scratch_shapes=[pltpu.VMEM((FOX_HEADS, tq, 1), f32), pltpu.VMEM((FOX_HEADS, tq, 1), f32),
                        pltpu.VMEM((FOX_HEADS, tq, LANES), f32)],
        compiler_params=pltpu.CompilerParams(dimension_semantics=("arbitrary", "arbitrary", "arbitrary")),
        name="fox_prompt",
    )(qf, fox_kv, cum, cum_t)


def _select_top_blocks(score, n_top):
    nb = score.shape[-1]
    blk = lax.broadcasted_iota(jnp.int32, (1, nb), 1).astype(jnp.float32)

    def body(_, carry):
        sc, sel = carry
        mx = jnp.max(sc, axis=-1, keepdims=True)
        first = jnp.min(jnp.where(sc == mx, blk, float(nb)), axis=-1, keepdims=True)
        pick = blk == first
        return jnp.where(pick, -jnp.inf, sc), jnp.where(pick, 1.0, sel)

    _, sel = lax.fori_loop(0, n_top, body, (score, jnp.zeros_like(score)))
    return sel


def _block_importance_scores(p_sum, qpos, nb):
    blk = lax.broadcasted_iota(jnp.int32, (1, nb), 1)
    cur = qpos // NSA_BLOCK
    forced = (blk == 0) | (blk == cur) | (blk == cur - 1)
    score = jnp.where(forced, p_sum + NSA_FORCE_BONUS, p_sum)
    return jnp.where(blk <= cur, score, -1.0)


def _masked_softmax(s, valid):
    s = jnp.where(valid, s, NEG_BIG)
    m = jnp.max(s, axis=-1, keepdims=True)
    e = jnp.where(valid, jnp.exp(s - m), 0.0)
    d = jnp.sum(e, axis=-1, keepdims=True)
    return e / jnp.where(d > 0, d, 1.0)


def _nsa_kernel(q_ref, misc_ref, cmp_ref, slc_ref, win_ref, o_ref, m_sc, l_sc, acc_sc, *, tq, tk, seq):
    i = pl.program_id(1)
    nb = seq // NSA_BLOCK
    g4 = NSA_GROUP
    left, right = _half_masks()
    qpos = i * tq + lax.broadcasted_iota(jnp.int32, (tq, 1), 0)
    blk = lax.broadcasted_iota(jnp.int32, (1, nb), 1)
    done = (blk + 1) * NSA_BLOCK <= qpos + 1
    kcm = cmp_ref[:, 0:LANES]
    vcm = cmp_ref[:, LANES:2 * LANES]
    lane_pos = lax.broadcasted_iota(jnp.int32, (1, tk), 1)
    n_top = min(NSA_TOP_BLOCKS, nb)
    n_chunks = ((i + 1) * tq + tk - 1) // tk
    tw = min(tq + NSA_WINDOW, seq)
    win_start = pl.multiple_of(jnp.clip(i * tq - NSA_WINDOW, 0, seq - tw), 16)
    win_pos = win_start + lax.broadcasted_iota(jnp.int32, (1, tw), 1)
    win_dist = qpos - win_pos
    win_valid = (win_dist >= 0) & (win_dist <= NSA_WINDOW)
    qs = [q_ref[:, c * LANES:(c + 1) * LANES] for c in range(g4)]
    qgs, o_cs, scores = [], [], []
    for g in range(NSA_KV_HEADS):
        hm = left if g == 0 else right
        qg = jnp.concatenate([jnp.where(hm, q, jnp.zeros_like(q)) for q in qs], axis=0)
        s_c = lax.dot_general(qg, kcm, _NT, preferred_element_type=jnp.float32).reshape(g4, tq, nb)
        p_c = _masked_softmax(s_c, done[None])
        o_cs.append(jnp.dot(p_c.reshape(g4 * tq, nb).astype(jnp.bfloat16), vcm,
                            preferred_element_type=jnp.float32).reshape(g4, tq, LANES))
        scores.append(_block_importance_scores(jnp.sum(p_c, axis=0), qpos, nb))
        qgs.append(qg)
    sel_all = _select_top_blocks(jnp.concatenate(scores, axis=0), n_top).astype(jnp.bfloat16)

    outs = []
    for g in range(NSA_KV_HEADS):
        qg, o_c = qgs[g], o_cs[g]
        sel = sel_all[g * tq:(g + 1) * tq]

        m_sc[...] = jnp.full_like(m_sc, NEG_BIG)
        l_sc[...] = jnp.zeros_like(l_sc)
        acc_sc[...] = jnp.zeros_like(acc_sc)

        def slc_body(c, _):
            start = pl.multiple_of(c * tk, tk)
            k = slc_ref[pl.ds(start, tk), 0:LANES]
            v = slc_ref[pl.ds(start, tk), LANES:2 * LANES]
            kpos = c * tk + lane_pos
            expand = (lax.broadcasted_iota(jnp.int32, (nb, 1), 0) == kpos // NSA_BLOCK)
            chosen = jnp.dot(sel, jnp.where(expand, 1.0, 0.0).astype(jnp.bfloat16),
                             preferred_element_type=jnp.float32) > 0.5
            valid = chosen & (kpos <= qpos)
            s = lax.dot_general(qg, k, _NT, preferred_element_type=jnp.float32).reshape(g4, tq, tk)
            m_sc[...], l_sc[...], acc_sc[...] = _online_update(s, valid[None], v, m_sc[...], l_sc[...], acc_sc[...])
            return 0

        lax.fori_loop(0, n_chunks, slc_body, 0)
        o_s = _finish(l_sc[...], acc_sc[...])

        k = win_ref[pl.ds(win_start, tw), 0:LANES]
        v = win_ref[pl.ds(win_start, tw), LANES:2 * LANES]
        s_w = lax.dot_general(qg, k, _NT, preferred_element_type=jnp.float32).reshape(g4, tq, tw)
        p_w = _masked_softmax(s_w, win_valid[None])
        o_w = jnp.dot(p_w.reshape(g4 * tq, tw).astype(jnp.bfloat16), v,
                      preferred_element_type=jnp.float32).reshape(g4, tq, LANES)

        heads = []
        for c in range(g4):
            h = g * g4 + c
            gate = lambda br: misc_ref[:, _M_GATE + br * NSA_HEADS + h:_M_GATE + br * NSA_HEADS + h + 1]
            heads.append(gate(0) * o_c[c] + gate(1) * o_s[c] + gate(2) * o_w[c])
        outs.append(heads)
    for c in range(g4):
        o_ref[:, c * LANES:(c + 1) * LANES] = jnp.where(left, outs[0][c], outs[1][c]).astype(o_ref.dtype)


def _nsa_prompt(qn, misc, cmp_bf, slc_kv, win_kv, *, batch, seq, tq, tk):
    nq = seq // tq
    nb = seq // NSA_BLOCK
    f32 = jnp.float32
    g4 = NSA_GROUP
    return pl.pallas_call(
        functools.partial(_nsa_kernel, tq=tq, tk=tk, seq=seq),
        grid=(batch, nq),
        in_specs=[pl.BlockSpec((tq, 4 * LANES), lambda b, i: (b * nq + i, 0)),
                  pl.BlockSpec((tq, LANES), lambda b, i: (b * nq + i, 0)),
                  pl.BlockSpec((nb, 2 * LANES), lambda b, i: (b, 0)),
                  pl.BlockSpec((seq, 2 * LANES), lambda b, i: (b, 0)),
                  pl.BlockSpec((seq, 2 * LANES), lambda b, i: (b, 0))],
        out_specs=pl.BlockSpec((tq, 4 * LANES), lambda b, i: (b * nq + i, 0)),
        out_shape=jax.ShapeDtypeStruct((batch * seq, 4 * LANES), jnp.bfloat16),
        scratch_shapes=[pltpu.VMEM((g4, tq, 1), f32), pltpu.VMEM((g4, tq, 1), f32),
                        pltpu.VMEM((g4, tq, LANES), f32)],
        compiler_params=pltpu.CompilerParams(dimension_semantics=("arbitrary", "arbitrary"),
                                             vmem_limit_bytes=VMEM_LIMIT),
        name="nsa_prompt",
    )(qn, misc, cmp_bf, slc_kv, win_kv)


_INT_MIN = -2 ** 31


def _sortable_key(x):
    bits = lax.bitcast_convert_type(x, jnp.int32)
    return jnp.where(bits < 0, bits ^ jnp.int32(0x7FFFFFFF), bits)


def _lane_fold(x):
    acc = x[:, 0:LANES]
    for c in range(1, x.shape[-1] // LANES):
        acc = acc + x[:, c * LANES:(c + 1) * LANES]
    return acc


def _dsa_kernel(dqi_ref, misc_ref, dki_ref, v_ref, upper_ref, o_ref, key_sc, seen_sc, m_sc, l_sc, acc_sc,
                *, tq, tk, n_keep):
    i = pl.program_id(1)
    nh = DSA_HEADS
    left, right = _half_masks()
    qpos = i * tq + lax.broadcasted_iota(jnp.int32, (tq, 1), 0)
    lane_pos = lax.broadcasted_iota(jnp.int32, (1, tk), 1)
    chunks = [dqi_ref[:, h * LANES:(h + 1) * LANES] for h in range(nh)]
    q_att = jnp.concatenate([jnp.where(left, q, jnp.zeros_like(q)) for q in chunks], axis=0)
    q_idx = jnp.concatenate([jnp.where(right, q, jnp.zeros_like(q)) for q in chunks], axis=0)
    n_chunks = ((i + 1) * tq + tk - 1) // tk

    def score_body(c, _):
        start = pl.multiple_of(c * tk, tk)
        kk = dki_ref[pl.ds(start, tk), :]
        a = lax.dot_general(q_idx, kk, _NT, preferred_element_type=jnp.float32).reshape(nh, tq, tk)
        a = jnp.maximum(a, 0.0)
        sc = a[0] * misc_ref[:, _M_IW:_M_IW + 1]
        for h in range(1, nh):
            sc = sc + a[h] * misc_ref[:, _M_IW + h:_M_IW + h + 1]
        kpos = c * tk + lane_pos
        sc = jnp.where(kpos <= qpos, sc, -jnp.inf)
        key_sc[c] = _sortable_key(sc)
        return 0

    lax.fori_loop(0, n_chunks, score_body, 0)

    def count(pred):
        def body(c, acc):
            return acc + _lane_fold(jnp.where(pred(c), 1.0, 0.0))
        part = lax.fori_loop(0, n_chunks, body, jnp.zeros((tq, LANES), jnp.float32))
        return jnp.sum(part, axis=-1, keepdims=True)

    def bit_body(it, lo):
        cand = lo + lax.shift_left(jnp.int32(1), jnp.int32(31) - it)
        cnt = count(lambda c: key_sc[c] >= cand)
        return jnp.where(cnt >= n_keep, cand, lo)

    thr = lax.fori_loop(0, 32, bit_body, jnp.full((tq, 1), _INT_MIN, jnp.int32))

    need = n_keep - count(lambda c: key_sc[c] > thr)
    seen_sc[...] = jnp.zeros_like(seen_sc)
    m_sc[...] = jnp.full_like(m_sc, NEG_BIG)
    l_sc[...] = jnp.zeros_like(l_sc)
    acc_sc[...] = jnp.zeros_like(acc_sc)

    def att_body(c, _):
        start = pl.multiple_of(c * tk, tk)
        kk = dki_ref[pl.ds(start, tk), :]
        vv = v_ref[pl.ds(start, tk), :]
        kpos = c * tk + lane_pos
        key = key_sc[c]
        tie = key == thr
        rank = seen_sc[...] + jnp.dot(jnp.where(tie, 1.0, 0.0).astype(jnp.bfloat16), upper_ref[...],
                                      preferred_element_type=jnp.float32)
        seen_sc[...] = rank[:, tk - 1:tk]
        valid = ((key > thr) | (tie & (rank <= need))) & (kpos <= qpos)
        s = lax.dot_general(q_att, kk, _NT, preferred_element_type=jnp.float32).reshape(nh, tq, tk)
        m_sc[...], l_sc[...], acc_sc[...] = _online_update(s, valid[None], vv, m_sc[...], l_sc[...], acc_sc[...])
        return 0

    lax.fori_loop(0, n_chunks, att_body, 0)
    o = _finish(l_sc[...], acc_sc[...])
    for h in range(nh):
        o_ref[:, h * LANES:(h + 1) * LANES] = jnp.where(left, o[h], 0.0).astype(o_ref.dtype)


def _dsa_prompt(dqi, misc, dki_bf, misc_bf, *, batch, seq, tq, tk, n_keep):
    nq = seq // tq
    nk = seq // tk
    f32 = jnp.float32
    nh = DSA_HEADS
    upper = jnp.asarray(np.triu(np.ones((tk, tk))), jnp.bfloat16)
    return pl.pallas_call(
        functools.partial(_dsa_kernel, tq=tq, tk=tk, n_keep=n_keep),
        grid=(batch, nq),
        in_specs=[pl.BlockSpec((tq, 4 * LANES), lambda b, i: (b * nq + i, 0)),
                  pl.BlockSpec((tq, LANES), lambda b, i: (b * nq + i, 0)),
                  pl.BlockSpec((seq, LANES), lambda b, i: (b, 0)),
                  pl.BlockSpec((seq, LANES), lambda b, i: (b, 0)),
                  pl.BlockSpec((tk, tk), lambda b, i: (0, 0))],
        out_specs=pl.BlockSpec((tq, 4 * LANES), lambda b, i: (b * nq + i, 0)),
        out_shape=jax.ShapeDtypeStruct((batch * seq, 4 * LANES), jnp.bfloat16),
        scratch_shapes=[pltpu.VMEM((nk, tq, tk), jnp.int32), pltpu.VMEM((tq, 1), f32),
                        pltpu.VMEM((nh, tq, 1), f32), pltpu.VMEM((nh, tq, 1), f32),
                        pltpu.VMEM((nh, tq, LANES), f32)],
        compiler_params=pltpu.CompilerParams(dimension_semantics=("arbitrary", "arbitrary"),
                                             vmem_limit_bytes=VMEM_LIMIT),
        name="dsa_prompt",
    )(dqi, misc, dki_bf, misc_bf, upper)


def _merge_kernel(x_ref, g_ref, on_ref, of_ref, od_ref, wmg_ref, wbn_ref, wbf_ref, wbd_ref, wo_ref, out_ref):
    x = x_ref[...]
    xn = _rms_rows(x, g_ref[...]).astype(jnp.bfloat16)
    y = None
    for br, (o_ref, wb_ref) in enumerate(((on_ref, wbn_ref), (of_ref, wbf_ref), (od_ref, wbd_ref))):
        gate = jax.nn.sigmoid(jnp.dot(xn, wmg_ref[:, br * D_MODEL:(br + 1) * D_MODEL],
                                      preferred_element_type=jnp.float32))
        term = gate * jnp.dot(o_ref[...], wb_ref[...], preferred_element_type=jnp.float32)
        y = term if y is None else y + term
    out_ref[...] = x + jnp.dot(y.astype(jnp.bfloat16), wo_ref[...], preferred_element_type=jnp.float32)


def _merge(x2d, norm_g, o_nsa, o_fox, o_dsa, w_mg, w_bn, w_bf, w_bd, w_o, *, tm):
    n = x2d.shape[0]
    row = lambda a: pl.BlockSpec((tm, a.shape[1]), lambda i: (i, 0))
    full = lambda a: pl.BlockSpec(a.shape, lambda i: (0,) * a.ndim)
    args = (x2d, norm_g, o_nsa, o_fox, o_dsa, w_mg, w_bn, w_bf, w_bd, w_o)
    return pl.pallas_call(
        _merge_kernel,
        grid=(n // tm,),
        in_specs=[row(x2d), full(norm_g), row(o_nsa), row(o_fox), row(o_dsa)] + [full(a) for a in args[5:]],
        out_specs=row(x2d),
        out_shape=jax.ShapeDtypeStruct(x2d.shape, x2d.dtype),
        compiler_params=pltpu.CompilerParams(dimension_semantics=("arbitrary",), vmem_limit_bytes=VMEM_LIMIT),
        name="merge",
    )(*args)


_R_GROUP = N_EXPERTS


def _route(logits):
    lane = lax.broadcasted_iota(jnp.int32, (1, LANES), 1)
    lanef = lane.astype(jnp.float32)
    is_grp = (lane >= _R_GROUP) & (lane < _R_GROUP + N_GROUPS)
    lg = jnp.where(is_grp, logits, -jnp.inf)
    gmax = jnp.max(lg, axis=-1, keepdims=True)
    grp = jnp.min(jnp.where(lg == gmax, lanef, float(LANES)), axis=-1, keepdims=True) - _R_GROUP
    g1 = 1.0 / jnp.sum(jnp.where(is_grp, jnp.exp(lg - gmax), 0.0), axis=-1, keepdims=True)
    in_grp = (lane < N_EXPERTS) & ((lane // EXPERTS_PER_GROUP).astype(jnp.float32) == grp)
    le = jnp.where(in_grp, logits, -jnp.inf)
    m1 = jnp.max(le, axis=-1, keepdims=True)
    i1 = jnp.min(jnp.where(le == m1, lanef, float(LANES)), axis=-1, keepdims=True)
    le2 = jnp.where(lanef == i1, -jnp.inf, le)
    m2 = jnp.max(le2, axis=-1, keepdims=True)
    i2 = jnp.min(jnp.where(le2 == m2, lanef, float(LANES)), axis=-1, keepdims=True)
    t = jnp.exp(m2 - m1)
    w1 = g1 * (1.0 / (1.0 + t))
    w2 = g1 * (t / (1.0 + t))
    return jnp.where(lanef == i1, w1, jnp.where(lanef == i2, w2, 0.0))


def _moe_kernel(x_ref, g_ref, wr_ref, br_ref, wg_ref, wu_ref, wd_ref, out_ref, xn_sc, comb_sc, acc_sc):
    e = pl.program_id(1)

    @pl.when(e == 0)
    def _():
        x = x_ref[...]
        xn = _rms_rows(x, g_ref[...])
        logits = jnp.dot(xn, wr_ref[...], preferred_element_type=jnp.float32,
                         precision=lax.Precision.HIGHEST) + br_ref[...]
        comb_sc[...] = _route(logits)
        xn_sc[...] = xn.astype(jnp.bfloat16)
        acc_sc[...] = x

    xn = xn_sc[...]
    h = (jax.nn.silu(jnp.dot(xn, wg_ref[...], preferred_element_type=jnp.float32))
         * jnp.dot(xn, wu_ref[...], preferred_element_type=jnp.float32))
    y = jnp.dot(h.astype(jnp.bfloat16), wd_ref[...], preferred_element_type=jnp.float32)
    lane = lax.broadcasted_iota(jnp.int32, (1, LANES), 1)
    ce = jnp.sum(jnp.where(lane == e, comb_sc[...], 0.0), axis=-1, keepdims=True)
    acc_sc[...] += ce * y

    @pl.when(e == pl.num_programs(1) - 1)
    def _():
        out_ref[...] = acc_sc[...]


def _moe(x2d, norm_g, w_router, b_router, w_gate, w_up, w_down, *, tm):
    n = x2d.shape[0]
    f32 = jnp.float32
    return pl.pallas_call(
        _moe_kernel,
        grid=(n // tm, N_EXPERTS),
        in_specs=[pl.BlockSpec((tm, D_MODEL), lambda i, e: (i, 0)),
                  pl.BlockSpec((1, D_MODEL), lambda i, e: (0, 0)),
                  pl.BlockSpec((D_MODEL, LANES), lambda i, e: (0, 0)),
                  pl.BlockSpec((1, LANES), lambda i, e: (0, 0)),
                  pl.BlockSpec((None, D_MODEL, D_EXPERT), lambda i, e: (e, 0, 0)),
                  pl.BlockSpec((None, D_MODEL, D_EXPERT), lambda i, e: (e, 0, 0)),
                  pl.BlockSpec((None, D_EXPERT, D_MODEL), lambda i, e: (e, 0, 0))],
        out_specs=pl.BlockSpec((tm, D_MODEL), lambda i, e: (i, 0)),
        out_shape=jax.ShapeDtypeStruct(x2d.shape, x2d.dtype),
        scratch_shapes=[pltpu.VMEM((tm, D_MODEL), jnp.bfloat16), pltpu.VMEM((tm, LANES), f32),
                        pltpu.VMEM((tm, D_MODEL), f32)],
        compiler_params=pltpu.CompilerParams(dimension_semantics=("arbitrary", "arbitrary"),
                                             vmem_limit_bytes=VMEM_LIMIT),
        name="moe",
    )(x2d, norm_g, w_router, b_router, w_gate, w_up, w_down)


def _new_page(row, dtype):
    first = lax.broadcasted_iota(jnp.int32, (PAGE_SIZE, 1), 0) == 0
    return jnp.where(first, row, 0.0).astype(dtype)


def _nsa_sample_kernel(pt_ref, qbd_ref, gate_ref, new_ref, neww_ref, cw_ref, *rest, pg, n_pages):
    page_refs = rest[:pg]
    o_ref, ks_sc, vs_sc, cmpk_sc, cmpv_sc = rest[pg:]
    s = pl.program_id(1)
    n_tok = n_pages * PAGE_SIZE
    nb_s = n_tok // NSA_BLOCK + 1
    nbpad = cmpk_sc.shape[0]
    l_pad = ks_sc.shape[0]
    bpp = PAGE_SIZE // NSA_BLOCK
    bf16 = jnp.bfloat16

    for k in range(pg):
        page = page_refs[k][...]
        p = s * pg + k
        row0 = pl.multiple_of(p * PAGE_SIZE, PAGE_SIZE)
        ks_sc[pl.ds(row0, PAGE_SIZE), :] = page[:, 2 * LANES:3 * LANES].astype(bf16)
        vs_sc[pl.ds(row0, PAGE_SIZE), :] = page[:, 3 * LANES:4 * LANES].astype(bf16)
        cmpk_sc[pl.ds(p * bpp, bpp), :] = jnp.mean(page[:, 0:LANES].reshape(bpp, NSA_BLOCK, LANES), axis=1)
        cmpv_sc[pl.ds(p * bpp, bpp), :] = jnp.mean(page[:, LANES:2 * LANES].reshape(bpp, NSA_BLOCK, LANES), axis=1)

    @pl.when(s == pl.num_programs(1) - 1)
    def _():
        new = new_ref[...]
        ks_sc[n_tok:l_pad, :] = _new_page(new[:, 2 * LANES:3 * LANES], bf16)
        vs_sc[n_tok:l_pad, :] = _new_page(new[:, 3 * LANES:4 * LANES], bf16)
        tail = nbpad - (nb_s - 1)
        first = lax.broadcasted_iota(jnp.int32, (tail, 1), 0) == 0
        cmpk_sc[nb_s - 1:nbpad, :] = jnp.where(first, new[:, 0:LANES] * (1.0 / NSA_BLOCK), 0.0)
        cmpv_sc[nb_s - 1:nbpad, :] = jnp.where(first, new[:, LANES:2 * LANES] * (1.0 / NSA_BLOCK), 0.0)

        qbd = qbd_ref[...]
        nh = qbd.shape[0]
        qpos = n_tok
        blk = lax.broadcasted_iota(jnp.int32, (1, nbpad), 1)
        done = ((blk + 1) * NSA_BLOCK <= qpos + 1) & (blk < nb_s)
        s_c = lax.dot_general(qbd, cmpk_sc[...].astype(bf16), _NT, preferred_element_type=jnp.float32)
        p_c = _masked_softmax(s_c, done)
        o_c = jnp.dot(p_c.astype(bf16), cmpv_sc[...].astype(bf16), preferred_element_type=jnp.float32)
        imp = jnp.sum(p_c.reshape(NSA_KV_HEADS, NSA_GROUP, nbpad), axis=1)
        qpos_col = jnp.full((NSA_KV_HEADS, 1), qpos, jnp.int32)
        score = _block_importance_scores(imp, qpos_col, nbpad)
        score = jnp.where(blk < nb_s, score, -2.0)
        sel = _select_top_blocks(score, min(NSA_TOP_BLOCKS, nb_s))
        sel8 = jnp.concatenate([jnp.broadcast_to(sel[g:g + 1], (NSA_GROUP, nbpad)) for g in range(NSA_KV_HEADS)],
                               axis=0).astype(bf16)
        kpos = lax.broadcasted_iota(jnp.int32, (1, l_pad), 1)
        expand = lax.broadcasted_iota(jnp.int32, (nbpad, 1), 0) == kpos // NSA_BLOCK
        chosen = jnp.dot(sel8, jnp.where(expand, 1.0, 0.0).astype(bf16), preferred_element_type=jnp.float32) > 0.5
        s_s = lax.dot_general(qbd, ks_sc[...], _NT, preferred_element_type=jnp.float32)
        p_s = _masked_softmax(s_s, chosen & (kpos <= qpos))
        o_s = jnp.dot(p_s.astype(bf16), vs_sc[...], preferred_element_type=jnp.float32)

        cw = cw_ref[...].astype(bf16)
        nw = neww_ref[...].astype(bf16).astype(jnp.float32)
        s1 = lax.dot_general(qbd, cw[:, 0:LANES], _NT, preferred_element_type=jnp.float32)
        s2 = jnp.sum(qbd.astype(jnp.float32) * nw[:, 0:LANES], axis=-1, keepdims=True)
        m = jnp.maximum(jnp.max(s1, axis=-1, keepdims=True), s2)
        e1 = jnp.exp(s1 - m)
        e2 = jnp.exp(s2 - m)
        d = jnp.sum(e1, axis=-1, keepdims=True) + e2
        p1 = e1 / d
        p2 = (e2 / d).astype(bf16).astype(jnp.float32)
        o_w = jnp.dot(p1.astype(bf16), cw[:, LANES:2 * LANES], preferred_element_type=jnp.float32) + p2 * nw[:, LANES:2 * LANES]

        g = gate_ref[...]
        o_ref[...] = g[:, 0:1] * o_c + g[:, 1:2] * o_s + g[:, 2:3] * o_w


def _page_specs(cache, layer, pg, width):
    return [pl.BlockSpec((None, None, PAGE_SIZE, width),
                         functools.partial(lambda b, s, pt, k: (layer, pt[b, s * pg + k], 0, 0), k=k))
            for k in range(pg)]


def _seq_spec(shape):
    nd = len(shape)
    return pl.BlockSpec((None,) + tuple(shape[1:]), lambda b, s, pt: (b,) + (0,) * (nd - 1))


def _nsa_sample(page_table, qbd, gates, new_rows, new_win, cache_win_l, cache, layer, *, pg):
    bs, n_pages = page_table.shape
    n_tok = n_pages * PAGE_SIZE
    l_pad = n_tok + PAGE_SIZE
    nbpad = -(-(n_tok // NSA_BLOCK + 1) // 8) * 8
    f32, bf16 = jnp.float32, jnp.bfloat16
    fixed = (qbd, gates, new_rows, new_win, cache_win_l)
    grid_spec = pltpu.PrefetchScalarGridSpec(
        num_scalar_prefetch=1,
        grid=(bs, n_pages // pg),
        in_specs=[_seq_spec(a.shape) for a in fixed] + _page_specs(cache, layer, pg, 4 * LANES),
        out_specs=_seq_spec((bs, NSA_HEADS, LANES)),
        scratch_shapes=[pltpu.VMEM((l_pad, LANES), bf16), pltpu.VMEM((l_pad, LANES), bf16),
                        pltpu.VMEM((nbpad, LANES), f32), pltpu.VMEM((nbpad, LANES), f32)])
    return pl.pallas_call(
        functools.partial(_nsa_sample_kernel, pg=pg, n_pages=n_pages),
        grid_spec=grid_spec,
        out_shape=jax.ShapeDtypeStruct((bs, NSA_HEADS, LANES), f32),
        compiler_params=pltpu.CompilerParams(dimension_semantics=("arbitrary", "arbitrary"),
                                             vmem_limit_bytes=VMEM_LIMIT),
        name="nsa_sample",
    )(page_table, *fixed, *([cache] * pg))


def _fox_sample_kernel(pt_ref, qbd_ref, new_ref, newlf_ref, du_ref, *rest, pg, n_pages):
    page_refs = rest[:pg]
    lf_refs = rest[pg:2 * pg]
    o_ref, k_sc, v_sc, lf_sc, cum_sc, bias_sc = rest[2 * pg:]
    s = pl.program_id(1)
    n_tok = n_pages * PAGE_SIZE
    l_pad = k_sc.shape[0]
    bf16 = jnp.bfloat16
    hi = lax.Precision.HIGHEST
    w = FOX_HEADS * HEAD_DIM

    for k in range(pg):
        page = page_refs[k][...]
        p = s * pg + k
        row0 = pl.multiple_of(p * PAGE_SIZE, PAGE_SIZE)
        k_sc[pl.ds(row0, PAGE_SIZE), :] = page[:, 0:w].astype(bf16)
        v_sc[pl.ds(row0, PAGE_SIZE), :] = page[:, w:2 * w].astype(bf16)
        lf_sc[pl.ds(p, 1), :] = lf_refs[k][...]

    @pl.when(s == pl.num_programs(1) - 1)
    def _():
        new = new_ref[...]
        k_sc[n_tok:l_pad, :] = _new_page(new[:, 0:w], bf16)
        v_sc[n_tok:l_pad, :] = _new_page(new[:, w:2 * w], bf16)
        lf = lf_sc[...]
        rowi = lax.broadcasted_iota(jnp.int32, (n_pages, n_pages), 0)
        coli = lax.broadcasted_iota(jnp.int32, (n_pages, n_pages), 1)
        strict_lower = jnp.where(coli < rowi, 1.0, 0.0)
        kpos = lax.broadcasted_iota(jnp.int32, (1, l_pad), 1)
        for h in range(FOX_HEADS):
            within = jnp.dot(lf, du_ref[h], preferred_element_type=jnp.float32, precision=hi)
            before = jnp.dot(strict_lower, within, preferred_element_type=jnp.float32, precision=hi)
            cum = within + before[:, PAGE_SIZE - 1:PAGE_SIZE]
            cum_sc[...] = cum
            total = cum[n_pages - 1:n_pages, PAGE_SIZE - 1:PAGE_SIZE]
            cq = total + newlf_ref[:, h:h + 1]
            for p in range(n_pages):
                bias_sc[h:h + 1, p * PAGE_SIZE:(p + 1) * PAGE_SIZE] = cq - cum_sc[p:p + 1, :]
            bias_sc[h:h + 1, n_tok:l_pad] = jnp.zeros((1, l_pad - n_tok), jnp.float32)
        qbd = qbd_ref[...]
        sc = lax.dot_general(qbd, k_sc[...], _NT, preferred_element_type=jnp.float32) + bias_sc[...]
        pr = _masked_softmax(sc, kpos <= n_tok)
        o_ref[...] = jnp.dot(pr.astype(bf16), v_sc[...], preferred_element_type=jnp.float32)


def _fox_sample(page_table, qbd, new_rows, new_lf, cache, cache_lf, layer, *, pg):
    bs, n_pages = page_table.shape
    n_tok = n_pages * PAGE_SIZE
    l_pad = n_tok + PAGE_SIZE
    w = FOX_HEADS * HEAD_DIM
    f32, bf16 = jnp.float32, jnp.bfloat16
    lane = np.arange(PAGE_SIZE * FOX_HEADS)
    du = np.stack([((lane % FOX_HEADS == h)[:, None] & ((lane // FOX_HEADS)[:, None] <= np.arange(PAGE_SIZE)[None, :]))
                   for h in range(FOX_HEADS)]).astype(np.float32)
    du = jnp.asarray(du)
    fixed = (qbd, new_rows, new_lf)
    lf_specs = [pl.BlockSpec((None, None, 1, PAGE_SIZE * FOX_HEADS),
                             functools.partial(lambda b, s, pt, k: (layer, pt[b, s * pg + k], 0, 0), k=k))
                for k in range(pg)]
    grid_spec = pltpu.PrefetchScalarGridSpec(
        num_scalar_prefetch=1,
        grid=(bs, n_pages // pg),
        in_specs=[_seq_spec(a.shape) for a in fixed] + [pl.BlockSpec(du.shape, lambda b, s, pt: (0, 0, 0))]
        + _page_specs(cache, layer, pg, 2 * w) + lf_specs,
        out_specs=_seq_spec((bs, FOX_HEADS, w)),
        scratch_shapes=[pltpu.VMEM((l_pad, w), bf16), pltpu.VMEM((l_pad, w), bf16),
                        pltpu.VMEM((n_pages, PAGE_SIZE * FOX_HEADS), f32),
                        pltpu.VMEM((n_pages, PAGE_SIZE), f32), pltpu.VMEM((FOX_HEADS, l_pad), f32)])
    return pl.pallas_call(
        functools.partial(_fox_sample_kernel, pg=pg, n_pages=n_pages),
        grid_spec=grid_spec,
        out_shape=jax.ShapeDtypeStruct((bs, FOX_HEADS, w), f32),
        compiler_params=pltpu.CompilerParams(dimension_semantics=("arbitrary", "arbitrary"),
                                             vmem_limit_bytes=VMEM_LIMIT),
        name="fox_sample",
    )(page_table, *fixed, du, *([cache] * pg), *([cache_lf] * pg))


def _dsa_sample_kernel(pt_ref, qatt_ref, qidx_ref, iw_ref, new_ref, *rest, pg, n_pages, n_keep):
    page_refs = rest[:pg]
    o_ref, kv_sc, ik_sc = rest[pg:]
    s = pl.program_id(1)
    n_tok = n_pages * PAGE_SIZE
    l_pad = kv_sc.shape[0]
    bf16 = jnp.bfloat16
    hd = HEAD_DIM

    def split(page):
        ik = jnp.concatenate([page[:, 2 * hd:3 * hd], jnp.zeros((page.shape[0], hd), page.dtype)], axis=1)
        return page[:, 0:2 * hd].astype(bf16), ik.astype(bf16)

    for k in range(pg):
        p = s * pg + k
        row0 = pl.multiple_of(p * PAGE_SIZE, PAGE_SIZE)
        kv, ik = split(page_refs[k][...])
        kv_sc[pl.ds(row0, PAGE_SIZE), :] = kv
        ik_sc[pl.ds(row0, PAGE_SIZE), :] = ik

    @pl.when(s == pl.num_programs(1) - 1)
    def _():
        first = lax.broadcasted_iota(jnp.int32, (PAGE_SIZE, 1), 0) == 0
        kv, ik = split(jnp.where(first, new_ref[...], 0.0))
        kv_sc[n_tok:l_pad, :] = kv
        ik_sc[n_tok:l_pad, :] = ik
        kpos = lax.broadcasted_iota(jnp.int32, (1, l_pad), 1)
        causal = kpos <= n_tok
        a = lax.dot_general(qidx_ref[...], ik_sc[...], _NT, preferred_element_type=jnp.float32)
        a = jnp.maximum(a, 0.0) * iw_ref[...]
        sc = a[0:1]
        for h in range(1, DSA_IDX_HEADS):
            sc = sc + a[h:h + 1]
        key = _sortable_key(jnp.where(causal, sc, -jnp.inf))

        def count(mask):
            return jnp.sum(jnp.where(mask, 1.0, 0.0), axis=-1, keepdims=True)

        def bit_body(it, lo):
            cand = lo + lax.shift_left(jnp.int32(1), jnp.int32(31) - it)
            return jnp.where(count(key >= cand) >= n_keep, cand, lo)

        thr = lax.fori_loop(0, 32, bit_body, jnp.full((1, 1), _INT_MIN, jnp.int32))
        need = n_keep - count(key > thr)
        tie = jnp.where(key == thr, kpos, jnp.int32(2 ** 30))
        idx_bits = max(1, (l_pad - 1).bit_length())

        def idx_body(it, bound):
            step = lax.shift_left(jnp.int32(1), jnp.int32(idx_bits - 1) - it)
            return jnp.where(count(tie <= bound + step - 1) < need, bound + step, bound)

        bound = lax.fori_loop(0, idx_bits, idx_body, jnp.zeros((1, 1), jnp.int32))
        valid = ((key > thr) | (tie <= bound)) & causal
        s_att = lax.dot_general(qatt_ref[...], kv_sc[...], _NT, preferred_element_type=jnp.float32)
        pr = _masked_softmax(s_att, valid)
        o_ref[...] = jnp.dot(pr.astype(bf16), kv_sc[...], preferred_element_type=jnp.float32)


def _dsa_sample(page_table, q_att, q_idx, iw_col, new_rows, cache, layer, *, pg, n_keep):
    bs, n_pages = page_table.shape
    n_tok = n_pages * PAGE_SIZE
    l_pad = n_tok + PAGE_SIZE
    f32, bf16 = jnp.float32, jnp.bfloat16
    fixed = (q_att, q_idx, iw_col, new_rows)
    grid_spec = pltpu.PrefetchScalarGridSpec(
        num_scalar_prefetch=1,
        grid=(bs, n_pages // pg),
        in_specs=[_seq_spec(a.shape) for a in fixed] + _page_specs(cache, layer, pg, 3 * HEAD_DIM),
        out_specs=_seq_spec((bs, DSA_HEADS, LANES)),
        scratch_shapes=[pltpu.VMEM((l_pad, LANES), bf16), pltpu.VMEM((l_pad, LANES), bf16)])
    return pl.pallas_call(
        functools.partial(_dsa_sample_kernel, pg=pg, n_pages=n_pages, n_keep=n_keep),
        grid_spec=grid_spec,
        out_shape=jax.ShapeDtypeStruct((bs, DSA_HEADS, LANES), f32),
        compiler_params=pltpu.CompilerParams(dimension_semantics=("arbitrary", "arbitrary"),
                                             vmem_limit_bytes=VMEM_LIMIT),
        name="dsa_sample",
    )(page_table, *fixed, *([cache] * pg))


def _token0_page(col, dtype):
    first = lax.broadcasted_iota(jnp.int32, (1, PAGE_SIZE), 1) == 0
    return jnp.where(first, col, 0.0).astype(dtype)


def _hi_lo(x):
    hi = x.astype(jnp.bfloat16)
    return hi, (x - hi.astype(jnp.float32)).astype(jnp.bfloat16)


def _scores(q, pages_sc, n):
    return jnp.concatenate([jnp.dot(q, pages_sc[p], preferred_element_type=jnp.float32) for p in range(n)], axis=1)


def _weighted_values(p, pages_sc, n):
    p = p.astype(jnp.bfloat16)
    out = None
    for k in range(n):
        term = lax.dot_general(p[:, k * PAGE_SIZE:(k + 1) * PAGE_SIZE], pages_sc[k], _NT,
                               preferred_element_type=jnp.float32)
        out = term if out is None else out + term
    return out


def _split_dot_nt(a, b_bf16):
    hi = a.astype(jnp.bfloat16)
    lo = (a - hi.astype(jnp.float32)).astype(jnp.bfloat16)
    return (lax.dot_general(hi, b_bf16, _NT, preferred_element_type=jnp.float32)
            + lax.dot_general(lo, b_bf16, _NT, preferred_element_type=jnp.float32))


def _nsa_decode_kernel(pt_ref, qbd_ref, gate_ref, new_ref, neww_ref, cw_ref, *rest, pg, n_pages):
    page_refs = rest[:pg]
    o_ref, kc_sc, kcl_sc, vc_sc, ks_sc, vs_sc = rest[pg:]
    s = pl.program_id(1)
    n_tok = n_pages * PAGE_SIZE
    l_pad = n_tok + PAGE_SIZE
    nb_s = n_tok // NSA_BLOCK + 1
    nbpad = -(-nb_s // 8) * 8
    bf16 = jnp.bfloat16
    nh = NSA_HEADS

    for k in range(pg):
        page = page_refs[k][...]
        p = s * pg + k
        kc_sc[p], kcl_sc[p] = _hi_lo(page[0 * LANES:1 * LANES])
        vc_sc[p] = page[1 * LANES:2 * LANES].astype(bf16)
        ks_sc[p] = page[2 * LANES:3 * LANES].astype(bf16)
        vs_sc[p] = page[3 * LANES:4 * LANES].astype(bf16)

    @pl.when(s == pl.num_programs(1) - 1)
    def _():
        new = new_ref[...]
        kc_sc[n_pages], kcl_sc[n_pages] = _hi_lo(_token0_page(new[0 * LANES:1 * LANES], jnp.float32))
        vc_sc[n_pages] = _token0_page(new[1 * LANES:2 * LANES], bf16)
        ks_sc[n_pages] = _token0_page(new[2 * LANES:3 * LANES], bf16)
        vs_sc[n_pages] = _token0_page(new[3 * LANES:4 * LANES], bf16)
        n_all = n_pages + 1
        q_hl = qbd_ref[...]
        qbd = q_hl[0:nh]
        qpos = n_tok
        kpos = lax.broadcasted_iota(jnp.int32, (1, l_pad), 1)
        blk = lax.broadcasted_iota(jnp.int32, (1, nbpad), 1)
        incid = jnp.where(lax.broadcasted_iota(jnp.int32, (nbpad, 1), 0) == kpos // NSA_BLOCK, 1.0, 0.0).astype(bf16)
        inv = 1.0 / NSA_BLOCK

        s_hl = _scores(q_hl, kc_sc, n_all)
        s_tok = s_hl[0:nh] + s_hl[nh:2 * nh] + _scores(qbd, kcl_sc, n_all)
        s_c = _split_dot_nt(s_tok, incid) * inv
        done = ((blk + 1) * NSA_BLOCK <= qpos + 1) & (blk < nb_s)
        p_c = _masked_softmax(s_c, done)
        p_tok = jnp.dot(p_c.astype(bf16), incid, preferred_element_type=jnp.float32) * inv
        o_c = _weighted_values(p_tok, vc_sc, n_all)

        imp = jnp.sum(p_c.reshape(NSA_KV_HEADS, NSA_GROUP, nbpad), axis=1)
        score = _block_importance_scores(imp, jnp.full((NSA_KV_HEADS, 1), qpos, jnp.int32), nbpad)
        score = jnp.where(blk < nb_s, score, -2.0)
        sel = _select_top_blocks(score, min(NSA_TOP_BLOCKS, nb_s))
        sel8 = jnp.concatenate([jnp.broadcast_to(sel[g:g + 1], (NSA_GROUP, nbpad)) for g in range(NSA_KV_HEADS)],
                               axis=0).astype(bf16)
        chosen = jnp.dot(sel8, incid, preferred_element_type=jnp.float32) > 0.5
        p_s = _masked_softmax(_scores(qbd, ks_sc, n_all), chosen & (kpos <= qpos))
        o_s = _weighted_values(p_s, vs_sc, n_all)

        cw = cw_ref[...].astype(bf16)
        nw = neww_ref[...]
        win_keep = cw.shape[1]
        s_w = jnp.concatenate([jnp.dot(qbd, cw[0:LANES], preferred_element_type=jnp.float32),
                               jnp.dot(qbd, _token0_page(nw[0:LANES], bf16), preferred_element_type=jnp.float32)],
                              axis=1)
        wpos = lax.broadcasted_iota(jnp.int32, (1, win_keep + PAGE_SIZE), 1)
        p_w = _masked_softmax(s_w, wpos <= win_keep).astype(bf16)
        o_w = (lax.dot_general(p_w[:, :win_keep], cw[LANES:2 * LANES], _NT, preferred_element_type=jnp.float32)
               + lax.dot_general(p_w[:, win_keep:], _token0_page(nw[LANES:2 * LANES], bf16), _NT,
                                 preferred_element_type=jnp.float32))

        g = gate_ref[...]
        o_ref[...] = g[:, 0:1] * o_c + g[:, 1:2] * o_s + g[:, 2:3] * o_w


def _page_specs_t(layer, pg, rows):
    return [pl.BlockSpec((None, None, rows, PAGE_SIZE),
                         functools.partial(lambda b, s, pt, k: (layer, pt[b, s * pg + k], 0, 0), k=k))
            for k in range(pg)]


def _decode_call(kernel_fn, name, page_table, fixed, const, caches, layer, out_tail, scratch, *, pg):
    bs, n_pages = page_table.shape
    const_specs = [pl.BlockSpec(a.shape, functools.partial(lambda b, s, pt, nd: (0,) * nd, nd=a.ndim)) for a in const]
    page_specs = []
    for cache in caches:
        page_specs += _page_specs_t(layer, pg, cache.shape[2])
    grid_spec = pltpu.PrefetchScalarGridSpec(
        num_scalar_prefetch=1,
        grid=(bs, n_pages // pg),
        in_specs=[_seq_spec(a.shape) for a in fixed] + const_specs + page_specs,
        out_specs=_seq_spec((bs,) + out_tail),
        scratch_shapes=scratch)
    operands = list(fixed) + list(const)
    for cache in caches:
        operands += [cache] * pg
    return pl.pallas_call(
        kernel_fn,
        grid_spec=grid_spec,
        out_shape=jax.ShapeDtypeStruct((bs,) + out_tail, jnp.float32),
        compiler_params=pltpu.CompilerParams(dimension_semantics=("arbitrary", "arbitrary"),
                                             vmem_limit_bytes=VMEM_LIMIT),
        name=name,
    )(page_table, *operands)


def _nsa_decode(page_table, qbd, gates, new_t, neww_t, cache_win_t, cache_t, layer, *, pg):
    n_pages = page_table.shape[1]
    buf = pltpu.VMEM((n_pages + 1, LANES, PAGE_SIZE), jnp.bfloat16)
    return _decode_call(functools.partial(_nsa_decode_kernel, pg=pg, n_pages=n_pages), "nsa_sample", page_table,
                        (qbd, gates, new_t, neww_t, cache_win_t), (), (cache_t,), layer, (NSA_HEADS, LANES),
                        [buf, buf, buf, buf, buf], pg=pg)


def _fox_decode_kernel(pt_ref, qbd_ref, new_ref, newlf_ref, *rest, pg, n_pages):
    page_refs = rest[:pg]
    lf_refs = rest[pg:2 * pg]
    o_ref, k_sc, v_sc, lf_sc, cum_sc = rest[2 * pg:]
    s = pl.program_id(1)
    n_tok = n_pages * PAGE_SIZE
    l_pad = n_tok + PAGE_SIZE
    bf16 = jnp.bfloat16
    hi = lax.Precision.HIGHEST
    w = FOX_HEADS * HEAD_DIM

    for k in range(pg):
        page = page_refs[k][...]
        p = s * pg + k
        k_sc[p] = page[0:w].astype(bf16)
        v_sc[p] = page[w:2 * w].astype(bf16)
        lf = lf_refs[k][...]
        for h in range(FOX_HEADS):
            lf_sc[h, pl.ds(p, 1), :] = lf[h:h + 1, :]

    @pl.when(s == pl.num_programs(1) - 1)
    def _():
        new = new_ref[...]
        k_sc[n_pages] = _token0_page(new[0:w], bf16)
        v_sc[n_pages] = _token0_page(new[w:2 * w], bf16)
        n_all = n_pages + 1
        iota2 = lambda n, axis: lax.broadcasted_iota(jnp.int32, (n, n), axis)
        upper_incl = jnp.where(iota2(PAGE_SIZE, 0) <= iota2(PAGE_SIZE, 1), 1.0, 0.0)
        lower_strict = jnp.where(iota2(n_pages, 1) < iota2(n_pages, 0), 1.0, 0.0)
        cqs = []
        for h in range(FOX_HEADS):
            within = jnp.dot(lf_sc[h], upper_incl, preferred_element_type=jnp.float32, precision=hi)
            before = jnp.dot(lower_strict, within, preferred_element_type=jnp.float32, precision=hi)
            cum = within + before[:, PAGE_SIZE - 1:PAGE_SIZE]
            cum_sc[h] = cum
            cqs.append(cum[n_pages - 1:n_pages, PAGE_SIZE - 1:PAGE_SIZE] + newlf_ref[:, h:h + 1])
        cq = jnp.concatenate(cqs, axis=0)
        qbd = qbd_ref[...]
        parts = []
        for p in range(n_pages):
            ck = jnp.concatenate([cum_sc[h, p:p + 1, :] for h in range(FOX_HEADS)], axis=0)
            parts.append(jnp.dot(qbd, k_sc[p], preferred_element_type=jnp.float32) + (cq - ck))
        parts.append(jnp.dot(qbd, k_sc[n_pages], preferred_element_type=jnp.float32))
        sc = jnp.concatenate(parts, axis=1)
        kpos = lax.broadcasted_iota(jnp.int32, (1, l_pad), 1)
        o_ref[...] = _weighted_values(_masked_softmax(sc, kpos <= n_tok), v_sc, n_all)


def _fox_decode(page_table, qbd, new_t, new_lf, cache_t, cache_lf_t, layer, *, pg):
    n_pages = page_table.shape[1]
    w = FOX_HEADS * HEAD_DIM
    buf = pltpu.VMEM((n_pages + 1, w, PAGE_SIZE), jnp.bfloat16)
    lfbuf = pltpu.VMEM((FOX_HEADS, n_pages, PAGE_SIZE), jnp.float32)
    return _decode_call(functools.partial(_fox_decode_kernel, pg=pg, n_pages=n_pages), "fox_sample", page_table,
                        (qbd, new_t, new_lf), (), (cache_t, cache_lf_t), layer, (FOX_HEADS, w),
                        [buf, buf, lfbuf, lfbuf], pg=pg)


def _dsa_decode_kernel(pt_ref, qatt_ref, qidx_ref, iw_ref, new_ref, *rest, pg, n_pages, n_keep):
    page_refs = rest[:pg]
    o_ref, kv_sc, ik_sc, ikl_sc = rest[pg:]
    s = pl.program_id(1)
    n_tok = n_pages * PAGE_SIZE
    l_pad = n_tok + PAGE_SIZE
    bf16 = jnp.bfloat16
    hd = HEAD_DIM
    nh = DSA_IDX_HEADS

    for k in range(pg):
        page = page_refs[k][...]
        p = s * pg + k
        kv_sc[p] = page[0:2 * hd].astype(bf16)
        ik_sc[p], ikl_sc[p] = _hi_lo(page[2 * hd:3 * hd])

    @pl.when(s == pl.num_programs(1) - 1)
    def _():
        new = new_ref[...]
        kv_sc[n_pages] = _token0_page(new[0:2 * hd], bf16)
        ik_sc[n_pages], ikl_sc[n_pages] = _hi_lo(_token0_page(new[2 * hd:3 * hd], jnp.float32))
        n_all = n_pages + 1
        kpos = lax.broadcasted_iota(jnp.int32, (1, l_pad), 1)
        causal = kpos <= n_tok
        q_hl = qidx_ref[...]
        a_hl = _scores(q_hl, ik_sc, n_all)
        a = a_hl[0:nh] + a_hl[nh:2 * nh] + _scores(q_hl[0:nh], ikl_sc, n_all)
        a = jnp.maximum(a, 0.0) * iw_ref[...]
        sc = a[0:1]
        for h in range(1, DSA_IDX_HEADS):
            sc = sc + a[h:h + 1]
        key = _sortable_key(jnp.where(causal, sc, -jnp.inf))

        def count(mask):
            return jnp.sum(jnp.where(mask, 1.0, 0.0), axis=-1, keepdims=True)

        def bit_body(it, lo):
            cand = lo + lax.shift_left(jnp.int32(1), jnp.int32(31) - it)
            return jnp.where(count(key >= cand) >= n_keep, cand, lo)

        thr = lax.fori_loop(0, 32, bit_body, jnp.full((1, 1), _INT_MIN, jnp.int32))
        need = n_keep - count(key > thr)
        tie = jnp.where(key == thr, kpos, jnp.int32(2 ** 30))
        idx_bits = max(1, (l_pad - 1).bit_length())

        def idx_body(it, bound):
            step = lax.shift_left(jnp.int32(1), jnp.int32(idx_bits - 1) - it)
            return jnp.where(count(tie <= bound + step - 1) < need, bound + step, bound)

        bound = lax.fori_loop(0, idx_bits, idx_body, jnp.zeros((1, 1), jnp.int32))
        valid = ((key > thr) | (tie <= bound)) & causal
        pr = _masked_softmax(_scores(qatt_ref[...], kv_sc, n_all), valid)
        o_ref[...] = _weighted_values(pr, kv_sc, n_all)


def _dsa_decode(page_table, q_att, q_idx, iw_col, new_t, cache_t, layer, *, pg, n_keep):
    n_pages = page_table.shape[1]
    hd = HEAD_DIM
    return _decode_call(functools.partial(_dsa_decode_kernel, pg=pg, n_pages=n_pages, n_keep=n_keep), "dsa_sample",
                        page_table, (q_att, q_idx, iw_col, new_t), (), (cache_t,), layer, (DSA_HEADS, LANES),
                        [pltpu.VMEM((n_pages + 1, 2 * hd, PAGE_SIZE), jnp.bfloat16),
                         pltpu.VMEM((n_pages + 1, hd, PAGE_SIZE), jnp.bfloat16),
                         pltpu.VMEM((n_pages + 1, hd, PAGE_SIZE), jnp.bfloat16)], pg=pg)


def _merge_weights(w_bn, w_bf, w_bd, w_o):
    hd = HEAD_DIM
    rows = []
    for c in range(NSA_GROUP):
        rows += list(range(hd * c, hd * c + hd)) + list(range(hd * (c + NSA_GROUP), hd * (c + NSA_GROUP) + hd))
    w_bn_p = jnp.take(w_bn, np.asarray(rows, np.int32), axis=0).astype(jnp.bfloat16)
    w_bd_p = jnp.pad(w_bd.reshape(DSA_HEADS, hd, D_MODEL), ((0, 0), (0, hd), (0, 0)))
    w_bd_p = w_bd_p.reshape(DSA_HEADS * LANES, D_MODEL).astype(jnp.bfloat16)
    return w_bn_p, w_bf.astype(jnp.bfloat16), w_bd_p, w_o.astype(jnp.bfloat16)


def _router_weights(w_rg, b_rg, w_re, b_re):
    pad = LANES - N_EXPERTS - N_GROUPS
    w = jnp.concatenate([w_re, w_rg, jnp.zeros((D_MODEL, pad), w_re.dtype)], axis=1).astype(jnp.float32)
    b = jnp.concatenate([b_re, b_rg, jnp.zeros((pad,), b_re.dtype)]).astype(jnp.float32)[None, :]
    return w, b


TM_PROJ = 512
TQ_FOX = 512
TQ_NSA = 256
TQ_DSA = 256
TK_ATTN = 1024
TM_MERGE = 512
TM_MOE = 1024
PAGES_PER_STEP = 64


def _hi_lo_rows(q):
    hi = q.astype(jnp.bfloat16)
    lo = (q - hi.astype(jnp.float32)).astype(jnp.bfloat16)
    return jnp.concatenate([hi, lo], axis=1)


def _dsa_rows(dki, misc):
    return jnp.concatenate([dki[:, :HEAD_DIM], misc[:, :HEAD_DIM], dki[:, HEAD_DIM:]], axis=-1)


def kernel(x_prompt, x_sample, cache_nsa, cache_fox, cache_fox_logf, cache_dsa, cache_win, page_table, norm_attn_g, w_in, nsa_qk_g, fox_qk_g, fox_b_f, dsa_qk_g, w_branch_nsa, w_branch_fox, w_branch_dsa, w_out, norm_ffn_g, w_router_group, b_router_group, w_router_expert, b_router_expert, w_exp_gate, w_exp_up, w_exp_down):
    depth = w_in.shape[0]
    B, S, D = x_prompt.shape
    Bs, T, _ = x_sample.shape
    assert T == 1 and D == D_MODEL
    n_pages = page_table.shape[1]
    past_len = n_pages * PAGE_SIZE
    win_keep = cache_win.shape[2]
    n_phys = cache_nsa.shape[1]
    bf16 = jnp.bfloat16
    hd = HEAD_DIM

    tm_proj = min(TM_PROJ, S)
    cos_p, sin_p = _rope_tables(jnp.arange(S))
    cos_s, sin_s = _rope_tables(jnp.full((Bs,), past_len))
    pg = min(PAGES_PER_STEP, n_pages)
    c_nsa = jnp.moveaxis(cache_nsa.reshape(depth, n_phys, PAGE_SIZE, 4 * LANES), 2, 3)
    c_fox = jnp.moveaxis(cache_fox.reshape(depth, n_phys, PAGE_SIZE, 2 * FOX_HEADS * hd), 2, 3)
    c_lf = jnp.moveaxis(cache_fox_logf.astype(jnp.float32), 2, 3)
    c_dsa = jnp.moveaxis(cache_dsa.reshape(depth, n_phys, PAGE_SIZE, 3 * hd), 2, 3)
    c_win = jnp.moveaxis(cache_win.reshape(depth, Bs, win_keep, 2 * LANES), 2, 3)
    lane = jnp.arange(LANES)
    left = lane < hd

    xp = x_prompt.reshape(B * S, D)
    xs = x_sample.reshape(Bs, D)
    outs = [[] for _ in range(10)]
    for l in range(depth):
        w_perm, w_mg, gain_row, bf_row = _layer_tables(w_in[l], nsa_qk_g[l], fox_qk_g[l], fox_b_f[l], dsa_qk_g[l])
        w_b = _merge_weights(w_branch_nsa[l], w_branch_fox[l], w_branch_dsa[l], w_out[l])
        w_r, b_r = _router_weights(w_router_group[l], b_router_group[l], w_router_expert[l], b_router_expert[l])
        w_e = (w_exp_gate[l].astype(bf16), w_exp_up[l].astype(bf16), w_exp_down[l].astype(bf16))
        g_attn = norm_attn_g[l][None, :]
        g_ffn = norm_ffn_g[l][None, :]

        (qn, nsa_rows, win_rows, slc_kv, win_kv, cmp, qf, fox_rows, fox_kv, dqi, dki, dki_bf, misc, misc_bf,
         cum) = _project(xp, g_attn, w_perm.astype(bf16), gain_row, cos_p, sin_p, bf_row,
                         tm=tm_proj, tiles_per_batch=S // tm_proj, with_blocks=True)
        cum_t = cum[:, _M_FF:_M_IW].reshape(B, S, FOX_HEADS).transpose(0, 2, 1)
        o_fox = _fox_prompt(qf, fox_kv, cum, cum_t, batch=B, seq=S, tq=min(TQ_FOX, S), tk=min(TK_ATTN, S))
        o_nsa = _nsa_prompt(qn, misc, cmp.astype(bf16), slc_kv, win_kv, batch=B, seq=S, tq=min(TQ_NSA, S),
                            tk=min(TK_ATTN, S))
        o_dsa = _dsa_prompt(dqi, misc, dki_bf, misc_bf, batch=B, seq=S, tq=min(TQ_DSA, S), tk=min(TK_ATTN, S),
                            n_keep=min(DSA_TOPK, S // 4))
        xp = _merge(xp, g_attn, o_nsa, o_fox, o_dsa, w_mg, *w_b, tm=min(TM_MERGE, B * S))
        xp = _moe(xp, g_ffn, w_r, b_r, *w_e, tm=min(TM_MOE, B * S))
        outs[0].append(nsa_rows.reshape(B, S, 4, NSA_KV_HEADS, hd))
        outs[2].append(fox_rows.reshape(B, S, 2, FOX_HEADS, hd))
        outs[4].append(misc[:, _M_FF:_M_IW].reshape(B, S, FOX_HEADS))
        outs[6].append(_dsa_rows(dki, misc).reshape(B, S, 3, hd))
        outs[8].append(win_rows.reshape(B, S, 2, NSA_KV_HEADS, hd)[:, S - min(NSA_WINDOW, S):])

        (qn, nsa_rows, win_rows, _, _, _, qf, fox_rows, _, dqi, dki, _, misc, _, _) = _project(
            xs, g_attn, w_perm, gain_row, cos_s, sin_s, bf_row, tm=Bs, tiles_per_batch=1, with_blocks=False)
        chunks = qn.reshape(Bs, NSA_GROUP, LANES)
        qbd = _hi_lo_rows(jnp.concatenate([jnp.where(left, chunks, 0), jnp.where(left, 0, chunks)], axis=1))
        gates = misc[:, _M_GATE:_M_FF].reshape(Bs, 3, NSA_HEADS).transpose(0, 2, 1)
        gates = jnp.pad(gates, ((0, 0), (0, 0), (0, LANES - 3)))
        o = _nsa_decode(page_table, qbd, gates, nsa_rows[:, :, None], win_rows[:, :, None], c_win[l], c_nsa, l, pg=pg)
        o_nsa = jnp.where(left, o[:, :NSA_GROUP], o[:, NSA_GROUP:]).reshape(Bs, 4 * LANES).astype(bf16)

        head_of_lane = jnp.arange(FOX_HEADS * hd) // hd
        qbd_f = jnp.where(head_of_lane[None, None, :] == jnp.arange(FOX_HEADS)[None, :, None], qf[:, None, :], 0)
        qbd_f = qbd_f.astype(bf16)
        o = _fox_decode(page_table, qbd_f, fox_rows[:, :, None], misc[:, None, _M_FF:_M_IW], c_fox, c_lf, l, pg=pg)
        o_fox = jnp.einsum('bhhd->bhd', o.reshape(Bs, FOX_HEADS, FOX_HEADS, hd)).reshape(Bs, FOX_HEADS * hd).astype(bf16)

        chunks = dqi.reshape(Bs, DSA_HEADS, LANES)
        q_att = jnp.where(left, chunks, 0).astype(bf16)
        q_idx = _hi_lo_rows(chunks[..., hd:])
        dsa_new = _dsa_rows(dki, misc)
        o = _dsa_decode(page_table, q_att, q_idx, misc[:, _M_IW:_M_END, None], dsa_new[:, :, None], c_dsa, l,
                        pg=pg, n_keep=min(DSA_TOPK, (past_len + 1) // 4))
        o_dsa = jnp.concatenate([o[..., hd:], jnp.zeros_like(o[..., hd:])], axis=-1).reshape(Bs, 4 * LANES).astype(bf16)

        xs = _merge(xs, g_attn, o_nsa, o_fox, o_dsa, w_mg, *w_b, tm=Bs)
        xs = _moe(xs, g_ffn, w_r, b_r, *w_e, tm=Bs)
        outs[1].append(nsa_rows.reshape(Bs, 1, 4, NSA_KV_HEADS, hd))
        outs[3].append(fox_rows.reshape(Bs, 1, 2, FOX_HEADS, hd))
        outs[5].append(misc[:, _M_FF:_M_IW].reshape(Bs, 1, FOX_HEADS))
        outs[7].append(dsa_new.reshape(Bs, 1, 3, hd))
        win_all = jnp.concatenate([cache_win[l], win_rows.reshape(Bs, 1, 2, NSA_KV_HEADS, hd)], axis=1)
        outs[9].append(win_all[:, 1:])
    return (xp.reshape(B, S, D), xs.reshape(Bs, 1, D)) + tuple(jnp.stack(o) for o in outs)
```

```python
import functools

import numpy as np
import jax
import jax.numpy as jnp
from jax import lax
from jax.experimental import pallas as pl
from jax.experimental.pallas import tpu as pltpu

D_MODEL = 1024
HEAD_DIM = 64
HALF = HEAD_DIM // 2
NSA_HEADS = 8
NSA_KV_HEADS = 2
NSA_GROUP = NSA_HEADS // NSA_KV_HEADS
NSA_BLOCK = 64
NSA_TOP_BLOCKS = 16
NSA_WINDOW = 512
NSA_FORCE_BONUS = 8.0
FOX_HEADS = 4
DSA_HEADS = 4
DSA_IDX_HEADS = 4
DSA_TOPK = 256
N_GROUPS = 4
EXPERTS_PER_GROUP = 4
N_EXPERTS = N_GROUPS * EXPERTS_PER_GROUP
D_EXPERT = 512
ROPE_THETA = 10000.0
NORM_EPS = 1e-6
NEG_BIG = -1e30
PAGE_SIZE = 128
QK_SCALE = HEAD_DIM ** -0.5

LANES = 128
VMEM_LIMIT = 56 * 1024 * 1024

_O_NQ = 0
_O_NKV = _O_NQ + NSA_HEADS * HEAD_DIM
_O_NGATE = _O_NKV + 6 * NSA_KV_HEADS * HEAD_DIM
_O_FQKV = _O_NGATE + 3 * NSA_HEADS
_O_FF = _O_FQKV + 3 * FOX_HEADS * HEAD_DIM
_O_DQ = _O_FF + FOX_HEADS
_O_DKV = _O_DQ + DSA_HEADS * HEAD_DIM
_O_DIQ = _O_DKV + 2 * HEAD_DIM
_O_DIK = _O_DIQ + DSA_IDX_HEADS * HEAD_DIM
_O_DIW = _O_DIK + HEAD_DIM
_O_MG = _O_DIW + DSA_IDX_HEADS
D_IN = _O_MG + 3 * D_MODEL

_M_GATE = HEAD_DIM
_M_FF = _M_GATE + 3 * NSA_HEADS
_M_IW = _M_FF + FOX_HEADS
_M_END = _M_IW + DSA_IDX_HEADS

_C_NQ = 0
_C_NKV = 4
_C_FQ = 10
_C_FK = 12
_C_FV = 14
_C_DQI = 16
_C_DKI = 20
_C_MISC = 21
N_CHUNKS = 22
D_PROJ = N_CHUNKS * LANES


def _proj_column_permutation():
    idx = []
    for c in range(4):
        idx += list(range(_O_NQ + 64 * c, _O_NQ + 64 * c + 64))
        idx += list(range(_O_NQ + 64 * (c + 4), _O_NQ + 64 * (c + 4) + 64))
    idx += list(range(_O_NKV, _O_NKV + 768))
    idx += list(range(_O_FQKV, _O_FQKV + 768))
    for h in range(4):
        idx += list(range(_O_DQ + 64 * h, _O_DQ + 64 * h + 64))
        idx += list(range(_O_DIQ + 64 * h, _O_DIQ + 64 * h + 64))
    idx += list(range(_O_DKV, _O_DKV + 64)) + list(range(_O_DIK, _O_DIK + 64))
    idx += list(range(_O_DKV + 64, _O_DKV + 128))
    idx += list(range(_O_NGATE, _O_NGATE + 24)) + list(range(_O_FF, _O_FF + 4)) + list(range(_O_DIW, _O_DIW + 4))
    idx += [D_IN] * (LANES - _M_END)
    assert len(idx) == D_PROJ
    return np.asarray(idx, np.int32)


def _rope_tables(pos):
    inv = ROPE_THETA ** (-jnp.arange(HALF, dtype=jnp.float32) * 2.0 / HEAD_DIM)
    ang = pos.astype(jnp.float32)[:, None] * inv[None, :]
    cos, sin = jnp.cos(ang), jnp.sin(ang)
    cos = jnp.concatenate([cos, cos, cos, cos], axis=-1)
    sin = jnp.concatenate([-sin, sin, -sin, sin], axis=-1)
    return cos, sin


def _rms_rows(x, g):
    return x * lax.rsqrt(jnp.mean(jnp.square(x), axis=-1, keepdims=True) + NORM_EPS) * g


def _split_dot(a, b_bf16):
    hi = a.astype(jnp.bfloat16)
    lo = (a - hi.astype(jnp.float32)).astype(jnp.bfloat16)
    return (jnp.dot(hi, b_bf16, preferred_element_type=jnp.float32)
            + jnp.dot(lo, b_bf16, preferred_element_type=jnp.float32))


def _proj_kernel(x_ref, g_ref, w_ref, gain_ref, cos_ref, sin_ref, bd_ref, bf_ref, tri_ref,
                 qn_ref, nsa_rows_ref, win_rows_ref, slc_kv_ref, win_kv_ref, cmp_ref,
                 qf_ref, fox_rows_ref, fox_kv_ref, dqi_ref, dki_ref, dki_bf_ref,
                 misc_ref, misc_bf_ref, cum_ref, carry_ref, *, tiles_per_batch, with_blocks):
    i = pl.program_id(0)
    precise = w_ref.dtype == jnp.float32
    xn = _rms_rows(x_ref[...], g_ref[...])
    if not precise:
        xn = xn.astype(jnp.bfloat16)
    lane = lax.broadcasted_iota(jnp.int32, (1, LANES), 1)
    left = lane < HEAD_DIM
    first_half = (lane % HEAD_DIM) < HALF
    cos = cos_ref[...]
    sin = sin_ref[...]
    bd = bd_ref[...]

    def chunk(c):
        return jnp.dot(xn, w_ref[:, c * LANES:(c + 1) * LANES], preferred_element_type=jnp.float32,
                       precision=lax.Precision.HIGHEST if precise else None)

    def head_norm(h, c, only_left=False):
        ms = _split_dot(h * h, bd)
        y = h * lax.rsqrt(ms + NORM_EPS) * gain_ref[:, c * LANES:(c + 1) * LANES]
        return jnp.where(left, y, h) if only_left else y

    def rope(h):
        swapped = jnp.where(first_half, pltpu.roll(h, LANES - HALF, 1), pltpu.roll(h, HALF, 1))
        return h * cos + swapped * sin

    for c in range(4):
        q = rope(head_norm(chunk(_C_NQ + c), _C_NQ + c)) * QK_SCALE
        qn_ref[:, c * LANES:(c + 1) * LANES] = q.astype(qn_ref.dtype)

    kc = rope(head_norm(chunk(_C_NKV + 0), _C_NKV + 0))
    vc = chunk(_C_NKV + 1)
    ks = rope(head_norm(chunk(_C_NKV + 2), _C_NKV + 2))
    vs = chunk(_C_NKV + 3)
    kw = rope(head_norm(chunk(_C_NKV + 4), _C_NKV + 4))
    vw = chunk(_C_NKV + 5)
    nsa_rows_ref[:, 0 * LANES:1 * LANES] = kc
    nsa_rows_ref[:, 1 * LANES:2 * LANES] = vc
    nsa_rows_ref[:, 2 * LANES:3 * LANES] = ks
    nsa_rows_ref[:, 3 * LANES:4 * LANES] = vs
    win_rows_ref[:, 0:LANES] = kw
    win_rows_ref[:, LANES:2 * LANES] = vw
    slc_kv_ref[:, 0:LANES] = ks.astype(jnp.bfloat16)
    slc_kv_ref[:, LANES:2 * LANES] = vs.astype(jnp.bfloat16)
    win_kv_ref[:, 0:LANES] = kw.astype(jnp.bfloat16)
    win_kv_ref[:, LANES:2 * LANES] = vw.astype(jnp.bfloat16)
    if with_blocks:
        tm = kc.shape[0]
        nblk = tm // NSA_BLOCK
        cmp_ref[:, 0:LANES] = jnp.mean(kc.reshape(nblk, NSA_BLOCK, LANES), axis=1)
        cmp_ref[:, LANES:2 * LANES] = jnp.mean(vc.reshape(nblk, NSA_BLOCK, LANES), axis=1)
    else:
        cmp_ref[...] = jnp.zeros_like(cmp_ref)

    for c in range(2):
        qf = head_norm(chunk(_C_FQ + c), _C_FQ + c) * QK_SCALE
        qf_ref[:, c * LANES:(c + 1) * LANES] = qf.astype(qf_ref.dtype)
        fk = head_norm(chunk(_C_FK + c), _C_FK + c)
        fv = chunk(_C_FV + c)
        fox_rows_ref[:, c * LANES:(c + 1) * LANES] = fk
        fox_rows_ref[:, (2 + c) * LANES:(3 + c) * LANES] = fv
        fox_kv_ref[:, c * LANES:(c + 1) * LANES] = fk.astype(jnp.bfloat16)
        fox_kv_ref[:, (2 + c) * LANES:(3 + c) * LANES] = fv.astype(jnp.bfloat16)

    for c in range(4):
        dqi = rope(head_norm(chunk(_C_DQI + c), _C_DQI + c, only_left=True)) * QK_SCALE
        dqi_ref[:, c * LANES:(c + 1) * LANES] = dqi.astype(dqi_ref.dtype)

    dki = rope(head_norm(chunk(_C_DKI), _C_DKI, only_left=True))
    dki_ref[...] = dki
    dki_bf_ref[...] = dki.astype(jnp.bfloat16)

    m = chunk(_C_MISC)
    zf = m + bf_ref[...]
    logf = jnp.minimum(zf, 0.0) - jnp.log(1.0 + jnp.exp(-jnp.abs(zf)))
    is_gate = (lane >= _M_GATE) & (lane < _M_FF)
    is_ff = (lane >= _M_FF) & (lane < _M_IW)
    is_iw = (lane >= _M_IW) & (lane < _M_END)
    out = jnp.where(is_gate, jax.nn.sigmoid(m), m)
    out = jnp.where(is_ff, logf, out)
    out = jnp.where(is_iw, m * (DSA_IDX_HEADS ** -0.5), out)
    misc_ref[...] = out
    misc_bf_ref[...] = out.astype(jnp.bfloat16)

    @pl.when(i % tiles_per_batch == 0)
    def _():
        carry_ref[...] = jnp.zeros_like(carry_ref)

    lf = jnp.where(is_ff, logf, 0.0)
    cum = jnp.dot(tri_ref[...], lf, preferred_element_type=jnp.float32,
                  precision=lax.Precision.HIGHEST) + carry_ref[...]
    cum_ref[...] = cum
    carry_ref[...] = cum[cum.shape[0] - 1:, :]


def _project(x2d, norm_g, w_perm, gain_row, cos, sin, bf_row, *, tm, tiles_per_batch, with_blocks):
    n = x2d.shape[0]
    nt = n // tm
    ncmp = max(tm // NSA_BLOCK, 8) if not with_blocks else tm // NSA_BLOCK
    bd = jnp.asarray(np.kron(np.eye(2), np.full((HEAD_DIM, HEAD_DIM), 1.0 / HEAD_DIM)), jnp.bfloat16)
    tri = jnp.asarray(np.tril(np.ones((tm, tm))), jnp.float32)
    f32, bf16 = jnp.float32, jnp.bfloat16
    row = lambda w: pl.BlockSpec((tm, w), lambda i: (i, 0))
    full = lambda a: pl.BlockSpec(a.shape, lambda i: (0,) * a.ndim)
    pos_spec = pl.BlockSpec((tm, LANES), lambda i: (i % tiles_per_batch, 0))
    qdt = w_perm.dtype
    outs = [
        (512, qdt),
        (512, f32),
        (256, f32),
        (256, bf16),
        (256, bf16),
        None,
        (256, qdt),
        (512, f32),
        (512, bf16),
        (512, qdt),
        (128, f32),
        (128, bf16),
        (128, f32),
        (128, bf16),
        (128, f32),
    ]
    out_shape, out_specs = [], []
    for o in outs:
        if o is None:
            out_shape.append(jax.ShapeDtypeStruct((nt * ncmp, 2 * LANES), f32))
            out_specs.append(pl.BlockSpec((ncmp, 2 * LANES), lambda i: (i, 0)))
        else:
            out_shape.append(jax.ShapeDtypeStruct((n, o[0]), o[1]))
            out_specs.append(row(o[0]))
    return pl.pallas_call(
        functools.partial(_proj_kernel, tiles_per_batch=tiles_per_batch, with_blocks=with_blocks),
        grid=(nt,),
        in_specs=[row(D_MODEL), full(norm_g), full(w_perm), full(gain_row), pos_spec, pos_spec,
                  full(bd), full(bf_row), full(tri)],
        out_specs=out_specs,
        out_shape=out_shape,
        scratch_shapes=[pltpu.VMEM((1, LANES), f32)],
        compiler_params=pltpu.CompilerParams(dimension_semantics=("arbitrary",),
                                             vmem_limit_bytes=VMEM_LIMIT),
        name="proj",
    )(x2d, norm_g, w_perm, gain_row, cos, sin, bd, bf_row, tri)


def _layer_tables(w_in_l, nsa_qk_g_l, fox_qk_g_l, fox_b_f_l, dsa_qk_g_l):
    perm = _proj_column_permutation()
    w_ext = jnp.concatenate([w_in_l, jnp.zeros((D_MODEL, 1), w_in_l.dtype)], axis=1)
    w_perm = jnp.take(w_ext, perm, axis=1).astype(jnp.float32)
    w_mg = w_in_l[:, _O_MG:].astype(jnp.float32)
    one = jnp.ones((HEAD_DIM,), jnp.float32)
    two = lambda g: jnp.concatenate([g, g])
    ones2 = two(one)
    parts = [two(nsa_qk_g_l[0])] * 4
    parts += [two(nsa_qk_g_l[1]), ones2, two(nsa_qk_g_l[2]), ones2, two(nsa_qk_g_l[3]), ones2]
    parts += [two(fox_qk_g_l[0])] * 2 + [two(fox_qk_g_l[1])] * 2 + [ones2] * 2
    parts += [jnp.concatenate([dsa_qk_g_l[0], one])] * 4
    parts += [jnp.concatenate([dsa_qk_g_l[1], one]), ones2]
    gain_row = jnp.concatenate(parts).astype(jnp.float32)[None, :]
    bf_row = jnp.zeros((LANES,), jnp.float32).at[_M_FF:_M_IW].set(fox_b_f_l.astype(jnp.float32))[None, :]
    return w_perm, w_mg, gain_row, bf_row


_NT = (((1,), (1,)), ((), ()))


def _half_masks():
    lane = lax.broadcasted_iota(jnp.int32, (1, LANES), 1)
    return lane < HEAD_DIM, lane >= HEAD_DIM


def _online_update(s, valid, v, m_prev, l_prev, acc_prev):
    s = jnp.where(valid, s, -jnp.inf)
    m_new = jnp.maximum(m_prev, jnp.max(s, axis=-1, keepdims=True))
    p = jnp.exp(s - m_new)
    alpha = jnp.exp(m_prev - m_new)
    l_new = alpha * l_prev + jnp.sum(p, axis=-1, keepdims=True)
    pv = jnp.dot(p.reshape(-1, p.shape[-1]).astype(jnp.bfloat16), v, preferred_element_type=jnp.float32)
    acc_new = alpha * acc_prev + pv.reshape(acc_prev.shape)
    return m_new, l_new, acc_new


def _two_stage_loop(n, produce, consume, buf0, buf1):
    produce(0, buf0)

    def pair(j, _):
        c = 2 * j
        produce(c + 1, buf1)
        consume(c, buf0)
        produce(c + 2, buf0)
        consume(c + 1, buf1)
        return 0

    pairs = (n - 1) // 2
    lax.fori_loop(0, pairs, pair, 0)
    c = 2 * pairs

    @pl.when(c + 1 < n)
    def _():
        produce(c + 1, buf1)
        consume(c, buf0)
        consume(c + 1, buf1)

    @pl.when(c + 1 >= n)
    def _():
        consume(c, buf0)


def _finish(l, acc):
    return acc / jnp.where(l > 0, l, 1.0)


def _fox_kernel(q_ref, kv_ref, cq_ref, ck_ref, o_ref, m_ref, l_ref, acc_ref, *, tq, tk):
    i = pl.program_id(1)
    j = pl.program_id(2)
    left, right = _half_masks()

    @pl.when(j == 0)
    def _():
        m_ref[...] = jnp.full_like(m_ref, NEG_BIG)
        l_ref[...] = jnp.zeros_like(l_ref)
        acc_ref[...] = jnp.zeros_like(acc_ref)

    @pl.when(j * tk < (i + 1) * tq)
    def _():
        rows = i * tq + lax.broadcasted_iota(jnp.int32, (tq, 1), 0)
        cols = j * tk + lax.broadcasted_iota(jnp.int32, (1, tk), 1)
        valid = cols <= rows
        for h in range(FOX_HEADS):
            c = h // 2
            qc = q_ref[:, c * LANES:(c + 1) * LANES]
            qh = jnp.where(left if h % 2 == 0 else right, qc, jnp.zeros_like(qc))
            k = kv_ref[:, c * LANES:(c + 1) * LANES]
            v = kv_ref[:, (2 + c) * LANES:(3 + c) * LANES]
            s = lax.dot_general(qh, k, _NT, preferred_element_type=jnp.float32)
            s = s + cq_ref[:, _M_FF + h:_M_FF + h + 1] - ck_ref[h:h + 1, :]
            m_ref[h], l_ref[h], acc_ref[h] = _online_update(s, valid, v, m_ref[h], l_ref[h], acc_ref[h])

    @pl.when(j == pl.num_programs(2) - 1)
    def _():
        for c in range(FOX_HEADS // 2):
            o = jnp.where(left, _finish(l_ref[2 * c], acc_ref[2 * c]), _finish(l_ref[2 * c + 1], acc_ref[2 * c + 1]))
            o_ref[:, c * LANES:(c + 1) * LANES] = o.astype(o_ref.dtype)


def _fox_prompt(qf, fox_kv, cum, cum_t, *, batch, seq, tq, tk):
    nq = seq // tq
    nk = seq // tk
    f32 = jnp.float32
    last = lambda i, j: jnp.minimum(j, ((i + 1) * tq - 1) // tk)
    return pl.pallas_call(
        functools.partial(_fox_kernel, tq=tq, tk=tk),
        grid=(batch, nq, nk),
        in_specs=[pl.BlockSpec((tq, 2 * LANES), lambda b, i, j: (b * nq + i, 0)),
                  pl.BlockSpec((tk, 4 * LANES), lambda b, i, j: (b * nk + last(i, j), 0)),
                  pl.BlockSpec((tq, LANES), lambda b, i, j: (b * nq + i, 0)),
                  pl.BlockSpec((None, FOX_HEADS, tk), lambda b, i, j: (b, 0, last(i, j)))],
        out_specs=pl.BlockSpec((tq, 2 * LANES), lambda b, i, j: (b * nq + i, 0)),
        out_shape=jax.ShapeDtypeStruct((batch * seq, 2 * LANES), jnp.bfloat16),
        scratch_shapes=[pltpu.VMEM((FOX_HEADS, tq, 1), f32), pltpu.VMEM((FOX_HEADS, tq, 1), f32),
                        pltpu.VMEM((FOX_HEADS, tq, LANES), f32)],
        compiler_params=pltpu.CompilerParams(dimension_semantics=("arbitrary", "arbitrary", "arbitrary")),
        name="fox_prompt",
    )(qf, fox_kv, cum, cum_t)


def _select_top_blocks(score, n_top):
    nb = score.shape[-1]
    blk = lax.broadcasted_iota(jnp.int32, (1, nb), 1).astype(jnp.float32)

    def body(_, carry):
        sc, sel = carry
        mx = jnp.max(sc, axis=-1, keepdims=True)
        first = jnp.min(jnp.where(sc == mx, blk, float(nb)), axis=-1, keepdims=True)
        pick = blk == first
        return jnp.where(pick, -jnp.inf, sc), jnp.where(pick, 1.0, sel)

    _, sel = lax.fori_loop(0, n_top, body, (score, jnp.zeros_like(score)))
    return sel


def _block_importance_scores(p_sum, qpos, nb):
    blk = lax.broadcasted_iota(jnp.int32, (1, nb), 1)
    cur = qpos // NSA_BLOCK
    forced = (blk == 0) | (blk == cur) | (blk == cur - 1)
    score = jnp.where(forced, p_sum + NSA_FORCE_BONUS, p_sum)
    return jnp.where(blk <= cur, score, -1.0)


def _masked_softmax(s, valid):
    s = jnp.where(valid, s, NEG_BIG)
    m = jnp.max(s, axis=-1, keepdims=True)
    e = jnp.where(valid, jnp.exp(s - m), 0.0)
    d = jnp.sum(e, axis=-1, keepdims=True)
    return e / jnp.where(d > 0, d, 1.0)


def _nsa_kernel(q_ref, misc_ref, cmp_ref, slc_ref, win_ref, o_ref, m_sc, l_sc, acc_sc, s0_sc, s1_sc,
                *, tq, tk, seq):
    i = pl.program_id(1)
    nb = seq // NSA_BLOCK
    g4 = NSA_GROUP
    left, right = _half_masks()
    qpos = i * tq + lax.broadcasted_iota(jnp.int32, (tq, 1), 0)
    blk = lax.broadcasted_iota(jnp.int32, (1, nb), 1)
    done = (blk + 1) * NSA_BLOCK <= qpos + 1
    kcm = cmp_ref[:, 0:LANES]
    vcm = cmp_ref[:, LANES:2 * LANES]
    lane_pos = lax.broadcasted_iota(jnp.int32, (1, tk), 1)
    n_top = min(NSA_TOP_BLOCKS, nb)
    n_chunks = ((i + 1) * tq + tk - 1) // tk
    tw = min(tq + NSA_WINDOW, seq)
    win_start = pl.multiple_of(jnp.clip(i * tq - NSA_WINDOW, 0, seq - tw), 16)
    win_pos = win_start + lax.broadcasted_iota(jnp.int32, (1, tw), 1)
    win_dist = qpos - win_pos
    win_valid = (win_dist >= 0) & (win_dist <= NSA_WINDOW)
    qs = [q_ref[:, c * LANES:(c + 1) * LANES] for c in range(g4)]
    qgs, o_cs, scores = [], [], []
    for g in range(NSA_KV_HEADS):
        hm = left if g == 0 else right
        qg = jnp.concatenate([jnp.where(hm, q, jnp.zeros_like(q)) for q in qs], axis=0)
        s_c = lax.dot_general(qg, kcm, _NT, preferred_element_type=jnp.float32).reshape(g4, tq, nb)
        p_c = _masked_softmax(s_c, done[None])
        o_cs.append(jnp.dot(p_c.reshape(g4 * tq, nb).astype(jnp.bfloat16), vcm,
                            preferred_element_type=jnp.float32).reshape(g4, tq, LANES))
        scores.append(_block_importance_scores(jnp.sum(p_c, axis=0), qpos, nb))
        qgs.append(qg)
    sel_all = _select_top_blocks(jnp.concatenate(scores, axis=0), n_top).astype(jnp.bfloat16)

    outs = []
    for g in range(NSA_KV_HEADS):
        qg, o_c = qgs[g], o_cs[g]
        sel = sel_all[g * tq:(g + 1) * tq]

        m_sc[...] = jnp.full_like(m_sc, NEG_BIG)
        l_sc[...] = jnp.zeros_like(l_sc)
        acc_sc[...] = jnp.zeros_like(acc_sc)

        def scores(c, dst, qg=qg):
            start = pl.multiple_of(c * tk, tk)
            dst[...] = lax.dot_general(qg, slc_ref[pl.ds(start, tk), 0:LANES], _NT,
                                       preferred_element_type=jnp.float32)

        def attend(c, src, sel=sel):
            start = pl.multiple_of(c * tk, tk)
            v = slc_ref[pl.ds(start, tk), LANES:2 * LANES]
            kpos = c * tk + lane_pos
            expand = (lax.broadcasted_iota(jnp.int32, (nb, 1), 0) == kpos // NSA_BLOCK)
            chosen = jnp.dot(sel, jnp.where(expand, 1.0, 0.0).astype(jnp.bfloat16),
                             preferred_element_type=jnp.float32) > 0.5
            valid = chosen & (kpos <= qpos)
            s = src[...].reshape(g4, tq, tk)
            m_sc[...], l_sc[...], acc_sc[...] = _online_update(s, valid[None], v, m_sc[...], l_sc[...], acc_sc[...])

        _two_stage_loop(n_chunks, scores, attend, s0_sc, s1_sc)
        o_s = _finish(l_sc[...], acc_sc[...])

        k = win_ref[pl.ds(win_start, tw), 0:LANES]
        v = win_ref[pl.ds(win_start, tw), LANES:2 * LANES]
        s_w = lax.dot_general(qg, k, _NT, preferred_element_type=jnp.float32).reshape(g4, tq, tw)
        p_w = _masked_softmax(s_w, win_valid[None])
        o_w = jnp.dot(p_w.reshape(g4 * tq, tw).astype(jnp.bfloat16), v,
                      preferred_element_type=jnp.float32).reshape(g4, tq, LANES)

        heads = []
        for c in range(g4):
            h = g * g4 + c
            gate = lambda br: misc_ref[:, _M_GATE + br * NSA_HEADS + h:_M_GATE + br * NSA_HEADS + h + 1]
            heads.append(gate(0) * o_c[c] + gate(1) * o_s[c] + gate(2) * o_w[c])
        outs.append(heads)
    for c in range(g4):
        o_ref[:, c * LANES:(c + 1) * LANES] = jnp.where(left, outs[0][c], outs[1][c]).astype(o_ref.dtype)


def _nsa_prompt(qn, misc, cmp_bf, slc_kv, win_kv, *, batch, seq, tq, tk):
    nq = seq // tq
    nb = seq // NSA_BLOCK
    f32 = jnp.float32
    g4 = NSA_GROUP
    return pl.pallas_call(
        functools.partial(_nsa_kernel, tq=tq, tk=tk, seq=seq),
        grid=(batch, nq),
        in_specs=[pl.BlockSpec((tq, 4 * LANES), lambda b, i: (b * nq + i, 0)),
                  pl.BlockSpec((tq, LANES), lambda b, i: (b * nq + i, 0)),
                  pl.BlockSpec((nb, 2 * LANES), lambda b, i: (b, 0)),
                  pl.BlockSpec((seq, 2 * LANES), lambda b, i: (b, 0)),
                  pl.BlockSpec((seq, 2 * LANES), lambda b, i: (b, 0))],
        out_specs=pl.BlockSpec((tq, 4 * LANES), lambda b, i: (b * nq + i, 0)),
        out_shape=jax.ShapeDtypeStruct((batch * seq, 4 * LANES), jnp.bfloat16),
        scratch_shapes=[pltpu.VMEM((g4, tq, 1), f32), pltpu.VMEM((g4, tq, 1), f32),
                        pltpu.VMEM((g4, tq, LANES), f32),
                        pltpu.VMEM((g4 * tq, tk), f32), pltpu.VMEM((g4 * tq, tk), f32)],
        compiler_params=pltpu.CompilerParams(dimension_semantics=("arbitrary", "arbitrary"),
                                             vmem_limit_bytes=VMEM_LIMIT),
        name="nsa_prompt",
    )(qn, misc, cmp_bf, slc_kv, win_kv)


_INT_MIN = -2 ** 31


def _sortable_key(x):
    bits = lax.bitcast_convert_type(x, jnp.int32)
    return jnp.where(bits < 0, bits ^ jnp.int32(0x7FFFFFFF), bits)


def _lane_fold(x):
    acc = x[:, 0:LANES]
    for c in range(1, x.shape[-1] // LANES):
        acc = acc + x[:, c * LANES:(c + 1) * LANES]
    return acc


def _dsa_kernel(dqi_ref, misc_ref, dki_ref, v_ref, upper_ref, o_ref, key_sc, seen_sc, m_sc, l_sc, acc_sc,
                s0_sc, s1_sc, *, tq, tk, n_keep):
    i = pl.program_id(1)
    nh = DSA_HEADS
    left, right = _half_masks()
    qpos = i * tq + lax.broadcasted_iota(jnp.int32, (tq, 1), 0)
    lane_pos = lax.broadcasted_iota(jnp.int32, (1, tk), 1)
    chunks = [dqi_ref[:, h * LANES:(h + 1) * LANES] for h in range(nh)]
    q_att = jnp.concatenate([jnp.where(left, q, jnp.zeros_like(q)) for q in chunks], axis=0)
    q_idx = jnp.concatenate([jnp.where(right, q, jnp.zeros_like(q)) for q in chunks], axis=0)
    n_chunks = ((i + 1) * tq + tk - 1) // tk

    def score_body(c, _):
        start = pl.multiple_of(c * tk, tk)
        kk = dki_ref[pl.ds(start, tk), :]
        a = lax.dot_general(q_idx, kk, _NT, preferred_element_type=jnp.float32).reshape(nh, tq, tk)
        a = jnp.maximum(a, 0.0)
        sc = a[0] * misc_ref[:, _M_IW:_M_IW + 1]
        for h in range(1, nh):
            sc = sc + a[h] * misc_ref[:, _M_IW + h:_M_IW + h + 1]
        kpos = c * tk + lane_pos
        sc = jnp.where(kpos <= qpos, sc, -jnp.inf)
        key_sc[c] = _sortable_key(sc)
        return 0

    lax.fori_loop(0, n_chunks, score_body, 0)

    def count(pred):
        def body(c, acc):
            return acc + _lane_fold(jnp.where(pred(c), 1.0, 0.0))
        part = lax.fori_loop(0, n_chunks, body, jnp.zeros((tq, LANES), jnp.float32))
        return jnp.sum(part, axis=-1, keepdims=True)

    def bit_body(it, lo):
        cand = lo + lax.shift_left(jnp.int32(1), jnp.int32(31) - it)
        cnt = count(lambda c: key_sc[c] >= cand)
        return jnp.where(cnt >= n_keep, cand, lo)

    thr = lax.fori_loop(0, 32, bit_body, jnp.full((tq, 1), _INT_MIN, jnp.int32))

    need = n_keep - count(lambda c: key_sc[c] > thr)
    seen_sc[...] = jnp.zeros_like(seen_sc)
    m_sc[...] = jnp.full_like(m_sc, NEG_BIG)
    l_sc[...] = jnp.zeros_like(l_sc)
    acc_sc[...] = jnp.zeros_like(acc_sc)

    def scores(c, dst):
        start = pl.multiple_of(c * tk, tk)
        dst[...] = lax.dot_general(q_att, dki_ref[pl.ds(start, tk), :], _NT, preferred_element_type=jnp.float32)

    def attend(c, src):
        start = pl.multiple_of(c * tk, tk)
        vv = v_ref[pl.ds(start, tk), :]
        kpos = c * tk + lane_pos
        key = key_sc[c]
        tie = key == thr
        rank = seen_sc[...] + jnp.dot(jnp.where(tie, 1.0, 0.0).astype(jnp.bfloat16), upper_ref[...],
                                      preferred_element_type=jnp.float32)
        seen_sc[...] = rank[:, tk - 1:tk]
        valid = ((key > thr) | (tie & (rank <= need))) & (kpos <= qpos)
        s = src[...].reshape(nh, tq, tk)
        m_sc[...], l_sc[...], acc_sc[...] = _online_update(s, valid[None], vv, m_sc[...], l_sc[...], acc_sc[...])

    _two_stage_loop(n_chunks, scores, attend, s0_sc, s1_sc)
    o = _finish(l_sc[...], acc_sc[...])
    for h in range(nh):
        o_ref[:, h * LANES:(h + 1) * LANES] = jnp.where(left, o[h], 0.0).astype(o_ref.dtype)


def _dsa_prompt(dqi, misc, dki_bf, misc_bf, *, batch, seq, tq, tk, n_keep):
    nq = seq // tq
    nk = seq // tk
    f32 = jnp.float32
    nh = DSA_HEADS
    upper = jnp.asarray(np.triu(np.ones((tk, tk))), jnp.bfloat16)
    return pl.pallas_call(
        functools.partial(_dsa_kernel, tq=tq, tk=tk, n_keep=n_keep),
        grid=(batch, nq),
        in_specs=[pl.BlockSpec((tq, 4 * LANES), lambda b, i: (b * nq + i, 0)),
                  pl.BlockSpec((tq, LANES), lambda b, i: (b * nq + i, 0)),
                  pl.BlockSpec((seq, LANES), lambda b, i: (b, 0)),
                  pl.BlockSpec((seq, LANES), lambda b, i: (b, 0)),
                  pl.BlockSpec((tk, tk), lambda b, i: (0, 0))],
        out_specs=pl.BlockSpec((tq, 4 * LANES), lambda b, i: (b * nq + i, 0)),
        out_shape=jax.ShapeDtypeStruct((batch * seq, 4 * LANES), jnp.bfloat16),
        scratch_shapes=[pltpu.VMEM((nk, tq, tk), jnp.int32), pltpu.VMEM((tq, 1), f32),
                        pltpu.VMEM((nh, tq, 1), f32), pltpu.VMEM((nh, tq, 1), f32),
                        pltpu.VMEM((nh, tq, LANES), f32),
                        pltpu.VMEM((nh * tq, tk), f32), pltpu.VMEM((nh * tq, tk), f32)],
        compiler_params=pltpu.CompilerParams(dimension_semantics=("arbitrary", "arbitrary"),
                                             vmem_limit_bytes=VMEM_LIMIT),
        name="dsa_prompt",
    )(dqi, misc, dki_bf, misc_bf, upper)


def _merge_kernel(x_ref, g_ref, on_ref, of_ref, od_ref, wmg_ref, wbn_ref, wbf_ref, wbd_ref, wo_ref, out_ref):
    precise = wo_ref.dtype == jnp.float32
    cdt = jnp.float32 if precise else jnp.bfloat16
    dot = functools.partial(jnp.dot, preferred_element_type=jnp.float32,
                            precision=lax.Precision.HIGHEST if precise else None)
    x = x_ref[...]
    xn = _rms_rows(x, g_ref[...]).astype(cdt)
    y = None
    for br, (o_ref, wb_ref) in enumerate(((on_ref, wbn_ref), (of_ref, wbf_ref), (od_ref, wbd_ref))):
        gate = jax.nn.sigmoid(dot(xn, wmg_ref[:, br * D_MODEL:(br + 1) * D_MODEL]))
        term = gate * dot(o_ref[...], wb_ref[...])
        y = term if y is None else y + term
    out_ref[...] = x + dot(y.astype(cdt), wo_ref[...])


def _merge(x2d, norm_g, o_nsa, o_fox, o_dsa, w_mg, w_bn, w_bf, w_bd, w_o, *, tm):
    n = x2d.shape[0]
    row = lambda a: pl.BlockSpec((tm, a.shape[1]), lambda i: (i, 0))
    full = lambda a: pl.BlockSpec(a.shape, lambda i: (0,) * a.ndim)
    args = (x2d, norm_g, o_nsa, o_fox, o_dsa, w_mg, w_bn, w_bf, w_bd, w_o)
    return pl.pallas_call(
        _merge_kernel,
        grid=(n // tm,),
        in_specs=[row(x2d), full(norm_g), row(o_nsa), row(o_fox), row(o_dsa)] + [full(a) for a in args[5:]],
        out_specs=row(x2d),
        out_shape=jax.ShapeDtypeStruct(x2d.shape, x2d.dtype),
        compiler_params=pltpu.CompilerParams(dimension_semantics=("arbitrary",), vmem_limit_bytes=VMEM_LIMIT),
        name="merge",
    )(*args)


_R_GROUP = N_EXPERTS


def _route(logits):
    lane = lax.broadcasted_iota(jnp.int32, (1, LANES), 1)
    lanef = lane.astype(jnp.float32)
    is_grp = (lane >= _R_GROUP) & (lane < _R_GROUP + N_GROUPS)
    lg = jnp.where(is_grp, logits, -jnp.inf)
    gmax = jnp.max(lg, axis=-1, keepdims=True)
    grp = jnp.min(jnp.where(lg == gmax, lanef, float(LANES)), axis=-1, keepdims=True) - _R_GROUP
    g1 = 1.0 / jnp.sum(jnp.where(is_grp, jnp.exp(lg - gmax), 0.0), axis=-1, keepdims=True)
    in_grp = (lane < N_EXPERTS) & ((lane // EXPERTS_PER_GROUP).astype(jnp.float32) == grp)
    le = jnp.where(in_grp, logits, -jnp.inf)
    m1 = jnp.max(le, axis=-1, keepdims=True)
    i1 = jnp.min(jnp.where(le == m1, lanef, float(LANES)), axis=-1, keepdims=True)
    le2 = jnp.where(lanef == i1, -jnp.inf, le)
    m2 = jnp.max(le2, axis=-1, keepdims=True)
    i2 = jnp.min(jnp.where(le2 == m2, lanef, float(LANES)), axis=-1, keepdims=True)
    t = jnp.exp(m2 - m1)
    w1 = g1 * (1.0 / (1.0 + t))
    w2 = g1 * (t / (1.0 + t))
    return jnp.where(lanef == i1, w1, jnp.where(lanef == i2, w2, 0.0))


def _moe_kernel(x_ref, g_ref, wr_ref, br_ref, wg_ref, wu_ref, wd_ref, out_ref, xn_sc, comb_sc, acc_sc):
    e = pl.program_id(1)

    @pl.when(e == 0)
    def _():
        x = x_ref[...]
        xn = _rms_rows(x, g_ref[...])
        logits = jnp.dot(xn, wr_ref[...], preferred_element_type=jnp.float32,
                         precision=lax.Precision.HIGHEST) + br_ref[...]
        comb_sc[...] = _route(logits)
        xn_sc[...] = xn.astype(xn_sc.dtype)
        acc_sc[...] = x

    dot = functools.partial(jnp.dot, preferred_element_type=jnp.float32,
                            precision=lax.Precision.HIGHEST if wg_ref.dtype == jnp.float32 else None)
    xn = xn_sc[...]
    h = jax.nn.silu(dot(xn, wg_ref[...])) * dot(xn, wu_ref[...])
    y = dot(h.astype(xn.dtype), wd_ref[...])
    lane = lax.broadcasted_iota(jnp.int32, (1, LANES), 1)
    ce = jnp.sum(jnp.where(lane == e, comb_sc[...], 0.0), axis=-1, keepdims=True)
    acc_sc[...] += ce * y

    @pl.when(e == pl.num_programs(1) - 1)
    def _():
        out_ref[...] = acc_sc[...]


def _moe(x2d, norm_g, w_router, b_router, w_gate, w_up, w_down, *, tm):
    n = x2d.shape[0]
    f32 = jnp.float32
    return pl.pallas_call(
        _moe_kernel,
        grid=(n // tm, N_EXPERTS),
        in_specs=[pl.BlockSpec((tm, D_MODEL), lambda i, e: (i, 0)),
                  pl.BlockSpec((1, D_MODEL), lambda i, e: (0, 0)),
                  pl.BlockSpec((D_MODEL, LANES), lambda i, e: (0, 0)),
                  pl.BlockSpec((1, LANES), lambda i, e: (0, 0)),
                  pl.BlockSpec((None, D_MODEL, D_EXPERT), lambda i, e: (e, 0, 0)),
                  pl.BlockSpec((None, D_MODEL, D_EXPERT), lambda i, e: (e, 0, 0)),
                  pl.BlockSpec((None, D_EXPERT, D_MODEL), lambda i, e: (e, 0, 0))],
        out_specs=pl.BlockSpec((tm, D_MODEL), lambda i, e: (i, 0)),
        out_shape=jax.ShapeDtypeStruct(x2d.shape, x2d.dtype),
        scratch_shapes=[pltpu.VMEM((tm, D_MODEL), w_gate.dtype), pltpu.VMEM((tm, LANES), f32),
                        pltpu.VMEM((tm, D_MODEL), f32)],
        compiler_params=pltpu.CompilerParams(dimension_semantics=("arbitrary", "arbitrary"),
                                             vmem_limit_bytes=VMEM_LIMIT),
        name="moe",
    )(x2d, norm_g, w_router, b_router, w_gate, w_up, w_down)


def _new_page(row, dtype):
    first = lax.broadcasted_iota(jnp.int32, (PAGE_SIZE, 1), 0) == 0
    return jnp.where(first, row, 0.0).astype(dtype)


def _nsa_sample_kernel(pt_ref, qbd_ref, gate_ref, new_ref, neww_ref, cw_ref, *rest, pg, n_pages):
    page_refs = rest[:pg]
    o_ref, ks_sc, vs_sc, cmpk_sc, cmpv_sc = rest[pg:]
    s = pl.program_id(1)
    n_tok = n_pages * PAGE_SIZE
    nb_s = n_tok // NSA_BLOCK + 1
    nbpad = cmpk_sc.shape[0]
    l_pad = ks_sc.shape[0]
    bpp = PAGE_SIZE // NSA_BLOCK
    bf16 = jnp.bfloat16

    for k in range(pg):
        page = page_refs[k][...]
        p = s * pg + k
        row0 = pl.multiple_of(p * PAGE_SIZE, PAGE_SIZE)
        ks_sc[pl.ds(row0, PAGE_SIZE), :] = page[:, 2 * LANES:3 * LANES].astype(bf16)
        vs_sc[pl.ds(row0, PAGE_SIZE), :] = page[:, 3 * LANES:4 * LANES].astype(bf16)
        cmpk_sc[pl.ds(p * bpp, bpp), :] = jnp.mean(page[:, 0:LANES].reshape(bpp, NSA_BLOCK, LANES), axis=1)
        cmpv_sc[pl.ds(p * bpp, bpp), :] = jnp.mean(page[:, LANES:2 * LANES].reshape(bpp, NSA_BLOCK, LANES), axis=1)

    @pl.when(s == pl.num_programs(1) - 1)
    def _():
        new = new_ref[...]
        ks_sc[n_tok:l_pad, :] = _new_page(new[:, 2 * LANES:3 * LANES], bf16)
        vs_sc[n_tok:l_pad, :] = _new_page(new[:, 3 * LANES:4 * LANES], bf16)
        tail = nbpad - (nb_s - 1)
        first = lax.broadcasted_iota(jnp.int32, (tail, 1), 0) == 0
        cmpk_sc[nb_s - 1:nbpad, :] = jnp.where(first, new[:, 0:LANES] * (1.0 / NSA_BLOCK), 0.0)
        cmpv_sc[nb_s - 1:nbpad, :] = jnp.where(first, new[:, LANES:2 * LANES] * (1.0 / NSA_BLOCK), 0.0)

        qbd = qbd_ref[...]
        nh = qbd.shape[0]
        qpos = n_tok
        blk = lax.broadcasted_iota(jnp.int32, (1, nbpad), 1)
        done = ((blk + 1) * NSA_BLOCK <= qpos + 1) & (blk < nb_s)
        s_c = lax.dot_general(qbd, cmpk_sc[...].astype(bf16), _NT, preferred_element_type=jnp.float32)
        p_c = _masked_softmax(s_c, done)
        o_c = jnp.dot(p_c.astype(bf16), cmpv_sc[...].astype(bf16), preferred_element_type=jnp.float32)
        imp = jnp.sum(p_c.reshape(NSA_KV_HEADS, NSA_GROUP, nbpad), axis=1)
        qpos_col = jnp.full((NSA_KV_HEADS, 1), qpos, jnp.int32)
        score = _block_importance_scores(imp, qpos_col, nbpad)
        score = jnp.where(blk < nb_s, score, -2.0)
        sel = _select_top_blocks(score, min(NSA_TOP_BLOCKS, nb_s))
        sel8 = jnp.concatenate([jnp.broadcast_to(sel[g:g + 1], (NSA_GROUP, nbpad)) for g in range(NSA_KV_HEADS)],
                               axis=0).astype(bf16)
        kpos = lax.broadcasted_iota(jnp.int32, (1, l_pad), 1)
        expand = lax.broadcasted_iota(jnp.int32, (nbpad, 1), 0) == kpos // NSA_BLOCK
        chosen = jnp.dot(sel8, jnp.where(expand, 1.0, 0.0).astype(bf16), preferred_element_type=jnp.float32) > 0.5
        s_s = lax.dot_general(qbd, ks_sc[...], _NT, preferred_element_type=jnp.float32)
        p_s = _masked_softmax(s_s, chosen & (kpos <= qpos))
        o_s = jnp.dot(p_s.astype(bf16), vs_sc[...], preferred_element_type=jnp.float32)

        cw = cw_ref[...].astype(bf16)
        nw = neww_ref[...].astype(bf16).astype(jnp.float32)
        s1 = lax.dot_general(qbd, cw[:, 0:LANES], _NT, preferred_element_type=jnp.float32)
        s2 = jnp.sum(qbd.astype(jnp.float32) * nw[:, 0:LANES], axis=-1, keepdims=True)
        m = jnp.maximum(jnp.max(s1, axis=-1, keepdims=True), s2)
        e1 = jnp.exp(s1 - m)
        e2 = jnp.exp(s2 - m)
        d = jnp.sum(e1, axis=-1, keepdims=True) + e2
        p1 = e1 / d
        p2 = (e2 / d).astype(bf16).astype(jnp.float32)
        o_w = jnp.dot(p1.astype(bf16), cw[:, LANES:2 * LANES], preferred_element_type=jnp.float32) + p2 * nw[:, LANES:2 * LANES]

        g = gate_ref[...]
        o_ref[...] = g[:, 0:1] * o_c + g[:, 1:2] * o_s + g[:, 2:3] * o_w


def _page_specs(cache, layer, pg, width):
    return [pl.BlockSpec((None, None, PAGE_SIZE, width),
                         functools.partial(lambda b, s, pt, k: (layer, pt[b, s * pg + k], 0, 0), k=k))
            for k in range(pg)]


def _seq_spec(shape):
    nd = len(shape)
    return pl.BlockSpec((None,) + tuple(shape[1:]), lambda b, s, pt: (b,) + (0,) * (nd - 1))


def _nsa_sample(page_table, qbd, gates, new_rows, new_win, cache_win_l, cache, layer, *, pg):
    bs, n_pages = page_table.shape
    n_tok = n_pages * PAGE_SIZE
    l_pad = n_tok + PAGE_SIZE
    nbpad = -(-(n_tok // NSA_BLOCK + 1) // 8) * 8
    f32, bf16 = jnp.float32, jnp.bfloat16
    fixed = (qbd, gates, new_rows, new_win, cache_win_l)
    grid_spec = pltpu.PrefetchScalarGridSpec(
        num_scalar_prefetch=1,
        grid=(bs, n_pages // pg),
        in_specs=[_seq_spec(a.shape) for a in fixed] + _page_specs(cache, layer, pg, 4 * LANES),
        out_specs=_seq_spec((bs, NSA_HEADS, LANES)),
        scratch_shapes=[pltpu.VMEM((l_pad, LANES), bf16), pltpu.VMEM((l_pad, LANES), bf16),
                        pltpu.VMEM((nbpad, LANES), f32), pltpu.VMEM((nbpad, LANES), f32)])
    return pl.pallas_call(
        functools.partial(_nsa_sample_kernel, pg=pg, n_pages=n_pages),
        grid_spec=grid_spec,
        out_shape=jax.ShapeDtypeStruct((bs, NSA_HEADS, LANES), f32),
        compiler_params=pltpu.CompilerParams(dimension_semantics=("arbitrary", "arbitrary"),
                                             vmem_limit_bytes=VMEM_LIMIT),
        name="nsa_sample",
    )(page_table, *fixed, *([cache] * pg))


def _fox_sample_kernel(pt_ref, qbd_ref, new_ref, newlf_ref, du_ref, *rest, pg, n_pages):
    page_refs = rest[:pg]
    lf_refs = rest[pg:2 * pg]
    o_ref, k_sc, v_sc, lf_sc, cum_sc, bias_sc = rest[2 * pg:]
    s = pl.program_id(1)
    n_tok = n_pages * PAGE_SIZE
    l_pad = k_sc.shape[0]
    bf16 = jnp.bfloat16
    hi = lax.Precision.HIGHEST
    w = FOX_HEADS * HEAD_DIM

    for k in range(pg):
        page = page_refs[k][...]
        p = s * pg + k
        row0 = pl.multiple_of(p * PAGE_SIZE, PAGE_SIZE)
        k_sc[pl.ds(row0, PAGE_SIZE), :] = page[:, 0:w].astype(bf16)
        v_sc[pl.ds(row0, PAGE_SIZE), :] = page[:, w:2 * w].astype(bf16)
        lf_sc[pl.ds(p, 1), :] = lf_refs[k][...]

    @pl.when(s == pl.num_programs(1) - 1)
    def _():
        new = new_ref[...]
        k_sc[n_tok:l_pad, :] = _new_page(new[:, 0:w], bf16)
        v_sc[n_tok:l_pad, :] = _new_page(new[:, w:2 * w], bf16)
        lf = lf_sc[...]
        rowi = lax.broadcasted_iota(jnp.int32, (n_pages, n_pages), 0)
        coli = lax.broadcasted_iota(jnp.int32, (n_pages, n_pages), 1)
        strict_lower = jnp.where(coli < rowi, 1.0, 0.0)
        kpos = lax.broadcasted_iota(jnp.int32, (1, l_pad), 1)
        for h in range(FOX_HEADS):
            within = jnp.dot(lf, du_ref[h], preferred_element_type=jnp.float32, precision=hi)
            before = jnp.dot(strict_lower, within, preferred_element_type=jnp.float32, precision=hi)
            cum = within + before[:, PAGE_SIZE - 1:PAGE_SIZE]
            cum_sc[...] = cum
            total = cum[n_pages - 1:n_pages, PAGE_SIZE - 1:PAGE_SIZE]
            cq = total + newlf_ref[:, h:h + 1]
            for p in range(n_pages):
                bias_sc[h:h + 1, p * PAGE_SIZE:(p + 1) * PAGE_SIZE] = cq - cum_sc[p:p + 1, :]
            bias_sc[h:h + 1, n_tok:l_pad] = jnp.zeros((1, l_pad - n_tok), jnp.float32)
        qbd = qbd_ref[...]
        sc = lax.dot_general(qbd, k_sc[...], _NT, preferred_element_type=jnp.float32) + bias_sc[...]
        pr = _masked_softmax(sc, kpos <= n_tok)
        o_ref[...] = jnp.dot(pr.astype(bf16), v_sc[...], preferred_element_type=jnp.float32)


def _fox_sample(page_table, qbd, new_rows, new_lf, cache, cache_lf, layer, *, pg):
    bs, n_pages = page_table.shape
    n_tok = n_pages * PAGE_SIZE
    l_pad = n_tok + PAGE_SIZE
    w = FOX_HEADS * HEAD_DIM
    f32, bf16 = jnp.float32, jnp.bfloat16
    lane = np.arange(PAGE_SIZE * FOX_HEADS)
    du = np.stack([((lane % FOX_HEADS == h)[:, None] & ((lane // FOX_HEADS)[:, None] <= np.arange(PAGE_SIZE)[None, :]))
                   for h in range(FOX_HEADS)]).astype(np.float32)
    du = jnp.asarray(du)
    fixed = (qbd, new_rows, new_lf)
    lf_specs = [pl.BlockSpec((None, None, 1, PAGE_SIZE * FOX_HEADS),
                             functools.partial(lambda b, s, pt, k: (layer, pt[b, s * pg + k], 0, 0), k=k))
                for k in range(pg)]
    grid_spec = pltpu.PrefetchScalarGridSpec(
        num_scalar_prefetch=1,
        grid=(bs, n_pages // pg),
        in_specs=[_seq_spec(a.shape) for a in fixed] + [pl.BlockSpec(du.shape, lambda b, s, pt: (0, 0, 0))]
        + _page_specs(cache, layer, pg, 2 * w) + lf_specs,
        out_specs=_seq_spec((bs, FOX_HEADS, w)),
        scratch_shapes=[pltpu.VMEM((l_pad, w), bf16), pltpu.VMEM((l_pad, w), bf16),
                        pltpu.VMEM((n_pages, PAGE_SIZE * FOX_HEADS), f32),
                        pltpu.VMEM((n_pages, PAGE_SIZE), f32), pltpu.VMEM((FOX_HEADS, l_pad), f32)])
    return pl.pallas_call(
        functools.partial(_fox_sample_kernel, pg=pg, n_pages=n_pages),
        grid_spec=grid_spec,
        out_shape=jax.ShapeDtypeStruct((bs, FOX_HEADS, w), f32),
        compiler_params=pltpu.CompilerParams(dimension_semantics=("arbitrary", "arbitrary"),
                                             vmem_limit_bytes=VMEM_LIMIT),
        name="fox_sample",
    )(page_table, *fixed, du, *([cache] * pg), *([cache_lf] * pg))


def _dsa_sample_kernel(pt_ref, qatt_ref, qidx_ref, iw_ref, new_ref, *rest, pg, n_pages, n_keep):
    page_refs = rest[:pg]
    o_ref, kv_sc, ik_sc = rest[pg:]
    s = pl.program_id(1)
    n_tok = n_pages * PAGE_SIZE
    l_pad = kv_sc.shape[0]
    bf16 = jnp.bfloat16
    hd = HEAD_DIM

    def split(page):
        ik = jnp.concatenate([page[:, 2 * hd:3 * hd], jnp.zeros((page.shape[0], hd), page.dtype)], axis=1)
        return page[:, 0:2 * hd].astype(bf16), ik.astype(bf16)

    for k in range(pg):
        p = s * pg + k
        row0 = pl.multiple_of(p * PAGE_SIZE, PAGE_SIZE)
        kv, ik = split(page_refs[k][...])
        kv_sc[pl.ds(row0, PAGE_SIZE), :] = kv
        ik_sc[pl.ds(row0, PAGE_SIZE), :] = ik

    @pl.when(s == pl.num_programs(1) - 1)
    def _():
        first = lax.broadcasted_iota(jnp.int32, (PAGE_SIZE, 1), 0) == 0
        kv, ik = split(jnp.where(first, new_ref[...], 0.0))
        kv_sc[n_tok:l_pad, :] = kv
        ik_sc[n_tok:l_pad, :] = ik
        kpos = lax.broadcasted_iota(jnp.int32, (1, l_pad), 1)
        causal = kpos <= n_tok
        a = lax.dot_general(qidx_ref[...], ik_sc[...], _NT, preferred_element_type=jnp.float32)
        a = jnp.maximum(a, 0.0) * iw_ref[...]
        sc = a[0:1]
        for h in range(1, DSA_IDX_HEADS):
            sc = sc + a[h:h + 1]
        key = _sortable_key(jnp.where(causal, sc, -jnp.inf))

        def count(mask):
            return jnp.sum(jnp.where(mask, 1.0, 0.0), axis=-1, keepdims=True)

        def bit_body(it, lo):
            cand = lo + lax.shift_left(jnp.int32(1), jnp.int32(31) - it)
            return jnp.where(count(key >= cand) >= n_keep, cand, lo)

        thr = lax.fori_loop(0, 32, bit_body, jnp.full((1, 1), _INT_MIN, jnp.int32))
        need = n_keep - count(key > thr)
        tie = jnp.where(key == thr, kpos, jnp.int32(2 ** 30))
        idx_bits = max(1, (l_pad - 1).bit_length())

        def idx_body(it, bound):
            step = lax.shift_left(jnp.int32(1), jnp.int32(idx_bits - 1) - it)
            return jnp.where(count(tie <= bound + step - 1) < need, bound + step, bound)

        bound = lax.fori_loop(0, idx_bits, idx_body, jnp.zeros((1, 1), jnp.int32))
        valid = ((key > thr) | (tie <= bound)) & causal
        s_att = lax.dot_general(qatt_ref[...], kv_sc[...], _NT, preferred_element_type=jnp.float32)
        pr = _masked_softmax(s_att, valid)
        o_ref[...] = jnp.dot(pr.astype(bf16), kv_sc[...], preferred_element_type=jnp.float32)


def _dsa_sample(page_table, q_att, q_idx, iw_col, new_rows, cache, layer, *, pg, n_keep):
    bs, n_pages = page_table.shape
    n_tok = n_pages * PAGE_SIZE
    l_pad = n_tok + PAGE_SIZE
    f32, bf16 = jnp.float32, jnp.bfloat16
    fixed = (q_att, q_idx, iw_col, new_rows)
    grid_spec = pltpu.PrefetchScalarGridSpec(
        num_scalar_prefetch=1,
        grid=(bs, n_pages // pg),
        in_specs=[_seq_spec(a.shape) for a in fixed] + _page_specs(cache, layer, pg, 3 * HEAD_DIM),
        out_specs=_seq_spec((bs, DSA_HEADS, LANES)),
        scratch_shapes=[pltpu.VMEM((l_pad, LANES), bf16), pltpu.VMEM((l_pad, LANES), bf16)])
    return pl.pallas_call(
        functools.partial(_dsa_sample_kernel, pg=pg, n_pages=n_pages, n_keep=n_keep),
        grid_spec=grid_spec,
        out_shape=jax.ShapeDtypeStruct((bs, DSA_HEADS, LANES), f32),
        compiler_params=pltpu.CompilerParams(dimension_semantics=("arbitrary", "arbitrary"),
                                             vmem_limit_bytes=VMEM_LIMIT),
        name="dsa_sample",
    )(page_table, *fixed, *([cache] * pg))


def _token0_page(col, dtype):
    first = lax.broadcasted_iota(jnp.int32, (1, PAGE_SIZE), 1) == 0
    return jnp.where(first, col, 0.0).astype(dtype)


def _hi_lo(x):
    hi = x.astype(jnp.bfloat16)
    return hi, (x - hi.astype(jnp.float32)).astype(jnp.bfloat16)


def _scores(q, pages_sc, n):
    return jnp.concatenate([jnp.dot(q, pages_sc[p], preferred_element_type=jnp.float32) for p in range(n)], axis=1)


def _weighted_values(p, pages_sc, n):
    p = p.astype(jnp.bfloat16)
    out = None
    for k in range(n):
        term = lax.dot_general(p[:, k * PAGE_SIZE:(k + 1) * PAGE_SIZE], pages_sc[k], _NT,
                               preferred_element_type=jnp.float32)
        out = term if out is None else out + term
    return out


def _split_dot_nt(a, b_bf16):
    hi = a.astype(jnp.bfloat16)
    lo = (a - hi.astype(jnp.float32)).astype(jnp.bfloat16)
    return (lax.dot_general(hi, b_bf16, _NT, preferred_element_type=jnp.float32)
            + lax.dot_general(lo, b_bf16, _NT, preferred_element_type=jnp.float32))


def _nsa_decode_kernel(pt_ref, qbd_ref, gate_ref, new_ref, neww_ref, cw_ref, *rest, pg, n_pages):
    page_refs = rest[:pg]
    o_ref, kc_sc, kcl_sc, vc_sc, ks_sc, vs_sc = rest[pg:]
    s = pl.program_id(1)
    n_tok = n_pages * PAGE_SIZE
    l_pad = n_tok + PAGE_SIZE
    nb_s = n_tok // NSA_BLOCK + 1
    nbpad = -(-nb_s // 8) * 8
    bf16 = jnp.bfloat16
    nh = NSA_HEADS

    for k in range(pg):
        page = page_refs[k][...]
        p = s * pg + k
        kc_sc[p], kcl_sc[p] = _hi_lo(page[0 * LANES:1 * LANES])
        vc_sc[p] = page[1 * LANES:2 * LANES].astype(bf16)
        ks_sc[p] = page[2 * LANES:3 * LANES].astype(bf16)
        vs_sc[p] = page[3 * LANES:4 * LANES].astype(bf16)

    @pl.when(s == pl.num_programs(1) - 1)
    def _():
        new = new_ref[...]
        kc_sc[n_pages], kcl_sc[n_pages] = _hi_lo(_token0_page(new[0 * LANES:1 * LANES], jnp.float32))
        vc_sc[n_pages] = _token0_page(new[1 * LANES:2 * LANES], bf16)
        ks_sc[n_pages] = _token0_page(new[2 * LANES:3 * LANES], bf16)
        vs_sc[n_pages] = _token0_page(new[3 * LANES:4 * LANES], bf16)
        n_all = n_pages + 1
        q_hl = qbd_ref[...]
        qbd = q_hl[0:nh]
        qpos = n_tok
        kpos = lax.broadcasted_iota(jnp.int32, (1, l_pad), 1)
        blk = lax.broadcasted_iota(jnp.int32, (1, nbpad), 1)
        incid = jnp.where(lax.broadcasted_iota(jnp.int32, (nbpad, 1), 0) == kpos // NSA_BLOCK, 1.0, 0.0).astype(bf16)
        inv = 1.0 / NSA_BLOCK

        s_hl = _scores(q_hl, kc_sc, n_all)
        s_tok = s_hl[0:nh] + s_hl[nh:2 * nh] + _scores(qbd, kcl_sc, n_all)
        s_c = _split_dot_nt(s_tok, incid) * inv
        done = ((blk + 1) * NSA_BLOCK <= qpos + 1) & (blk < nb_s)
        p_c = _masked_softmax(s_c, done)
        p_tok = jnp.dot(p_c.astype(bf16), incid, preferred_element_type=jnp.float32) * inv
        o_c = _weighted_values(p_tok, vc_sc, n_all)

        imp = jnp.sum(p_c.reshape(NSA_KV_HEADS, NSA_GROUP, nbpad), axis=1)
        score = _block_importance_scores(imp, jnp.full((NSA_KV_HEADS, 1), qpos, jnp.int32), nbpad)
        score = jnp.where(blk < nb_s, score, -2.0)
        sel = _select_top_blocks(score, min(NSA_TOP_BLOCKS, nb_s))
        sel8 = jnp.concatenate([jnp.broadcast_to(sel[g:g + 1], (NSA_GROUP, nbpad)) for g in range(NSA_KV_HEADS)],
                               axis=0).astype(bf16)
        chosen = jnp.dot(sel8, incid, preferred_element_type=jnp.float32) > 0.5
        p_s = _masked_softmax(_scores(qbd, ks_sc, n_all), chosen & (kpos <= qpos))
        o_s = _weighted_values(p_s, vs_sc, n_all)

        cw = cw_ref[...].astype(bf16)
        nw = neww_ref[...]
        win_keep = cw.shape[1]
        s_w = jnp.concatenate([jnp.dot(qbd, cw[0:LANES], preferred_element_type=jnp.float32),
                               jnp.dot(qbd, _token0_page(nw[0:LANES], bf16), preferred_element_type=jnp.float32)],
                              axis=1)
        wpos = lax.broadcasted_iota(jnp.int32, (1, win_keep + PAGE_SIZE), 1)
        p_w = _masked_softmax(s_w, wpos <= win_keep).astype(bf16)
        o_w = (lax.dot_general(p_w[:, :win_keep], cw[LANES:2 * LANES], _NT, preferred_element_type=jnp.float32)
               + lax.dot_general(p_w[:, win_keep:], _token0_page(nw[LANES:2 * LANES], bf16), _NT,
                                 preferred_element_type=jnp.float32))

        g = gate_ref[...]
        o_ref[...] = g[:, 0:1] * o_c + g[:, 1:2] * o_s + g[:, 2:3] * o_w


def _page_specs_t(layer, pg, rows):
    return [pl.BlockSpec((None, None, rows, PAGE_SIZE),
                         functools.partial(lambda b, s, pt, k: (layer, pt[b, s * pg + k], 0, 0), k=k))
            for k in range(pg)]


def _decode_call(kernel_fn, name, page_table, fixed, const, caches, layer, out_tail, scratch, *, pg):
    bs, n_pages = page_table.shape
    const_specs = [pl.BlockSpec(a.shape, functools.partial(lambda b, s, pt, nd: (0,) * nd, nd=a.ndim)) for a in const]
    page_specs = []
    for cache in caches:
        page_specs += _page_specs_t(layer, pg, cache.shape[2])
    grid_spec = pltpu.PrefetchScalarGridSpec(
        num_scalar_prefetch=1,
        grid=(bs, n_pages // pg),
        in_specs=[_seq_spec(a.shape) for a in fixed] + const_specs + page_specs,
        out_specs=_seq_spec((bs,) + out_tail),
        scratch_shapes=scratch)
    operands = list(fixed) + list(const)
    for cache in caches:
        operands += [cache] * pg
    return pl.pallas_call(
        kernel_fn,
        grid_spec=grid_spec,
        out_shape=jax.ShapeDtypeStruct((bs,) + out_tail, jnp.float32),
        compiler_params=pltpu.CompilerParams(dimension_semantics=("arbitrary", "arbitrary"),
                                             vmem_limit_bytes=VMEM_LIMIT),
        name=name,
    )(page_table, *operands)


def _nsa_decode(page_table, qbd, gates, new_t, neww_t, cache_win_t, cache_t, layer, *, pg):
    n_pages = page_table.shape[1]
    buf = pltpu.VMEM((n_pages + 1, LANES, PAGE_SIZE), jnp.bfloat16)
    return _decode_call(functools.partial(_nsa_decode_kernel, pg=pg, n_pages=n_pages), "nsa_sample", page_table,
                        (qbd, gates, new_t, neww_t, cache_win_t), (), (cache_t,), layer, (NSA_HEADS, LANES),
                        [buf, buf, buf, buf, buf], pg=pg)


def _fox_decode_kernel(pt_ref, qbd_ref, new_ref, newlf_ref, *rest, pg, n_pages):
    page_refs = rest[:pg]
    lf_refs = rest[pg:2 * pg]
    o_ref, k_sc, v_sc, lf_sc, cum_sc = rest[2 * pg:]
    s = pl.program_id(1)
    n_tok = n_pages * PAGE_SIZE
    l_pad = n_tok + PAGE_SIZE
    bf16 = jnp.bfloat16
    hi = lax.Precision.HIGHEST
    w = FOX_HEADS * HEAD_DIM

    for k in range(pg):
        page = page_refs[k][...]
        p = s * pg + k
        k_sc[p] = page[0:w].astype(bf16)
        v_sc[p] = page[w:2 * w].astype(bf16)
        lf = lf_refs[k][...]
        for h in range(FOX_HEADS):
            lf_sc[h, pl.ds(p, 1), :] = lf[h:h + 1, :]

    @pl.when(s == pl.num_programs(1) - 1)
    def _():
        new = new_ref[...]
        k_sc[n_pages] = _token0_page(new[0:w], bf16)
        v_sc[n_pages] = _token0_page(new[w:2 * w], bf16)
        n_all = n_pages + 1
        iota2 = lambda n, axis: lax.broadcasted_iota(jnp.int32, (n, n), axis)
        upper_incl = jnp.where(iota2(PAGE_SIZE, 0) <= iota2(PAGE_SIZE, 1), 1.0, 0.0)
        lower_strict = jnp.where(iota2(n_pages, 1) < iota2(n_pages, 0), 1.0, 0.0)
        cqs = []
        for h in range(FOX_HEADS):
            within = jnp.dot(lf_sc[h], upper_incl, preferred_element_type=jnp.float32, precision=hi)
            before = jnp.dot(lower_strict, within, preferred_element_type=jnp.float32, precision=hi)
            cum = within + before[:, PAGE_SIZE - 1:PAGE_SIZE]
            cum_sc[h] = cum
            cqs.append(cum[n_pages - 1:n_pages, PAGE_SIZE - 1:PAGE_SIZE] + newlf_ref[:, h:h + 1])
        cq = jnp.concatenate(cqs, axis=0)
        qbd = qbd_ref[...]
        parts = []
        for p in range(n_pages):
            ck = jnp.concatenate([cum_sc[h, p:p + 1, :] for h in range(FOX_HEADS)], axis=0)
            parts.append(jnp.dot(qbd, k_sc[p], preferred_element_type=jnp.float32) + (cq - ck))
        parts.append(jnp.dot(qbd, k_sc[n_pages], preferred_element_type=jnp.float32))
        sc = jnp.concatenate(parts, axis=1)
        kpos = lax.broadcasted_iota(jnp.int32, (1, l_pad), 1)
        o_ref[...] = _weighted_values(_masked_softmax(sc, kpos <= n_tok), v_sc, n_all)


def _fox_decode(page_table, qbd, new_t, new_lf, cache_t, cache_lf_t, layer, *, pg):
    n_pages = page_table.shape[1]
    w = FOX_HEADS * HEAD_DIM
    buf = pltpu.VMEM((n_pages + 1, w, PAGE_SIZE), jnp.bfloat16)
    lfbuf = pltpu.VMEM((FOX_HEADS, n_pages, PAGE_SIZE), jnp.float32)
    return _decode_call(functools.partial(_fox_decode_kernel, pg=pg, n_pages=n_pages), "fox_sample", page_table,
                        (qbd, new_t, new_lf), (), (cache_t, cache_lf_t), layer, (FOX_HEADS, w),
                        [buf, buf, lfbuf, lfbuf], pg=pg)


def _dsa_decode_kernel(pt_ref, qatt_ref, qidx_ref, iw_ref, new_ref, *rest, pg, n_pages, n_keep):
    page_refs = rest[:pg]
    o_ref, kv_sc, ik_sc, ikl_sc = rest[pg:]
    s = pl.program_id(1)
    n_tok = n_pages * PAGE_SIZE
    l_pad = n_tok + PAGE_SIZE
    bf16 = jnp.bfloat16
    hd = HEAD_DIM
    nh = DSA_IDX_HEADS

    for k in range(pg):
        page = page_refs[k][...]
        p = s * pg + k
        kv_sc[p] = page[0:2 * hd].astype(bf16)
        ik_sc[p], ikl_sc[p] = _hi_lo(page[2 * hd:3 * hd])

    @pl.when(s == pl.num_programs(1) - 1)
    def _():
        new = new_ref[...]
        kv_sc[n_pages] = _token0_page(new[0:2 * hd], bf16)
        ik_sc[n_pages], ikl_sc[n_pages] = _hi_lo(_token0_page(new[2 * hd:3 * hd], jnp.float32))
        n_all = n_pages + 1
        kpos = lax.broadcasted_iota(jnp.int32, (1, l_pad), 1)
        causal = kpos <= n_tok
        q_hl = qidx_ref[...]
        a_hl = _scores(q_hl, ik_sc, n_all)
        a = a_hl[0:nh] + a_hl[nh:2 * nh] + _scores(q_hl[0:nh], ikl_sc, n_all)
        a = jnp.maximum(a, 0.0) * iw_ref[...]
        sc = a[0:1]
        for h in range(1, DSA_IDX_HEADS):
            sc = sc + a[h:h + 1]
        key = _sortable_key(jnp.where(causal, sc, -jnp.inf))

        def count(mask):
            return jnp.sum(jnp.where(mask, 1.0, 0.0), axis=-1, keepdims=True)

        def bit_body(it, lo):
            cand = lo + lax.shift_left(jnp.int32(1), jnp.int32(31) - it)
            return jnp.where(count(key >= cand) >= n_keep, cand, lo)

        thr = lax.fori_loop(0, 32, bit_body, jnp.full((1, 1), _INT_MIN, jnp.int32))
        need = n_keep - count(key > thr)
        tie = jnp.where(key == thr, kpos, jnp.int32(2 ** 30))
        idx_bits = max(1, (l_pad - 1).bit_length())

        def idx_body(it, bound):
            step = lax.shift_left(jnp.int32(1), jnp.int32(idx_bits - 1) - it)
            return jnp.where(count(tie <= bound + step - 1) < need, bound + step, bound)

        bound = lax.fori_loop(0, idx_bits, idx_body, jnp.zeros((1, 1), jnp.int32))
        valid = ((key > thr) | (tie <= bound)) & causal
        pr = _masked_softmax(_scores(qatt_ref[...], kv_sc, n_all), valid)
        o_ref[...] = _weighted_values(pr, kv_sc, n_all)


def _dsa_decode(page_table, q_att, q_idx, iw_col, new_t, cache_t, layer, *, pg, n_keep):
    n_pages = page_table.shape[1]
    hd = HEAD_DIM
    return _decode_call(functools.partial(_dsa_decode_kernel, pg=pg, n_pages=n_pages, n_keep=n_keep), "dsa_sample",
                        page_table, (q_att, q_idx, iw_col, new_t), (), (cache_t,), layer, (DSA_HEADS, LANES),
                        [pltpu.VMEM((n_pages + 1, 2 * hd, PAGE_SIZE), jnp.bfloat16),
                         pltpu.VMEM((n_pages + 1, hd, PAGE_SIZE), jnp.bfloat16),
                         pltpu.VMEM((n_pages + 1, hd, PAGE_SIZE), jnp.bfloat16)], pg=pg)


def _merge_weights(w_bn, w_bf, w_bd, w_o):
    hd = HEAD_DIM
    rows = []
    for c in range(NSA_GROUP):
        rows += list(range(hd * c, hd * c + hd)) + list(range(hd * (c + NSA_GROUP), hd * (c + NSA_GROUP) + hd))
    w_bn_p = jnp.take(w_bn, np.asarray(rows, np.int32), axis=0)
    w_bd_p = jnp.pad(w_bd.reshape(DSA_HEADS, hd, D_MODEL), ((0, 0), (0, hd), (0, 0)))
    w_bd_p = w_bd_p.reshape(DSA_HEADS * LANES, D_MODEL)
    return w_bn_p, w_bf, w_bd_p, w_o


def _router_weights(w_rg, b_rg, w_re, b_re):
    pad = LANES - N_EXPERTS - N_GROUPS
    w = jnp.concatenate([w_re, w_rg, jnp.zeros((D_MODEL, pad), w_re.dtype)], axis=1).astype(jnp.float32)
    b = jnp.concatenate([b_re, b_rg, jnp.zeros((pad,), b_re.dtype)]).astype(jnp.float32)[None, :]
    return w, b


TM_PROJ = 512
TQ_FOX = 512
TQ_NSA = 256
TQ_DSA = 256
TK_ATTN = 1024
TM_MERGE = 512
TM_MOE = 1024
PAGES_PER_STEP = 64


def _hi_lo_rows(q):
    hi = q.astype(jnp.bfloat16)
    lo = (q - hi.astype(jnp.float32)).astype(jnp.bfloat16)
    return jnp.concatenate([hi, lo], axis=1)


def _dsa_rows(dki, misc):
    return jnp.concatenate([dki[:, :HEAD_DIM], misc[:, :HEAD_DIM], dki[:, HEAD_DIM:]], axis=-1)


def kernel(x_prompt, x_sample, cache_nsa, cache_fox, cache_fox_logf, cache_dsa, cache_win, page_table, norm_attn_g, w_in, nsa_qk_g, fox_qk_g, fox_b_f, dsa_qk_g, w_branch_nsa, w_branch_fox, w_branch_dsa, w_out, norm_ffn_g, w_router_group, b_router_group, w_router_expert, b_router_expert, w_exp_gate, w_exp_up, w_exp_down):
    depth = w_in.shape[0]
    B, S, D = x_prompt.shape
    Bs, T, _ = x_sample.shape
    assert T == 1 and D == D_MODEL
    n_pages = page_table.shape[1]
    past_len = n_pages * PAGE_SIZE
    win_keep = cache_win.shape[2]
    n_phys = cache_nsa.shape[1]
    bf16 = jnp.bfloat16
    hd = HEAD_DIM

    tm_proj = min(TM_PROJ, S)
    cos_p, sin_p = _rope_tables(jnp.arange(S))
    cos_s, sin_s = _rope_tables(jnp.full((Bs,), past_len))
    pg = min(PAGES_PER_STEP, n_pages)
    c_nsa = jnp.moveaxis(cache_nsa.reshape(depth, n_phys, PAGE_SIZE, 4 * LANES), 2, 3)
    c_fox = jnp.moveaxis(cache_fox.reshape(depth, n_phys, PAGE_SIZE, 2 * FOX_HEADS * hd), 2, 3)
    c_lf = jnp.moveaxis(cache_fox_logf.astype(jnp.float32), 2, 3)
    c_dsa = jnp.moveaxis(cache_dsa.reshape(depth, n_phys, PAGE_SIZE, 3 * hd), 2, 3)
    c_win = jnp.moveaxis(cache_win.reshape(depth, Bs, win_keep, 2 * LANES), 2, 3)
    lane = jnp.arange(LANES)
    left = lane < hd

    xp = x_prompt.reshape(B * S, D)
    xs = x_sample.reshape(Bs, D)
    outs = [[] for _ in range(10)]
    for l in range(depth):
        w_perm, w_mg, gain_row, bf_row = _layer_tables(w_in[l], nsa_qk_g[l], fox_qk_g[l], fox_b_f[l], dsa_qk_g[l])
        w_m = (w_mg,) + _merge_weights(w_branch_nsa[l], w_branch_fox[l], w_branch_dsa[l], w_out[l])
        w_m_bf = tuple(w.astype(bf16) for w in w_m)
        w_r, b_r = _router_weights(w_router_group[l], b_router_group[l], w_router_expert[l], b_router_expert[l])
        w_e = (w_exp_gate[l].astype(bf16), w_exp_up[l].astype(bf16), w_exp_down[l].astype(bf16))
        g_attn = norm_attn_g[l][None, :]
        g_ffn = norm_ffn_g[l][None, :]

        (qn, nsa_rows, win_rows, slc_kv, win_kv, cmp, qf, fox_rows, fox_kv, dqi, dki, dki_bf, misc, misc_bf,
         cum) = _project(xp, g_attn, w_perm.astype(bf16), gain_row, cos_p, sin_p, bf_row,
                         tm=tm_proj, tiles_per_batch=S // tm_proj, with_blocks=True)
        cum_t = cum[:, _M_FF:_M_IW].reshape(B, S, FOX_HEADS).transpose(0, 2, 1)
        o_fox = _fox_prompt(qf, fox_kv, cum, cum_t, batch=B, seq=S, tq=min(TQ_FOX, S), tk=min(TK_ATTN, S))
        o_nsa = _nsa_prompt(qn, misc, cmp.astype(bf16), slc_kv, win_kv, batch=B, seq=S, tq=min(TQ_NSA, S),
                            tk=min(TK_ATTN, S))
        o_dsa = _dsa_prompt(dqi, misc, dki_bf, misc_bf, batch=B, seq=S, tq=min(TQ_DSA, S), tk=min(TK_ATTN, S),
                            n_keep=min(DSA_TOPK, S // 4))
        xp = _merge(xp, g_attn, o_nsa, o_fox, o_dsa, *w_m_bf, tm=min(TM_MERGE, B * S))
        xp = _moe(xp, g_ffn, w_r, b_r, *w_e, tm=min(TM_MOE, B * S))
        outs[0].append(nsa_rows.reshape(B, S, 4, NSA_KV_HEADS, hd))
        outs[2].append(fox_rows.reshape(B, S, 2, FOX_HEADS, hd))
        outs[4].append(misc[:, _M_FF:_M_IW].reshape(B, S, FOX_HEADS))
        outs[6].append(_dsa_rows(dki, misc).reshape(B, S, 3, hd))
        outs[8].append(win_rows.reshape(B, S, 2, NSA_KV_HEADS, hd)[:, S - min(NSA_WINDOW, S):])

        (qn, nsa_rows, win_rows, _, _, _, qf, fox_rows, _, dqi, dki, _, misc, _, _) = _project(
            xs, g_attn, w_perm, gain_row, cos_s, sin_s, bf_row, tm=Bs, tiles_per_batch=1, with_blocks=False)
        chunks = qn.reshape(Bs, NSA_GROUP, LANES)
        qbd = _hi_lo_rows(jnp.concatenate([jnp.where(left, chunks, 0), jnp.where(left, 0, chunks)], axis=1))
        gates = misc[:, _M_GATE:_M_FF].reshape(Bs, 3, NSA_HEADS).transpose(0, 2, 1)
        gates = jnp.pad(gates, ((0, 0), (0, 0), (0, LANES - 3)))
        o = _nsa_decode(page_table, qbd, gates, nsa_rows[:, :, None], win_rows[:, :, None], c_win[l], c_nsa, l, pg=pg)
        o_nsa = jnp.where(left, o[:, :NSA_GROUP], o[:, NSA_GROUP:]).reshape(Bs, 4 * LANES)

        head_of_lane = jnp.arange(FOX_HEADS * hd) // hd
        qbd_f = jnp.where(head_of_lane[None, None, :] == jnp.arange(FOX_HEADS)[None, :, None], qf[:, None, :], 0)
        qbd_f = qbd_f.astype(bf16)
        o = _fox_decode(page_table, qbd_f, fox_rows[:, :, None], misc[:, None, _M_FF:_M_IW], c_fox, c_lf, l, pg=pg)
        o_fox = jnp.einsum('bhhd->bhd', o.reshape(Bs, FOX_HEADS, FOX_HEADS, hd)).reshape(Bs, FOX_HEADS * hd)

        chunks = dqi.reshape(Bs, DSA_HEADS, LANES)
        q_att = jnp.where(left, chunks, 0).astype(bf16)
        q_idx = _hi_lo_rows(chunks[..., hd:])
        dsa_new = _dsa_rows(dki, misc)
        o = _dsa_decode(page_table, q_att, q_idx, misc[:, _M_IW:_M_END, None], dsa_new[:, :, None], c_dsa, l,
                        pg=pg, n_keep=min(DSA_TOPK, (past_len + 1) // 4))
        o_dsa = jnp.concatenate([o[..., hd:], jnp.zeros_like(o[..., hd:])], axis=-1).reshape(Bs, 4 * LANES)

        xs = _merge(xs, g_attn, o_nsa, o_fox, o_dsa, *w_m, tm=Bs)
        xs = _moe(xs, g_ffn, w_r, b_r, w_exp_gate[l], w_exp_up[l], w_exp_down[l], tm=Bs)
        outs[1].append(nsa_rows.reshape(Bs, 1, 4, NSA_KV_HEADS, hd))
        outs[3].append(fox_rows.reshape(Bs, 1, 2, FOX_HEADS, hd))
        outs[5].append(misc[:, _M_FF:_M_IW].reshape(Bs, 1, FOX_HEADS))
        outs[7].append(dsa_new.reshape(Bs, 1, 3, hd))
        win_all = jnp.concatenate([cache_win[l], win_rows.reshape(Bs, 1, 2, NSA_KV_HEADS, hd)], axis=1)
        outs[9].append(win_all[:, 1:])
    return (xp.reshape(B, S, D), xs.reshape(Bs, 1, D)) + tuple(jnp.stack(o) for o in outs)
```

```python
import functools

import numpy as np
import jax
import jax.numpy as jnp
from jax import lax
from jax.experimental import pallas as pl
from jax.experimental.pallas import tpu as pltpu

D_MODEL = 1024
HEAD_DIM = 64
HALF = HEAD_DIM // 2
NSA_HEADS = 8
NSA_KV_HEADS = 2
NSA_GROUP = NSA_HEADS // NSA_KV_HEADS
NSA_BLOCK = 64
NSA_TOP_BLOCKS = 16
NSA_WINDOW = 512
NSA_FORCE_BONUS = 8.0
FOX_HEADS = 4
DSA_HEADS = 4
DSA_IDX_HEADS = 4
DSA_TOPK = 256
N_GROUPS = 4
EXPERTS_PER_GROUP = 4
N_EXPERTS = N_GROUPS * EXPERTS_PER_GROUP
D_EXPERT = 512
ROPE_THETA = 10000.0
NORM_EPS = 1e-6
NEG_BIG = -1e30
PAGE_SIZE = 128
QK_SCALE = HEAD_DIM ** -0.5

LANES = 128
VMEM_LIMIT = 56 * 1024 * 1024

_O_NQ = 0
_O_NKV = _O_NQ + NSA_HEADS * HEAD_DIM
_O_NGATE = _O_NKV + 6 * NSA_KV_HEADS * HEAD_DIM
_O_FQKV = _O_NGATE + 3 * NSA_HEADS
_O_FF = _O_FQKV + 3 * FOX_HEADS * HEAD_DIM
_O_DQ = _O_FF + FOX_HEADS
_O_DKV = _O_DQ + DSA_HEADS * HEAD_DIM
_O_DIQ = _O_DKV + 2 * HEAD_DIM
_O_DIK = _O_DIQ + DSA_IDX_HEADS * HEAD_DIM
_O_DIW = _O_DIK + HEAD_DIM
_O_MG = _O_DIW + DSA_IDX_HEADS
D_IN = _O_MG + 3 * D_MODEL

_M_GATE = HEAD_DIM
_M_FF = _M_GATE + 3 * NSA_HEADS
_M_IW = _M_FF + FOX_HEADS
_M_END = _M_IW + DSA_IDX_HEADS

_C_NQ = 0
_C_NKV = 4
_C_FQ = 10
_C_FK = 12
_C_FV = 14
_C_DQI = 16
_C_DKI = 20
_C_MISC = 21
N_CHUNKS = 22
D_PROJ = N_CHUNKS * LANES


def _proj_column_permutation():
    idx = []
    for c in range(4):
        idx += list(range(_O_NQ + 64 * c, _O_NQ + 64 * c + 64))
        idx += list(range(_O_NQ + 64 * (c + 4), _O_NQ + 64 * (c + 4) + 64))
    idx += list(range(_O_NKV, _O_NKV + 768))
    idx += list(range(_O_FQKV, _O_FQKV + 768))
    for h in range(4):
        idx += list(range(_O_DQ + 64 * h, _O_DQ + 64 * h + 64))
        idx += list(range(_O_DIQ + 64 * h, _O_DIQ + 64 * h + 64))
    idx += list(range(_O_DKV, _O_DKV + 64)) + list(range(_O_DIK, _O_DIK + 64))
    idx += list(range(_O_DKV + 64, _O_DKV + 128))
    idx += list(range(_O_NGATE, _O_NGATE + 24)) + list(range(_O_FF, _O_FF + 4)) + list(range(_O_DIW, _O_DIW + 4))
    idx += [D_IN] * (LANES - _M_END)
    assert len(idx) == D_PROJ
    return np.asarray(idx, np.int32)


def _rope_tables(pos):
    inv = ROPE_THETA ** (-jnp.arange(HALF, dtype=jnp.float32) * 2.0 / HEAD_DIM)
    ang = pos.astype(jnp.float32)[:, None] * inv[None, :]
    cos, sin = jnp.cos(ang), jnp.sin(ang)
    cos = jnp.concatenate([cos, cos, cos, cos], axis=-1)
    sin = jnp.concatenate([-sin, sin, -sin, sin], axis=-1)
    return cos, sin


def _rms_rows(x, g):
    return x * lax.rsqrt(jnp.mean(jnp.square(x), axis=-1, keepdims=True) + NORM_EPS) * g


def _split_dot(a, b_bf16):
    hi = a.astype(jnp.bfloat16)
    lo = (a - hi.astype(jnp.float32)).astype(jnp.bfloat16)
    return (jnp.dot(hi, b_bf16, preferred_element_type=jnp.float32)
            + jnp.dot(lo, b_bf16, preferred_element_type=jnp.float32))


def _proj_kernel(x_ref, g_ref, w_ref, gain_ref, cos_ref, sin_ref, bd_ref, bf_ref, tri_ref,
                 qn_ref, nsa_rows_ref, win_rows_ref, slc_kv_ref, win_kv_ref, cmp_ref,
                 qf_ref, fox_rows_ref, fox_kv_ref, dqi_ref, dki_ref, dki_bf_ref,
                 misc_ref, misc_bf_ref, cum_ref, carry_ref, *, tiles_per_batch, with_blocks):
    i = pl.program_id(0)
    precise = w_ref.dtype == jnp.float32
    xn = _rms_rows(x_ref[...], g_ref[...])
    if not precise:
        xn = xn.astype(jnp.bfloat16)
    lane = lax.broadcasted_iota(jnp.int32, (1, LANES), 1)
    left = lane < HEAD_DIM
    first_half = (lane % HEAD_DIM) < HALF
    cos = cos_ref[...]
    sin = sin_ref[...]
    bd = bd_ref[...]

    def chunk(c):
        return jnp.dot(xn, w_ref[:, c * LANES:(c + 1) * LANES], preferred_element_type=jnp.float32,
                       precision=lax.Precision.HIGHEST if precise else None)

    def head_norm(h, c, only_left=False):
        ms = _split_dot(h * h, bd)
        y = h * lax.rsqrt(ms + NORM_EPS) * gain_ref[:, c * LANES:(c + 1) * LANES]
        return jnp.where(left, y, h) if only_left else y

    def rope(h):
        swapped = jnp.where(first_half, pltpu.roll(h, LANES - HALF, 1), pltpu.roll(h, HALF, 1))
        return h * cos + swapped * sin

    for c in range(4):
        q = rope(head_norm(chunk(_C_NQ + c), _C_NQ + c)) * QK_SCALE
        qn_ref[:, c * LANES:(c + 1) * LANES] = q.astype(qn_ref.dtype)

    kc = rope(head_norm(chunk(_C_NKV + 0), _C_NKV + 0))
    vc = chunk(_C_NKV + 1)
    ks = rope(head_norm(chunk(_C_NKV + 2), _C_NKV + 2))
    vs = chunk(_C_NKV + 3)
    kw = rope(head_norm(chunk(_C_NKV + 4), _C_NKV + 4))
    vw = chunk(_C_NKV + 5)
    nsa_rows_ref[:, 0 * LANES:1 * LANES] = kc
    nsa_rows_ref[:, 1 * LANES:2 * LANES] = vc
    nsa_rows_ref[:, 2 * LANES:3 * LANES] = ks
    nsa_rows_ref[:, 3 * LANES:4 * LANES] = vs
    win_rows_ref[:, 0:LANES] = kw
    win_rows_ref[:, LANES:2 * LANES] = vw
    slc_kv_ref[:, 0:LANES] = ks.astype(jnp.bfloat16)
    slc_kv_ref[:, LANES:2 * LANES] = vs.astype(jnp.bfloat16)
    win_kv_ref[:, 0:LANES] = kw.astype(jnp.bfloat16)
    win_kv_ref[:, LANES:2 * LANES] = vw.astype(jnp.bfloat16)
    if with_blocks:
        tm = kc.shape[0]
        nblk = tm // NSA_BLOCK
        cmp_ref[:, 0:LANES] = jnp.mean(kc.reshape(nblk, NSA_BLOCK, LANES), axis=1)
        cmp_ref[:, LANES:2 * LANES] = jnp.mean(vc.reshape(nblk, NSA_BLOCK, LANES), axis=1)
    else:
        cmp_ref[...] = jnp.zeros_like(cmp_ref)

    for c in range(2):
        qf = head_norm(chunk(_C_FQ + c), _C_FQ + c) * QK_SCALE
        qf_ref[:, c * LANES:(c + 1) * LANES] = qf.astype(qf_ref.dtype)
        fk = head_norm(chunk(_C_FK + c), _C_FK + c)
        fv = chunk(_C_FV + c)
        fox_rows_ref[:, c * LANES:(c + 1) * LANES] = fk
        fox_rows_ref[:, (2 + c) * LANES:(3 + c) * LANES] = fv
        fox_kv_ref[:, c * LANES:(c + 1) * LANES] = fk.astype(jnp.bfloat16)
        fox_kv_ref[:, (2 + c) * LANES:(3 + c) * LANES] = fv.astype(jnp.bfloat16)

    for c in range(4):
        dqi = rope(head_norm(chunk(_C_DQI + c), _C_DQI + c, only_left=True)) * QK_SCALE
        dqi_ref[:, c * LANES:(c + 1) * LANES] = dqi.astype(dqi_ref.dtype)

    dki = rope(head_norm(chunk(_C_DKI), _C_DKI, only_left=True))
    dki_ref[...] = dki
    dki_bf_ref[...] = dki.astype(jnp.bfloat16)

    m = chunk(_C_MISC)
    zf = m + bf_ref[...]
    logf = jnp.minimum(zf, 0.0) - jnp.log(1.0 + jnp.exp(-jnp.abs(zf)))
    is_gate = (lane >= _M_GATE) & (lane < _M_FF)
    is_ff = (lane >= _M_FF) & (lane < _M_IW)
    is_iw = (lane >= _M_IW) & (lane < _M_END)
    out = jnp.where(is_gate, jax.nn.sigmoid(m), m)
    out = jnp.where(is_ff, logf, out)
    out = jnp.where(is_iw, m * (DSA_IDX_HEADS ** -0.5), out)
    misc_ref[...] = out
    misc_bf_ref[...] = out.astype(jnp.bfloat16)

    @pl.when(i % tiles_per_batch == 0)
    def _():
        carry_ref[...] = jnp.zeros_like(carry_ref)

    lf = jnp.where(is_ff, logf, 0.0)
    cum = jnp.dot(tri_ref[...], lf, preferred_element_type=jnp.float32,
                  precision=lax.Precision.HIGHEST) + carry_ref[...]
    cum_ref[...] = cum
    carry_ref[...] = cum[cum.shape[0] - 1:, :]


def _project(x2d, norm_g, w_perm, gain_row, cos, sin, bf_row, *, tm, tiles_per_batch, with_blocks):
    n = x2d.shape[0]
    nt = n // tm
    ncmp = max(tm // NSA_BLOCK, 8) if not with_blocks else tm // NSA_BLOCK
    bd = jnp.asarray(np.kron(np.eye(2), np.full((HEAD_DIM, HEAD_DIM), 1.0 / HEAD_DIM)), jnp.bfloat16)
    tri = jnp.asarray(np.tril(np.ones((tm, tm))), jnp.float32)
    f32, bf16 = jnp.float32, jnp.bfloat16
    row = lambda w: pl.BlockSpec((tm, w), lambda i: (i, 0))
    full = lambda a: pl.BlockSpec(a.shape, lambda i: (0,) * a.ndim)
    pos_spec = pl.BlockSpec((tm, LANES), lambda i: (i % tiles_per_batch, 0))
    qdt = w_perm.dtype
    outs = [
        (512, qdt),
        (512, f32),
        (256, f32),
        (256, bf16),
        (256, bf16),
        None,
        (256, qdt),
        (512, f32),
        (512, bf16),
        (512, qdt),
        (128, f32),
        (128, bf16),
        (128, f32),
        (128, bf16),
        (128, f32),
    ]
    out_shape, out_specs = [], []
    for o in outs:
        if o is None:
            out_shape.append(jax.ShapeDtypeStruct((nt * ncmp, 2 * LANES), f32))
            out_specs.append(pl.BlockSpec((ncmp, 2 * LANES), lambda i: (i, 0)))
        else:
            out_shape.append(jax.ShapeDtypeStruct((n, o[0]), o[1]))
            out_specs.append(row(o[0]))
    return pl.pallas_call(
        functools.partial(_proj_kernel, tiles_per_batch=tiles_per_batch, with_blocks=with_blocks),
        grid=(nt,),
        in_specs=[row(D_MODEL), full(norm_g), full(w_perm), full(gain_row), pos_spec, pos_spec,
                  full(bd), full(bf_row), full(tri)],
        out_specs=out_specs,
        out_shape=out_shape,
        scratch_shapes=[pltpu.VMEM((1, LANES), f32)],
        compiler_params=pltpu.CompilerParams(dimension_semantics=("arbitrary",),
                                             vmem_limit_bytes=VMEM_LIMIT),
        name="proj",
    )(x2d, norm_g, w_perm, gain_row, cos, sin, bd, bf_row, tri)


def _layer_tables(w_in_l, nsa_qk_g_l, fox_qk_g_l, fox_b_f_l, dsa_qk_g_l):
    perm = _proj_column_permutation()
    w_ext = jnp.concatenate([w_in_l, jnp.zeros((D_MODEL, 1), w_in_l.dtype)], axis=1)
    w_perm = jnp.take(w_ext, perm, axis=1).astype(jnp.float32)
    w_mg = w_in_l[:, _O_MG:].astype(jnp.float32)
    one = jnp.ones((HEAD_DIM,), jnp.float32)
    two = lambda g: jnp.concatenate([g, g])
    ones2 = two(one)
    parts = [two(nsa_qk_g_l[0])] * 4
    parts += [two(nsa_qk_g_l[1]), ones2, two(nsa_qk_g_l[2]), ones2, two(nsa_qk_g_l[3]), ones2]
    parts += [two(fox_qk_g_l[0])] * 2 + [two(fox_qk_g_l[1])] * 2 + [ones2] * 2
    parts += [jnp.concatenate([dsa_qk_g_l[0], one])] * 4
    parts += [jnp.concatenate([dsa_qk_g_l[1], one]), ones2]
    gain_row = jnp.concatenate(parts).astype(jnp.float32)[None, :]
    bf_row = jnp.zeros((LANES,), jnp.float32).at[_M_FF:_M_IW].set(fox_b_f_l.astype(jnp.float32))[None, :]
    return w_perm, w_mg, gain_row, bf_row


_NT = (((1,), (1,)), ((), ()))


def _half_masks():
    lane = lax.broadcasted_iota(jnp.int32, (1, LANES), 1)
    return lane < HEAD_DIM, lane >= HEAD_DIM


def _online_update(s, valid, v, m_prev, l_prev, acc_prev):
    s = jnp.where(valid, s, -jnp.inf)
    m_new = jnp.maximum(m_prev, jnp.max(s, axis=-1, keepdims=True))
    p = jnp.exp(s - m_new)
    alpha = jnp.exp(m_prev - m_new)
    l_new = alpha * l_prev + jnp.sum(p, axis=-1, keepdims=True)
    pv = jnp.dot(p.reshape(-1, p.shape[-1]).astype(jnp.bfloat16), v, preferred_element_type=jnp.float32)
    acc_new = alpha * acc_prev + pv.reshape(acc_prev.shape)
    return m_new, l_new, acc_new


def _two_stage_loop(n, produce, consume, buf0, buf1):
    produce(0, buf0)

    def pair(j, _):
        c = 2 * j
        produce(c + 1, buf1)
        consume(c, buf0)
        produce(c + 2, buf0)
        consume(c + 1, buf1)
        return 0

    pairs = (n - 1) // 2
    lax.fori_loop(0, pairs, pair, 0)
    c = 2 * pairs

    @pl.when(c + 1 < n)
    def _():
        produce(c + 1, buf1)
        consume(c, buf0)
        consume(c + 1, buf1)

    @pl.when(c + 1 >= n)
    def _():
        consume(c, buf0)


def _finish(l, acc):
    return acc / jnp.where(l > 0, l, 1.0)


def _fox_kernel(q_ref, kv_ref, cq_ref, ck_ref, o_ref, m_ref, l_ref, acc_ref, *, tq, tk):
    i = pl.program_id(1)
    j = pl.program_id(2)
    left, right = _half_masks()

    @pl.when(j == 0)
    def _():
        m_ref[...] = jnp.full_like(m_ref, NEG_BIG)
        l_ref[...] = jnp.zeros_like(l_ref)
        acc_ref[...] = jnp.zeros_like(acc_ref)

    @pl.when(j * tk < (i + 1) * tq)
    def _():
        rows = i * tq + lax.broadcasted_iota(jnp.int32, (tq, 1), 0)
        cols = j * tk + lax.broadcasted_iota(jnp.int32, (1, tk), 1)
        valid = cols <= rows
        for h in range(FOX_HEADS):
            c = h // 2
            qc = q_ref[:, c * LANES:(c + 1) * LANES]
            qh = jnp.where(left if h % 2 == 0 else right, qc, jnp.zeros_like(qc))
            k = kv_ref[:, c * LANES:(c + 1) * LANES]
            v = kv_ref[:, (2 + c) * LANES:(3 + c) * LANES]
            s = lax.dot_general(qh, k, _NT, preferred_element_type=jnp.float32)
            s = s + cq_ref[:, _M_FF + h:_M_FF + h + 1] - ck_ref[h:h + 1, :]
            m_ref[h], l_ref[h], acc_ref[h] = _online_update(s, valid, v, m_ref[h], l_ref[h], acc_ref[h])

    @pl.when(j == pl.num_programs(2) - 1)
    def _():
        for c in range(FOX_HEADS // 2):
            o = jnp.where(left, _finish(l_ref[2 * c], acc_ref[2 * c]), _finish(l_ref[2 * c + 1], acc_ref[2 * c + 1]))
            o_ref[:, c * LANES:(c + 1) * LANES] = o.astype(o_ref.dtype)


def _fox_prompt(qf, fox_kv, cum, cum_t, *, batch, seq, tq, tk):
    nq = seq // tq
    nk = seq // tk
    f32 = jnp.float32
    last = lambda i, j: jnp.minimum(j, ((i + 1) * tq - 1) // tk)
    return pl.pallas_call(
        functools.partial(_fox_kernel, tq=tq, tk=tk),
        grid=(batch, nq, nk),
        in_specs=[pl.BlockSpec((tq, 2 * LANES), lambda b, i, j: (b * nq + i, 0)),
                  pl.BlockSpec((tk, 4 * LANES), lambda b, i, j: (b * nk + last(i, j), 0)),
                  pl.BlockSpec((tq, LANES), lambda b, i, j: (b * nq + i, 0)),
                  pl.BlockSpec((None, FOX_HEADS, tk), lambda b, i, j: (b, 0, last(i, j)))],
        out_specs=pl.BlockSpec((tq, 2 * LANES), lambda b, i, j: (b * nq + i, 0)),
        out_shape=jax.ShapeDtypeStruct((batch * seq, 2 * LANES), jnp.bfloat16),
        scratch_shapes=[pltpu.VMEM((FOX_HEADS, tq, 1), f32), pltpu.VMEM((FOX_HEADS, tq, 1), f32),
                        pltpu.VMEM((FOX_HEADS, tq, LANES), f32)],
        compiler_params=pltpu.CompilerParams(dimension_semantics=("arbitrary", "arbitrary", "arbitrary")),
        name="fox_prompt",
    )(qf, fox_kv, cum, cum_t)


def _select_top_blocks(score, n_top):
    nb = score.shape[-1]
    blk = lax.broadcasted_iota(jnp.int32, (1, nb), 1).astype(jnp.float32)

    def body(_, carry):
        sc, sel = carry
        mx = jnp.max(sc, axis=-1, keepdims=True)
        first = jnp.min(jnp.where(sc == mx, blk, float(nb)), axis=-1, keepdims=True)
        pick = blk == first
        return jnp.where(pick, -jnp.inf, sc), jnp.where(pick, 1.0, sel)

    _, sel = lax.fori_loop(0, n_top, body, (score, jnp.zeros_like(score)))
    return sel


def _block_importance_scores(p_sum, qpos, nb):
    blk = lax.broadcasted_iota(jnp.int32, (1, nb), 1)
    cur = qpos // NSA_BLOCK
    forced = (blk == 0) | (blk == cur) | (blk == cur - 1)
    score = jnp.where(forced, p_sum + NSA_FORCE_BONUS, p_sum)
    return jnp.where(blk <= cur, score, -1.0)


def _masked_softmax(s, valid):
    s = jnp.where(valid, s, NEG_BIG)
    m = jnp.max(s, axis=-1, keepdims=True)
    e = jnp.where(valid, jnp.exp(s - m), 0.0)
    d = jnp.sum(e, axis=-1, keepdims=True)
    return e / jnp.where(d > 0, d, 1.0)


def _nsa_kernel(q_ref, misc_ref, cmp_ref, slc_ref, win_ref, o_ref, m_sc, l_sc, acc_sc, s0_sc, s1_sc,
                *, tq, tk, seq):
    i = pl.program_id(1)
    nb = seq // NSA_BLOCK
    g4 = NSA_GROUP
    left, right = _half_masks()
    qpos = i * tq + lax.broadcasted_iota(jnp.int32, (tq, 1), 0)
    blk = lax.broadcasted_iota(jnp.int32, (1, nb), 1)
    done = (blk + 1) * NSA_BLOCK <= qpos + 1
    kcm = cmp_ref[:, 0:LANES]
    vcm = cmp_ref[:, LANES:2 * LANES]
    lane_pos = lax.broadcasted_iota(jnp.int32, (1, tk), 1)
    n_top = min(NSA_TOP_BLOCKS, nb)
    n_chunks = ((i + 1) * tq + tk - 1) // tk
    tw = min(tq + NSA_WINDOW, seq)
    win_start = pl.multiple_of(jnp.clip(i * tq - NSA_WINDOW, 0, seq - tw), 16)
    win_pos = win_start + lax.broadcasted_iota(jnp.int32, (1, tw), 1)
    win_dist = qpos - win_pos
    win_valid = (win_dist >= 0) & (win_dist <= NSA_WINDOW)
    qs = [q_ref[:, c * LANES:(c + 1) * LANES] for c in range(g4)]
    qgs, o_cs, scores = [], [], []
    for g in range(NSA_KV_HEADS):
        hm = left if g == 0 else right
        qg = jnp.concatenate([jnp.where(hm, q, jnp.zeros_like(q)) for q in qs], axis=0)
        s_c = lax.dot_general(qg, kcm, _NT, preferred_element_type=jnp.float32).reshape(g4, tq, nb)
        p_c = _masked_softmax(s_c, done[None])
        o_cs.append(jnp.dot(p_c.reshape(g4 * tq, nb).astype(jnp.bfloat16), vcm,
                            preferred_element_type=jnp.float32).reshape(g4, tq, LANES))
        scores.append(_block_importance_scores(jnp.sum(p_c, axis=0), qpos, nb))
        qgs.append(qg)
    sel_all = _select_top_blocks(jnp.concatenate(scores, axis=0), n_top).astype(jnp.bfloat16)

    outs = []
    for g in range(NSA_KV_HEADS):
        qg, o_c = qgs[g], o_cs[g]
        sel = sel_all[g * tq:(g + 1) * tq]

        m_sc[...] = jnp.full_like(m_sc, NEG_BIG)
        l_sc[...] = jnp.zeros_like(l_sc)
        acc_sc[...] = jnp.zeros_like(acc_sc)

        def scores(c, dst, qg=qg):
            start = pl.multiple_of(c * tk, tk)
            dst[...] = lax.dot_general(qg, slc_ref[pl.ds(start, tk), 0:LANES], _NT,
                                       preferred_element_type=jnp.float32)

        def attend(c, src, sel=sel):
            start = pl.multiple_of(c * tk, tk)
            v = slc_ref[pl.ds(start, tk), LANES:2 * LANES]
            kpos = c * tk + lane_pos
            expand = (lax.broadcasted_iota(jnp.int32, (nb, 1), 0) == kpos // NSA_BLOCK)
            chosen = jnp.dot(sel, jnp.where(expand, 1.0, 0.0).astype(jnp.bfloat16),
                             preferred_element_type=jnp.float32) > 0.5
            valid = chosen & (kpos <= qpos)
            s = src[...].reshape(g4, tq, tk)
            m_sc[...], l_sc[...], acc_sc[...] = _online_update(s, valid[None], v, m_sc[...], l_sc[...], acc_sc[...])

        _two_stage_loop(n_chunks, scores, attend, s0_sc, s1_sc)
        o_s = _finish(l_sc[...], acc_sc[...])

        k = win_ref[pl.ds(win_start, tw), 0:LANES]
        v = win_ref[pl.ds(win_start, tw), LANES:2 * LANES]
        s_w = lax.dot_general(qg, k, _NT, preferred_element_type=jnp.float32).reshape(g4, tq, tw)
        p_w = _masked_softmax(s_w, win_valid[None])
        o_w = jnp.dot(p_w.reshape(g4 * tq, tw).astype(jnp.bfloat16), v,
                      preferred_element_type=jnp.float32).reshape(g4, tq, LANES)

        heads = []
        for c in range(g4):
            h = g * g4 + c
            gate = lambda br: misc_ref[:, _M_GATE + br * NSA_HEADS + h:_M_GATE + br * NSA_HEADS + h + 1]
            heads.append(gate(0) * o_c[c] + gate(1) * o_s[c] + gate(2) * o_w[c])
        outs.append(heads)
    for c in range(g4):
        o_ref[:, c * LANES:(c + 1) * LANES] = jnp.where(left, outs[0][c], outs[1][c]).astype(o_ref.dtype)


def _nsa_prompt(qn, misc, cmp_bf, slc_kv, win_kv, *, batch, seq, tq, tk):
    nq = seq // tq
    nb = seq // NSA_BLOCK
    f32 = jnp.float32
    g4 = NSA_GROUP
    return pl.pallas_call(
        functools.partial(_nsa_kernel, tq=tq, tk=tk, seq=seq),
        grid=(batch, nq),
        in_specs=[pl.BlockSpec((tq, 4 * LANES), lambda b, i: (b * nq + i, 0)),
                  pl.BlockSpec((tq, LANES), lambda b, i: (b * nq + i, 0)),
                  pl.BlockSpec((nb, 2 * LANES), lambda b, i: (b, 0)),
                  pl.BlockSpec((seq, 2 * LANES), lambda b, i: (b, 0)),
                  pl.BlockSpec((seq, 2 * LANES), lambda b, i: (b, 0))],
        out_specs=pl.BlockSpec((tq, 4 * LANES), lambda b, i: (b * nq + i, 0)),
        out_shape=jax.ShapeDtypeStruct((batch * seq, 4 * LANES), jnp.bfloat16),
        scratch_shapes=[pltpu.VMEM((g4, tq, 1), f32), pltpu.VMEM((g4, tq, 1), f32),
                        pltpu.VMEM((g4, tq, LANES), f32),
                        pltpu.VMEM((g4 * tq, tk), f32), pltpu.VMEM((g4 * tq, tk), f32)],
        compiler_params=pltpu.CompilerParams(dimension_semantics=("arbitrary", "arbitrary"),
                                             vmem_limit_bytes=VMEM_LIMIT),
        name="nsa_prompt",
    )(qn, misc, cmp_bf, slc_kv, win_kv)


_INT_MIN = -2 ** 31


def _sortable_key(x):
    bits = lax.bitcast_convert_type(x, jnp.int32)
    return jnp.where(bits < 0, bits ^ jnp.int32(0x7FFFFFFF), bits)


def _lane_fold(x):
    acc = x[:, 0:LANES]
    for c in range(1, x.shape[-1] // LANES):
        acc = acc + x[:, c * LANES:(c + 1) * LANES]
    return acc


def _dsa_kernel(dqi_ref, misc_ref, dki_ref, v_ref, upper_ref, o_ref, key_sc, seen_sc, m_sc, l_sc, acc_sc,
                s0_sc, s1_sc, *, tq, tk, n_keep):
    i = pl.program_id(1)
    nh = DSA_HEADS
    left, right = _half_masks()
    qpos = i * tq + lax.broadcasted_iota(jnp.int32, (tq, 1), 0)
    lane_pos = lax.broadcasted_iota(jnp.int32, (1, tk), 1)
    chunks = [dqi_ref[:, h * LANES:(h + 1) * LANES] for h in range(nh)]
    q_att = jnp.concatenate([jnp.where(left, q, jnp.zeros_like(q)) for q in chunks], axis=0)
    q_idx = jnp.concatenate([jnp.where(right, q, jnp.zeros_like(q)) for q in chunks], axis=0)
    n_chunks = ((i + 1) * tq + tk - 1) // tk

    def score_body(c, _):
        start = pl.multiple_of(c * tk, tk)
        kk = dki_ref[pl.ds(start, tk), :]
        a = lax.dot_general(q_idx, kk, _NT, preferred_element_type=jnp.float32).reshape(nh, tq, tk)
        a = jnp.maximum(a, 0.0)
        sc = a[0] * misc_ref[:, _M_IW:_M_IW + 1]
        for h in range(1, nh):
            sc = sc + a[h] * misc_ref[:, _M_IW + h:_M_IW + h + 1]
        kpos = c * tk + lane_pos
        sc = jnp.where(kpos <= qpos, sc, -jnp.inf)
        key_sc[c] = _sortable_key(sc)
        return 0

    lax.fori_loop(0, n_chunks, score_body, 0)

    def count(pred):
        def body(c, acc):
            return acc + _lane_fold(jnp.where(pred(c), 1.0, 0.0))
        part = lax.fori_loop(0, n_chunks, body, jnp.zeros((tq, LANES), jnp.float32))
        return jnp.sum(part, axis=-1, keepdims=True)

    def bit_body(it, lo):
        cand = lo + lax.shift_left(jnp.int32(1), jnp.int32(31) - it)
        cnt = count(lambda c: key_sc[c] >= cand)
        return jnp.where(cnt >= n_keep, cand, lo)

    thr = lax.fori_loop(0, 32, bit_body, jnp.full((tq, 1), _INT_MIN, jnp.int32))

    need = n_keep - count(lambda c: key_sc[c] > thr)
    seen_sc[...] = jnp.zeros_like(seen_sc)
    m_sc[...] = jnp.full_like(m_sc, NEG_BIG)
    l_sc[...] = jnp.zeros_like(l_sc)
    acc_sc[...] = jnp.zeros_like(acc_sc)

    def scores(c, dst):
        start = pl.multiple_of(c * tk, tk)
        dst[...] = lax.dot_general(q_att, dki_ref[pl.ds(start, tk), :], _NT, preferred_element_type=jnp.float32)

    def attend(c, src):
        start = pl.multiple_of(c * tk, tk)
        vv = v_ref[pl.ds(start, tk), :]
        kpos = c * tk + lane_pos
        key = key_sc[c]
        tie = key == thr
        rank = seen_sc[...] + jnp.dot(jnp.where(tie, 1.0, 0.0).astype(jnp.bfloat16), upper_ref[...],
                                      preferred_element_type=jnp.float32)
        seen_sc[...] = rank[:, tk - 1:tk]
        valid = ((key > thr) | (tie & (rank <= need))) & (kpos <= qpos)
        s = src[...].reshape(nh, tq, tk)
        m_sc[...], l_sc[...], acc_sc[...] = _online_update(s, valid[None], vv, m_sc[...], l_sc[...], acc_sc[...])

    _two_stage_loop(n_chunks, scores, attend, s0_sc, s1_sc)
    o = _finish(l_sc[...], acc_sc[...])
    for h in range(nh):
        o_ref[:, h * LANES:(h + 1) * LANES] = jnp.where(left, o[h], 0.0).astype(o_ref.dtype)


def _dsa_prompt(dqi, misc, dki_bf, misc_bf, *, batch, seq, tq, tk, n_keep):
    nq = seq // tq
    nk = seq // tk
    f32 = jnp.float32
    nh = DSA_HEADS
    upper = jnp.asarray(np.triu(np.ones((tk, tk))), jnp.bfloat16)
    return pl.pallas_call(
        functools.partial(_dsa_kernel, tq=tq, tk=tk, n_keep=n_keep),
        grid=(batch, nq),
        in_specs=[pl.BlockSpec((tq, 4 * LANES), lambda b, i: (b * nq + i, 0)),
                  pl.BlockSpec((tq, LANES), lambda b, i: (b * nq + i, 0)),
                  pl.BlockSpec((seq, LANES), lambda b, i: (b, 0)),
                  pl.BlockSpec((seq, LANES), lambda b, i: (b, 0)),
                  pl.BlockSpec((tk, tk), lambda b, i: (0, 0))],
        out_specs=pl.BlockSpec((tq, 4 * LANES), lambda b, i: (b * nq + i, 0)),
        out_shape=jax.ShapeDtypeStruct((batch * seq, 4 * LANES), jnp.bfloat16),
        scratch_shapes=[pltpu.VMEM((nk, tq, tk), jnp.int32), pltpu.VMEM((tq, 1), f32),
                        pltpu.VMEM((nh, tq, 1), f32), pltpu.VMEM((nh, tq, 1), f32),
                        pltpu.VMEM((nh, tq, LANES), f32),
                        pltpu.VMEM((nh * tq, tk), f32), pltpu.VMEM((nh * tq, tk), f32)],
        compiler_params=pltpu.CompilerParams(dimension_semantics=("arbitrary", "arbitrary"),
                                             vmem_limit_bytes=VMEM_LIMIT),
        name="dsa_prompt",
    )(dqi, misc, dki_bf, misc_bf, upper)


def _merge_kernel(x_ref, g_ref, on_ref, of_ref, od_ref, wmg_ref, wbn_ref, wbf_ref, wbd_ref, wo_ref, out_ref):
    precise = wo_ref.dtype == jnp.float32
    cdt = jnp.float32 if precise else jnp.bfloat16
    dot = functools.partial(jnp.dot, preferred_element_type=jnp.float32,
                            precision=lax.Precision.HIGHEST if precise else None)
    x = x_ref[...]
    xn = _rms_rows(x, g_ref[...]).astype(cdt)
    y = None
    for br, (o_ref, wb_ref) in enumerate(((on_ref, wbn_ref), (of_ref, wbf_ref), (od_ref, wbd_ref))):
        gate = jax.nn.sigmoid(dot(xn, wmg_ref[:, br * D_MODEL:(br + 1) * D_MODEL]))
        term = gate * dot(o_ref[...], wb_ref[...])
        y = term if y is None else y + term
    out_ref[...] = x + dot(y.astype(cdt), wo_ref[...])


def _merge(x2d, norm_g, o_nsa, o_fox, o_dsa, w_mg, w_bn, w_bf, w_bd, w_o, *, tm):
    n = x2d.shape[0]
    row = lambda a: pl.BlockSpec((tm, a.shape[1]), lambda i: (i, 0))
    full = lambda a: pl.BlockSpec(a.shape, lambda i: (0,) * a.ndim)
    args = (x2d, norm_g, o_nsa, o_fox, o_dsa, w_mg, w_bn, w_bf, w_bd, w_o)
    return pl.pallas_call(
        _merge_kernel,
        grid=(n // tm,),
        in_specs=[row(x2d), full(norm_g), row(o_nsa), row(o_fox), row(o_dsa)] + [full(a) for a in args[5:]],
        out_specs=row(x2d),
        out_shape=jax.ShapeDtypeStruct(x2d.shape, x2d.dtype),
        compiler_params=pltpu.CompilerParams(dimension_semantics=("arbitrary",), vmem_limit_bytes=VMEM_LIMIT),
        name="merge",
    )(*args)


_R_GROUP = N_EXPERTS


def _route(logits):
    lane = lax.broadcasted_iota(jnp.int32, (1, LANES), 1)
    lanef = lane.astype(jnp.float32)
    is_grp = (lane >= _R_GROUP) & (lane < _R_GROUP + N_GROUPS)
    lg = jnp.where(is_grp, logits, -jnp.inf)
    gmax = jnp.max(lg, axis=-1, keepdims=True)
    grp = jnp.min(jnp.where(lg == gmax, lanef, float(LANES)), axis=-1, keepdims=True) - _R_GROUP
    g1 = 1.0 / jnp.sum(jnp.where(is_grp, jnp.exp(lg - gmax), 0.0), axis=-1, keepdims=True)
    in_grp = (lane < N_EXPERTS) & ((lane // EXPERTS_PER_GROUP).astype(jnp.float32) == grp)
    le = jnp.where(in_grp, logits, -jnp.inf)
    m1 = jnp.max(le, axis=-1, keepdims=True)
    i1 = jnp.min(jnp.where(le == m1, lanef, float(LANES)), axis=-1, keepdims=True)
    le2 = jnp.where(lanef == i1, -jnp.inf, le)
    m2 = jnp.max(le2, axis=-1, keepdims=True)
    i2 = jnp.min(jnp.where(le2 == m2, lanef, float(LANES)), axis=-1, keepdims=True)
    t = jnp.exp(m2 - m1)
    w1 = g1 * (1.0 / (1.0 + t))
    w2 = g1 * (t / (1.0 + t))
    return jnp.where(lanef == i1, w1, jnp.where(lanef == i2, w2, 0.0))


def _moe_kernel(x_ref, g_ref, wr_ref, br_ref, wg_ref, wu_ref, wd_ref, out_ref, xn_sc, comb_sc, acc_sc):
    e = pl.program_id(1)

    @pl.when(e == 0)
    def _():
        x = x_ref[...]
        xn = _rms_rows(x, g_ref[...])
        logits = jnp.dot(xn, wr_ref[...], preferred_element_type=jnp.float32,
                         precision=lax.Precision.HIGHEST) + br_ref[...]
        comb_sc[...] = _route(logits)
        xn_sc[...] = xn.astype(xn_sc.dtype)
        acc_sc[...] = x

    dot = functools.partial(jnp.dot, preferred_element_type=jnp.float32,
                            precision=lax.Precision.HIGHEST if wg_ref.dtype == jnp.float32 else None)
    xn = xn_sc[...]
    h = jax.nn.silu(dot(xn, wg_ref[...])) * dot(xn, wu_ref[...])
    y = dot(h.astype(xn.dtype), wd_ref[...])
    lane = lax.broadcasted_iota(jnp.int32, (1, LANES), 1)
    ce = jnp.sum(jnp.where(lane == e, comb_sc[...], 0.0), axis=-1, keepdims=True)
    acc_sc[...] += ce * y

    @pl.when(e == pl.num_programs(1) - 1)
    def _():
        out_ref[...] = acc_sc[...]


def _moe(x2d, norm_g, w_router, b_router, w_gate, w_up, w_down, *, tm):
    n = x2d.shape[0]
    f32 = jnp.float32
    return pl.pallas_call(
        _moe_kernel,
        grid=(n // tm, N_EXPERTS),
        in_specs=[pl.BlockSpec((tm, D_MODEL), lambda i, e: (i, 0)),
                  pl.BlockSpec((1, D_MODEL), lambda i, e: (0, 0)),
                  pl.BlockSpec((D_MODEL, LANES), lambda i, e: (0, 0)),
                  pl.BlockSpec((1, LANES), lambda i, e: (0, 0)),
                  pl.BlockSpec((None, D_MODEL, D_EXPERT), lambda i, e: (e, 0, 0)),
                  pl.BlockSpec((None, D_MODEL, D_EXPERT), lambda i, e: (e, 0, 0)),
                  pl.BlockSpec((None, D_EXPERT, D_MODEL), lambda i, e: (e, 0, 0))],
        out_specs=pl.BlockSpec((tm, D_MODEL), lambda i, e: (i, 0)),
        out_shape=jax.ShapeDtypeStruct(x2d.shape, x2d.dtype),
        scratch_shapes=[pltpu.VMEM((tm, D_MODEL), w_gate.dtype), pltpu.VMEM((tm, LANES), f32),
                        pltpu.VMEM((tm, D_MODEL), f32)],
        compiler_params=pltpu.CompilerParams(dimension_semantics=("arbitrary", "arbitrary"),
                                             vmem_limit_bytes=VMEM_LIMIT),
        name="moe",
    )(x2d, norm_g, w_router, b_router, w_gate, w_up, w_down)


def _seq_spec(shape):
    nd = len(shape)
    return pl.BlockSpec((None,) + tuple(shape[1:]), lambda b, s, pt: (b,) + (0,) * (nd - 1))


def _token0_page(col, dtype):
    first = lax.broadcasted_iota(jnp.int32, (1, PAGE_SIZE), 1) == 0
    return jnp.where(first, col, 0.0).astype(dtype)


def _hi_lo(x):
    hi = x.astype(jnp.bfloat16)
    return hi, (x - hi.astype(jnp.float32)).astype(jnp.bfloat16)


def _scores(q, keys_sc):
    return jnp.dot(q, keys_sc[...], preferred_element_type=jnp.float32)


def _weighted_values(p, values_sc):
    return lax.dot_general(p.astype(jnp.bfloat16), values_sc[...], _NT, preferred_element_type=jnp.float32)


def _split_dot_nt(a, b_bf16):
    hi = a.astype(jnp.bfloat16)
    lo = (a - hi.astype(jnp.float32)).astype(jnp.bfloat16)
    return (lax.dot_general(hi, b_bf16, _NT, preferred_element_type=jnp.float32)
            + lax.dot_general(lo, b_bf16, _NT, preferred_element_type=jnp.float32))


def _nsa_decode_kernel(pt_ref, qbd_ref, gate_ref, new_ref, neww_ref, cw_ref, *rest, pg, n_pages):
    page_refs = rest[:pg]
    o_ref, kc_sc, kcl_sc, vc_sc, ks_sc, vs_sc = rest[pg:]
    s = pl.program_id(1)
    n_tok = n_pages * PAGE_SIZE
    l_pad = n_tok + PAGE_SIZE
    nb_s = n_tok // NSA_BLOCK + 1
    nbpad = -(-nb_s // 8) * 8
    bf16 = jnp.bfloat16
    nh = NSA_HEADS

    for k in range(pg):
        page = page_refs[k][...]
        cols = slice(k * PAGE_SIZE, (k + 1) * PAGE_SIZE)
        kc_sc[:, cols], kcl_sc[:, cols] = _hi_lo(page[0 * LANES:1 * LANES])
        vc_sc[:, cols] = page[1 * LANES:2 * LANES].astype(bf16)
        ks_sc[:, cols] = page[2 * LANES:3 * LANES].astype(bf16)
        vs_sc[:, cols] = page[3 * LANES:4 * LANES].astype(bf16)

    @pl.when(s == pl.num_programs(1) - 1)
    def _():
        new = new_ref[...]
        kc_sc[:, n_tok:l_pad], kcl_sc[:, n_tok:l_pad] = _hi_lo(_token0_page(new[0 * LANES:1 * LANES], jnp.float32))
        vc_sc[:, n_tok:l_pad] = _token0_page(new[1 * LANES:2 * LANES], bf16)
        ks_sc[:, n_tok:l_pad] = _token0_page(new[2 * LANES:3 * LANES], bf16)
        vs_sc[:, n_tok:l_pad] = _token0_page(new[3 * LANES:4 * LANES], bf16)
        q_hl = qbd_ref[...]
        qbd = q_hl[0:nh]
        qpos = n_tok
        kpos = lax.broadcasted_iota(jnp.int32, (1, l_pad), 1)
        blk = lax.broadcasted_iota(jnp.int32, (1, nbpad), 1)
        incid = jnp.where(lax.broadcasted_iota(jnp.int32, (nbpad, 1), 0) == kpos // NSA_BLOCK, 1.0, 0.0).astype(bf16)
        inv = 1.0 / NSA_BLOCK

        s_hl = _scores(q_hl, kc_sc)
        s_tok = s_hl[0:nh] + s_hl[nh:2 * nh] + _scores(qbd, kcl_sc)
        s_c = _split_dot_nt(s_tok, incid) * inv
        done = ((blk + 1) * NSA_BLOCK <= qpos + 1) & (blk < nb_s)
        p_c = _masked_softmax(s_c, done)
        p_tok = jnp.dot(p_c.astype(bf16), incid, preferred_element_type=jnp.float32) * inv
        o_c = _weighted_values(p_tok, vc_sc)

        imp = jnp.sum(p_c.reshape(NSA_KV_HEADS, NSA_GROUP, nbpad), axis=1)
        score = _block_importance_scores(imp, jnp.full((NSA_KV_HEADS, 1), qpos, jnp.int32), nbpad)
        score = jnp.where(blk < nb_s, score, -2.0)
        sel = _select_top_blocks(score, min(NSA_TOP_BLOCKS, nb_s))
        sel8 = jnp.concatenate([jnp.broadcast_to(sel[g:g + 1], (NSA_GROUP, nbpad)) for g in range(NSA_KV_HEADS)],
                               axis=0).astype(bf16)
        chosen = jnp.dot(sel8, incid, preferred_element_type=jnp.float32) > 0.5
        p_s = _masked_softmax(_scores(qbd, ks_sc), chosen & (kpos <= qpos))
        o_s = _weighted_values(p_s, vs_sc)

        cw = cw_ref[...].astype(bf16)
        nw = neww_ref[...]
        win_keep = cw.shape[1]
        s_w = jnp.concatenate([jnp.dot(qbd, cw[0:LANES], preferred_element_type=jnp.float32),
                               jnp.dot(qbd, _token0_page(nw[0:LANES], bf16), preferred_element_type=jnp.float32)],
                              axis=1)
        wpos = lax.broadcasted_iota(jnp.int32, (1, win_keep + PAGE_SIZE), 1)
        p_w = _masked_softmax(s_w, wpos <= win_keep).astype(bf16)
        o_w = (lax.dot_general(p_w[:, :win_keep], cw[LANES:2 * LANES], _NT, preferred_element_type=jnp.float32)
               + lax.dot_general(p_w[:, win_keep:], _token0_page(nw[LANES:2 * LANES], bf16), _NT,
                                 preferred_element_type=jnp.float32))

        g = gate_ref[...]
        o_ref[...] = g[:, 0:1] * o_c + g[:, 1:2] * o_s + g[:, 2:3] * o_w


def _page_specs_t(layer, pg, rows):
    return [pl.BlockSpec((None, None, rows, PAGE_SIZE),
                         functools.partial(lambda b, s, pt, k: (layer, pt[b, s * pg + k], 0, 0), k=k))
            for k in range(pg)]


def _decode_call(kernel_fn, name, page_table, fixed, const, caches, layer, out_tail, scratch, *, pg):
    bs, n_pages = page_table.shape
    assert pg == n_pages, "the sample kernels place each page at a static offset of the sequence buffers"
    const_specs = [pl.BlockSpec(a.shape, functools.partial(lambda b, s, pt, nd: (0,) * nd, nd=a.ndim)) for a in const]
    page_specs = []
    for cache in caches:
        page_specs += _page_specs_t(layer, pg, cache.shape[2])
    grid_spec = pltpu.PrefetchScalarGridSpec(
        num_scalar_prefetch=1,
        grid=(bs, n_pages // pg),
        in_specs=[_seq_spec(a.shape) for a in fixed] + const_specs + page_specs,
        out_specs=_seq_spec((bs,) + out_tail),
        scratch_shapes=scratch)
    operands = list(fixed) + list(const)
    for cache in caches:
        operands += [cache] * pg
    return pl.pallas_call(
        kernel_fn,
        grid_spec=grid_spec,
        out_shape=jax.ShapeDtypeStruct((bs,) + out_tail, jnp.float32),
        compiler_params=pltpu.CompilerParams(dimension_semantics=("arbitrary", "arbitrary"),
                                             vmem_limit_bytes=VMEM_LIMIT),
        name=name,
    )(page_table, *operands)


def _nsa_decode(page_table, qbd, gates, new_t, neww_t, cache_win_t, cache_t, layer, *, pg):
    n_pages = page_table.shape[1]
    buf = pltpu.VMEM((LANES, (n_pages + 1) * PAGE_SIZE), jnp.bfloat16)
    return _decode_call(functools.partial(_nsa_decode_kernel, pg=pg, n_pages=n_pages), "nsa_sample", page_table,
                        (qbd, gates, new_t, neww_t, cache_win_t), (), (cache_t,), layer, (NSA_HEADS, LANES),
                        [buf, buf, buf, buf, buf], pg=pg)


def _fox_decode_kernel(pt_ref, qbd_ref, new_ref, newlf_ref, *rest, pg, n_pages):
    page_refs = rest[:pg]
    lf_refs = rest[pg:2 * pg]
    o_ref, k_sc, v_sc, lf_sc, cum_sc = rest[2 * pg:]
    s = pl.program_id(1)
    n_tok = n_pages * PAGE_SIZE
    l_pad = n_tok + PAGE_SIZE
    bf16 = jnp.bfloat16
    hi = lax.Precision.HIGHEST
    w = FOX_HEADS * HEAD_DIM

    for k in range(pg):
        page = page_refs[k][...]
        cols = slice(k * PAGE_SIZE, (k + 1) * PAGE_SIZE)
        k_sc[:, cols] = page[0:w].astype(bf16)
        v_sc[:, cols] = page[w:2 * w].astype(bf16)
        lf = lf_refs[k][...]
        for h in range(FOX_HEADS):
            lf_sc[h, k:k + 1, :] = lf[h:h + 1, :]

    @pl.when(s == pl.num_programs(1) - 1)
    def _():
        new = new_ref[...]
        k_sc[:, n_tok:l_pad] = _token0_page(new[0:w], bf16)
        v_sc[:, n_tok:l_pad] = _token0_page(new[w:2 * w], bf16)
        iota2 = lambda n, axis: lax.broadcasted_iota(jnp.int32, (n, n), axis)
        upper_incl = jnp.where(iota2(PAGE_SIZE, 0) <= iota2(PAGE_SIZE, 1), 1.0, 0.0)
        lower_strict = jnp.where(iota2(n_pages, 1) < iota2(n_pages, 0), 1.0, 0.0)
        cqs = []
        for h in range(FOX_HEADS):
            within = jnp.dot(lf_sc[h], upper_incl, preferred_element_type=jnp.float32, precision=hi)
            before = jnp.dot(lower_strict, within, preferred_element_type=jnp.float32, precision=hi)
            cum = within + before[:, PAGE_SIZE - 1:PAGE_SIZE]
            cum_sc[h] = cum
            cqs.append(cum[n_pages - 1:n_pages, PAGE_SIZE - 1:PAGE_SIZE] + newlf_ref[:, h:h + 1])
        cq = jnp.concatenate(cqs, axis=0)
        qbd = qbd_ref[...]
        decay = []
        for p in range(n_pages):
            ck = jnp.concatenate([cum_sc[h, p:p + 1, :] for h in range(FOX_HEADS)], axis=0)
            decay.append(cq - ck)
        decay.append(jnp.zeros((FOX_HEADS, PAGE_SIZE), jnp.float32))
        sc = _scores(qbd, k_sc) + jnp.concatenate(decay, axis=1)
        kpos = lax.broadcasted_iota(jnp.int32, (1, l_pad), 1)
        o_ref[...] = _weighted_values(_masked_softmax(sc, kpos <= n_tok), v_sc)


def _fox_decode(page_table, qbd, new_t, new_lf, cache_t, cache_lf_t, layer, *, pg):
    n_pages = page_table.shape[1]
    w = FOX_HEADS * HEAD_DIM
    buf = pltpu.VMEM((w, (n_pages + 1) * PAGE_SIZE), jnp.bfloat16)
    lfbuf = pltpu.VMEM((FOX_HEADS, n_pages, PAGE_SIZE), jnp.float32)
    return _decode_call(functools.partial(_fox_decode_kernel, pg=pg, n_pages=n_pages), "fox_sample", page_table,
                        (qbd, new_t, new_lf), (), (cache_t, cache_lf_t), layer, (FOX_HEADS, w),
                        [buf, buf, lfbuf, lfbuf], pg=pg)


def _dsa_decode_kernel(pt_ref, qatt_ref, qidx_ref, iw_ref, new_ref, *rest, pg, n_pages, n_keep):
    page_refs = rest[:pg]
    o_ref, kv_sc, ik_sc, ikl_sc = rest[pg:]
    s = pl.program_id(1)
    n_tok = n_pages * PAGE_SIZE
    l_pad = n_tok + PAGE_SIZE
    bf16 = jnp.bfloat16
    hd = HEAD_DIM
    nh = DSA_IDX_HEADS

    for k in range(pg):
        page = page_refs[k][...]
        cols = slice(k * PAGE_SIZE, (k + 1) * PAGE_SIZE)
        kv_sc[:, cols] = page[0:2 * hd].astype(bf16)
        ik_sc[:, cols], ikl_sc[:, cols] = _hi_lo(page[2 * hd:3 * hd])

    @pl.when(s == pl.num_programs(1) - 1)
    def _():
        new = new_ref[...]
        kv_sc[:, n_tok:l_pad] = _token0_page(new[0:2 * hd], bf16)
        ik_sc[:, n_tok:l_pad], ikl_sc[:, n_tok:l_pad] = _hi_lo(_token0_page(new[2 * hd:3 * hd], jnp.float32))
        kpos = lax.broadcasted_iota(jnp.int32, (1, l_pad), 1)
        causal = kpos <= n_tok
        q_hl = qidx_ref[...]
        a_hl = _scores(q_hl, ik_sc)
        a = a_hl[0:nh] + a_hl[nh:2 * nh] + _scores(q_hl[0:nh], ikl_sc)
        a = jnp.maximum(a, 0.0) * iw_ref[...]
        sc = a[0:1]
        for h in range(1, DSA_IDX_HEADS):
            sc = sc + a[h:h + 1]
        key = _sortable_key(jnp.where(causal, sc, -jnp.inf))

        def count(mask):
            return jnp.sum(jnp.where(mask, 1.0, 0.0), axis=-1, keepdims=True)

        def bit_body(it, lo):
            cand = lo + lax.shift_left(jnp.int32(1), jnp.int32(31) - it)
            return jnp.where(count(key >= cand) >= n_keep, cand, lo)

        thr = lax.fori_loop(0, 32, bit_body, jnp.full((1, 1), _INT_MIN, jnp.int32))
        need = n_keep - count(key > thr)
        tie = jnp.where(key == thr, kpos, jnp.int32(2 ** 30))
        idx_bits = max(1, (l_pad - 1).bit_length())

        def idx_body(it, bound):
            step = lax.shift_left(jnp.int32(1), jnp.int32(idx_bits - 1) - it)
            return jnp.where(count(tie <= bound + step - 1) < need, bound + step, bound)

        bound = lax.fori_loop(0, idx_bits, idx_body, jnp.zeros((1, 1), jnp.int32))
        valid = ((key > thr) | (tie <= bound)) & causal
        pr = _masked_softmax(_scores(qatt_ref[...], kv_sc), valid)
        o_ref[...] = _weighted_values(pr, kv_sc)


def _dsa_decode(page_table, q_att, q_idx, iw_col, new_t, cache_t, layer, *, pg, n_keep):
    n_pages = page_table.shape[1]
    hd = HEAD_DIM
    return _decode_call(functools.partial(_dsa_decode_kernel, pg=pg, n_pages=n_pages, n_keep=n_keep), "dsa_sample",
                        page_table, (q_att, q_idx, iw_col, new_t), (), (cache_t,), layer, (DSA_HEADS, LANES),
                        [pltpu.VMEM((2 * hd, (n_pages + 1) * PAGE_SIZE), jnp.bfloat16),
                         pltpu.VMEM((hd, (n_pages + 1) * PAGE_SIZE), jnp.bfloat16),
                         pltpu.VMEM((hd, (n_pages + 1) * PAGE_SIZE), jnp.bfloat16)], pg=pg)


def _merge_weights(w_bn, w_bf, w_bd, w_o):
    hd = HEAD_DIM
    rows = []
    for c in range(NSA_GROUP):
        rows += list(range(hd * c, hd * c + hd)) + list(range(hd * (c + NSA_GROUP), hd * (c + NSA_GROUP) + hd))
    w_bn_p = jnp.take(w_bn, np.asarray(rows, np.int32), axis=0)
    w_bd_p = jnp.pad(w_bd.reshape(DSA_HEADS, hd, D_MODEL), ((0, 0), (0, hd), (0, 0)))
    w_bd_p = w_bd_p.reshape(DSA_HEADS * LANES, D_MODEL)
    return w_bn_p, w_bf, w_bd_p, w_o


def _router_weights(w_rg, b_rg, w_re, b_re):
    pad = LANES - N_EXPERTS - N_GROUPS
    w = jnp.concatenate([w_re, w_rg, jnp.zeros((D_MODEL, pad), w_re.dtype)], axis=1).astype(jnp.float32)
    b = jnp.concatenate([b_re, b_rg, jnp.zeros((pad,), b_re.dtype)]).astype(jnp.float32)[None, :]
    return w, b


TM_PROJ = 512
TQ_FOX = 512
TQ_NSA = 256
TQ_DSA = 256
TK_ATTN = 1024
TM_MERGE = 512
TM_MOE = 1024


def _hi_lo_rows(q):
    hi = q.astype(jnp.bfloat16)
    lo = (q - hi.astype(jnp.float32)).astype(jnp.bfloat16)
    return jnp.concatenate([hi, lo], axis=1)


def _dsa_rows(dki, misc):
    return jnp.concatenate([dki[:, :HEAD_DIM], misc[:, :HEAD_DIM], dki[:, HEAD_DIM:]], axis=-1)


def kernel(x_prompt, x_sample, cache_nsa, cache_fox, cache_fox_logf, cache_dsa, cache_win, page_table, norm_attn_g, w_in, nsa_qk_g, fox_qk_g, fox_b_f, dsa_qk_g, w_branch_nsa, w_branch_fox, w_branch_dsa, w_out, norm_ffn_g, w_router_group, b_router_group, w_router_expert, b_router_expert, w_exp_gate, w_exp_up, w_exp_down):
    depth = w_in.shape[0]
    B, S, D = x_prompt.shape
    Bs, T, _ = x_sample.shape
    assert T == 1 and D == D_MODEL
    n_pages = page_table.shape[1]
    past_len = n_pages * PAGE_SIZE
    win_keep = cache_win.shape[2]
    n_phys = cache_nsa.shape[1]
    bf16 = jnp.bfloat16
    hd = HEAD_DIM

    tm_proj = min(TM_PROJ, S)
    cos_p, sin_p = _rope_tables(jnp.arange(S))
    cos_s, sin_s = _rope_tables(jnp.full((Bs,), past_len))
    pg = n_pages
    c_nsa = jnp.moveaxis(cache_nsa.reshape(depth, n_phys, PAGE_SIZE, 4 * LANES), 2, 3)
    c_fox = jnp.moveaxis(cache_fox.reshape(depth, n_phys, PAGE_SIZE, 2 * FOX_HEADS * hd), 2, 3)
    c_lf = jnp.moveaxis(cache_fox_logf.astype(jnp.float32), 2, 3)
    c_dsa = jnp.moveaxis(cache_dsa.reshape(depth, n_phys, PAGE_SIZE, 3 * hd), 2, 3)
    c_win = jnp.moveaxis(cache_win.reshape(depth, Bs, win_keep, 2 * LANES), 2, 3)
    lane = jnp.arange(LANES)
    left = lane < hd

    xp = x_prompt.reshape(B * S, D)
    xs = x_sample.reshape(Bs, D)
    outs = [[] for _ in range(10)]
    for l in range(depth):
        w_perm, w_mg, gain_row, bf_row = _layer_tables(w_in[l], nsa_qk_g[l], fox_qk_g[l], fox_b_f[l], dsa_qk_g[l])
        w_m = (w_mg,) + _merge_weights(w_branch_nsa[l], w_branch_fox[l], w_branch_dsa[l], w_out[l])
        w_m_bf = tuple(w.astype(bf16) for w in w_m)
        w_r, b_r = _router_weights(w_router_group[l], b_router_group[l], w_router_expert[l], b_router_expert[l])
        w_e = (w_exp_gate[l].astype(bf16), w_exp_up[l].astype(bf16), w_exp_down[l].astype(bf16))
        g_attn = norm_attn_g[l][None, :]
        g_ffn = norm_ffn_g[l][None, :]

        (qn, nsa_rows, win_rows, slc_kv, win_kv, cmp, qf, fox_rows, fox_kv, dqi, dki, dki_bf, misc, misc_bf,
         cum) = _project(xp, g_attn, w_perm.astype(bf16), gain_row, cos_p, sin_p, bf_row,
                         tm=tm_proj, tiles_per_batch=S // tm_proj, with_blocks=True)
        cum_t = cum[:, _M_FF:_M_IW].reshape(B, S, FOX_HEADS).transpose(0, 2, 1)
        o_fox = _fox_prompt(qf, fox_kv, cum, cum_t, batch=B, seq=S, tq=min(TQ_FOX, S), tk=min(TK_ATTN, S))
        o_nsa = _nsa_prompt(qn, misc, cmp.astype(bf16), slc_kv, win_kv, batch=B, seq=S, tq=min(TQ_NSA, S),
                            tk=min(TK_ATTN, S))
        o_dsa = _dsa_prompt(dqi, misc, dki_bf, misc_bf, batch=B, seq=S, tq=min(TQ_DSA, S), tk=min(TK_ATTN, S),
                            n_keep=min(DSA_TOPK, S // 4))
        xp = _merge(xp, g_attn, o_nsa, o_fox, o_dsa, *w_m_bf, tm=min(TM_MERGE, B * S))
        xp = _moe(xp, g_ffn, w_r, b_r, *w_e, tm=min(TM_MOE, B * S))
        outs[0].append(nsa_rows.reshape(B, S, 4, NSA_KV_HEADS, hd))
        outs[2].append(fox_rows.reshape(B, S, 2, FOX_HEADS, hd))
        outs[4].append(misc[:, _M_FF:_M_IW].reshape(B, S, FOX_HEADS))
        outs[6].append(_dsa_rows(dki, misc).reshape(B, S, 3, hd))
        outs[8].append(win_rows.reshape(B, S, 2, NSA_KV_HEADS, hd)[:, S - min(NSA_WINDOW, S):])

        (qn, nsa_rows, win_rows, _, _, _, qf, fox_rows, _, dqi, dki, _, misc, _, _) = _project(
            xs, g_attn, w_perm, gain_row, cos_s, sin_s, bf_row, tm=Bs, tiles_per_batch=1, with_blocks=False)
        chunks = qn.reshape(Bs, NSA_GROUP, LANES)
        qbd = _hi_lo_rows(jnp.concatenate([jnp.where(left, chunks, 0), jnp.where(left, 0, chunks)], axis=1))
        gates = misc[:, _M_GATE:_M_FF].reshape(Bs, 3, NSA_HEADS).transpose(0, 2, 1)
        gates = jnp.pad(gates, ((0, 0), (0, 0), (0, LANES - 3)))
        o = _nsa_decode(page_table, qbd, gates, nsa_rows[:, :, None], win_rows[:, :, None], c_win[l], c_nsa, l, pg=pg)
        o_nsa = jnp.where(left, o[:, :NSA_GROUP], o[:, NSA_GROUP:]).reshape(Bs, 4 * LANES)

        head_of_lane = jnp.arange(FOX_HEADS * hd) // hd
        qbd_f = jnp.where(head_of_lane[None, None, :] == jnp.arange(FOX_HEADS)[None, :, None], qf[:, None, :], 0)
        qbd_f = qbd_f.astype(bf16)
        o = _fox_decode(page_table, qbd_f, fox_rows[:, :, None], misc[:, None, _M_FF:_M_IW], c_fox, c_lf, l, pg=pg)
        o_fox = jnp.einsum('bhhd->bhd', o.reshape(Bs, FOX_HEADS, FOX_HEADS, hd)).reshape(Bs, FOX_HEADS * hd)

        chunks = dqi.reshape(Bs, DSA_HEADS, LANES)
        q_att = jnp.where(left, chunks, 0).astype(bf16)
        q_idx = _hi_lo_rows(chunks[..., hd:])
        dsa_new = _dsa_rows(dki, misc)
        o = _dsa_decode(page_table, q_att, q_idx, misc[:, _M_IW:_M_END, None], dsa_new[:, :, None], c_dsa, l,
                        pg=pg, n_keep=min(DSA_TOPK, (past_len + 1) // 4))
        o_dsa = jnp.concatenate([o[..., hd:], jnp.zeros_like(o[..., hd:])], axis=-1).reshape(Bs, 4 * LANES)

        xs = _merge(xs, g_attn, o_nsa, o_fox, o_dsa, *w_m, tm=Bs)
        xs = _moe(xs, g_ffn, w_r, b_r, w_exp_gate[l], w_exp_up[l], w_exp_down[l], tm=Bs)
        outs[1].append(nsa_rows.reshape(Bs, 1, 4, NSA_KV_HEADS, hd))
        outs[3].append(fox_rows.reshape(Bs, 1, 2, FOX_HEADS, hd))
        outs[5].append(misc[:, _M_FF:_M_IW].reshape(Bs, 1, FOX_HEADS))
        outs[7].append(dsa_new.reshape(Bs, 1, 3, hd))
        win_all = jnp.concatenate([cache_win[l], win_rows.reshape(Bs, 1, 2, NSA_KV_HEADS, hd)], axis=1)
        outs[9].append(win_all[:, 1:])
    return (xp.reshape(B, S, D), xs.reshape(Bs, 1, D)) + tuple(jnp.stack(o) for o in outs)
```

```python
import functools

import numpy as np
import jax
import jax.numpy as jnp
from jax import lax
from jax.experimental import pallas as pl
from jax.experimental.pallas import tpu as pltpu

D_MODEL = 1024
HEAD_DIM = 64
HALF = HEAD_DIM // 2
NSA_HEADS = 8
NSA_KV_HEADS = 2
NSA_GROUP = NSA_HEADS // NSA_KV_HEADS
NSA_BLOCK = 64
NSA_TOP_BLOCKS = 16
NSA_WINDOW = 512
NSA_FORCE_BONUS = 8.0
FOX_HEADS = 4
DSA_HEADS = 4
DSA_IDX_HEADS = 4
DSA_TOPK = 256
N_GROUPS = 4
EXPERTS_PER_GROUP = 4
N_EXPERTS = N_GROUPS * EXPERTS_PER_GROUP
D_EXPERT = 512
ROPE_THETA = 10000.0
NORM_EPS = 1e-6
NEG_BIG = -1e30
PAGE_SIZE = 128
QK_SCALE = HEAD_DIM ** -0.5

LANES = 128
VMEM_LIMIT = 56 * 1024 * 1024

_O_NQ = 0
_O_NKV = _O_NQ + NSA_HEADS * HEAD_DIM
_O_NGATE = _O_NKV + 6 * NSA_KV_HEADS * HEAD_DIM
_O_FQKV = _O_NGATE + 3 * NSA_HEADS
_O_FF = _O_FQKV + 3 * FOX_HEADS * HEAD_DIM
_O_DQ = _O_FF + FOX_HEADS
_O_DKV = _O_DQ + DSA_HEADS * HEAD_DIM
_O_DIQ = _O_DKV + 2 * HEAD_DIM
_O_DIK = _O_DIQ + DSA_IDX_HEADS * HEAD_DIM
_O_DIW = _O_DIK + HEAD_DIM
_O_MG = _O_DIW + DSA_IDX_HEADS
D_IN = _O_MG + 3 * D_MODEL

_M_GATE = HEAD_DIM
_M_FF = _M_GATE + 3 * NSA_HEADS
_M_IW = _M_FF + FOX_HEADS
_M_END = _M_IW + DSA_IDX_HEADS

_C_NQ = 0
_C_NKV = 4
_C_FQ = 10
_C_FK = 12
_C_FV = 14
_C_DQI = 16
_C_DKI = 20
_C_MISC = 21
N_CHUNKS = 22
D_PROJ = N_CHUNKS * LANES


def _proj_column_permutation():
    idx = []
    for c in range(4):
        idx += list(range(_O_NQ + 64 * c, _O_NQ + 64 * c + 64))
        idx += list(range(_O_NQ + 64 * (c + 4), _O_NQ + 64 * (c + 4) + 64))
    idx += list(range(_O_NKV, _O_NKV + 768))
    idx += list(range(_O_FQKV, _O_FQKV + 768))
    for h in range(4):
        idx += list(range(_O_DQ + 64 * h, _O_DQ + 64 * h + 64))
        idx += list(range(_O_DIQ + 64 * h, _O_DIQ + 64 * h + 64))
    idx += list(range(_O_DKV, _O_DKV + 64)) + list(range(_O_DIK, _O_DIK + 64))
    idx += list(range(_O_DKV + 64, _O_DKV + 128))
    idx += list(range(_O_NGATE, _O_NGATE + 24)) + list(range(_O_FF, _O_FF + 4)) + list(range(_O_DIW, _O_DIW + 4))
    idx += [D_IN] * (LANES - _M_END)
    assert len(idx) == D_PROJ
    return np.asarray(idx, np.int32)


def _rope_tables(pos):
    inv = ROPE_THETA ** (-jnp.arange(HALF, dtype=jnp.float32) * 2.0 / HEAD_DIM)
    ang = pos.astype(jnp.float32)[:, None] * inv[None, :]
    cos, sin = jnp.cos(ang), jnp.sin(ang)
    cos = jnp.concatenate([cos, cos, cos, cos], axis=-1)
    sin = jnp.concatenate([-sin, sin, -sin, sin], axis=-1)
    return cos, sin


def _rms_rows(x, g):
    return x * lax.rsqrt(jnp.mean(jnp.square(x), axis=-1, keepdims=True) + NORM_EPS) * g


def _split_dot(a, b_bf16):
    hi = a.astype(jnp.bfloat16)
    lo = (a - hi.astype(jnp.float32)).astype(jnp.bfloat16)
    return (jnp.dot(hi, b_bf16, preferred_element_type=jnp.float32)
            + jnp.dot(lo, b_bf16, preferred_element_type=jnp.float32))


def _proj_kernel(x_ref, g_ref, w_ref, gain_ref, cos_ref, sin_ref, bd_ref, bf_ref, tri_ref,
                 qn_ref, nsa_rows_ref, win_rows_ref, slc_kv_ref, win_kv_ref, cmp_ref,
                 qf_ref, fox_rows_ref, fox_kv_ref, dqi_ref, dki_ref, dki_bf_ref,
                 misc_ref, misc_bf_ref, cum_ref, carry_ref, *, tiles_per_batch, with_blocks):
    i = pl.program_id(0)
    precise = w_ref.dtype == jnp.float32
    xn = _rms_rows(x_ref[...], g_ref[...])
    if not precise:
        xn = xn.astype(jnp.bfloat16)
    lane = lax.broadcasted_iota(jnp.int32, (1, LANES), 1)
    left = lane < HEAD_DIM
    first_half = (lane % HEAD_DIM) < HALF
    cos = cos_ref[...]
    sin = sin_ref[...]
    bd = bd_ref[...]

    def chunk(c):
        return jnp.dot(xn, w_ref[:, c * LANES:(c + 1) * LANES], preferred_element_type=jnp.float32,
                       precision=lax.Precision.HIGHEST if precise else None)

    def head_norm(h, c, only_left=False):
        ms = _split_dot(h * h, bd)
        y = h * lax.rsqrt(ms + NORM_EPS) * gain_ref[:, c * LANES:(c + 1) * LANES]
        return jnp.where(left, y, h) if only_left else y

    def rope(h):
        swapped = jnp.where(first_half, pltpu.roll(h, LANES - HALF, 1), pltpu.roll(h, HALF, 1))
        return h * cos + swapped * sin

    for c in range(4):
        q = rope(head_norm(chunk(_C_NQ + c), _C_NQ + c)) * QK_SCALE
        qn_ref[:, c * LANES:(c + 1) * LANES] = q.astype(qn_ref.dtype)

    kc = rope(head_norm(chunk(_C_NKV + 0), _C_NKV + 0))
    vc = chunk(_C_NKV + 1)
    ks = rope(head_norm(chunk(_C_NKV + 2), _C_NKV + 2))
    vs = chunk(_C_NKV + 3)
    kw = rope(head_norm(chunk(_C_NKV + 4), _C_NKV + 4))
    vw = chunk(_C_NKV + 5)
    nsa_rows_ref[:, 0 * LANES:1 * LANES] = kc
    nsa_rows_ref[:, 1 * LANES:2 * LANES] = vc
    nsa_rows_ref[:, 2 * LANES:3 * LANES] = ks
    nsa_rows_ref[:, 3 * LANES:4 * LANES] = vs
    win_rows_ref[:, 0:LANES] = kw
    win_rows_ref[:, LANES:2 * LANES] = vw
    slc_kv_ref[:, 0:LANES] = ks.astype(jnp.bfloat16)
    slc_kv_ref[:, LANES:2 * LANES] = vs.astype(jnp.bfloat16)
    win_kv_ref[:, 0:LANES] = kw.astype(jnp.bfloat16)
    win_kv_ref[:, LANES:2 * LANES] = vw.astype(jnp.bfloat16)
    if with_blocks:
        tm = kc.shape[0]
        nblk = tm // NSA_BLOCK
        cmp_ref[:, 0:LANES] = jnp.mean(kc.reshape(nblk, NSA_BLOCK, LANES), axis=1)
        cmp_ref[:, LANES:2 * LANES] = jnp.mean(vc.reshape(nblk, NSA_BLOCK, LANES), axis=1)
    else:
        cmp_ref[...] = jnp.zeros_like(cmp_ref)

    for c in range(2):
        qf = head_norm(chunk(_C_FQ + c), _C_FQ + c) * QK_SCALE
        qf_ref[:, c * LANES:(c + 1) * LANES] = qf.astype(qf_ref.dtype)
        fk = head_norm(chunk(_C_FK + c), _C_FK + c)
        fv = chunk(_C_FV + c)
        fox_rows_ref[:, c * LANES:(c + 1) * LANES] = fk
        fox_rows_ref[:, (2 + c) * LANES:(3 + c) * LANES] = fv
        fox_kv_ref[:, c * LANES:(c + 1) * LANES] = fk.astype(jnp.bfloat16)
        fox_kv_ref[:, (2 + c) * LANES:(3 + c) * LANES] = fv.astype(jnp.bfloat16)

    for c in range(4):
        dqi = rope(head_norm(chunk(_C_DQI + c), _C_DQI + c, only_left=True)) * QK_SCALE
        dqi_ref[:, c * LANES:(c + 1) * LANES] = dqi.astype(dqi_ref.dtype)

    dki = rope(head_norm(chunk(_C_DKI), _C_DKI, only_left=True))
    dki_ref[...] = dki
    dki_bf_ref[...] = dki.astype(jnp.bfloat16)

    m = chunk(_C_MISC)
    zf = m + bf_ref[...]
    logf = jnp.minimum(zf, 0.0) - jnp.log(1.0 + jnp.exp(-jnp.abs(zf)))
    is_gate = (lane >= _M_GATE) & (lane < _M_FF)
    is_ff = (lane >= _M_FF) & (lane < _M_IW)
    is_iw = (lane >= _M_IW) & (lane < _M_END)
    out = jnp.where(is_gate, jax.nn.sigmoid(m), m)
    out = jnp.where(is_ff, logf, out)
    out = jnp.where(is_iw, m * (DSA_IDX_HEADS ** -0.5), out)
    misc_ref[...] = out
    misc_bf_ref[...] = out.astype(jnp.bfloat16)

    @pl.when(i % tiles_per_batch == 0)
    def _():
        carry_ref[...] = jnp.zeros_like(carry_ref)

    lf = jnp.where(is_ff, logf, 0.0)
    cum = jnp.dot(tri_ref[...], lf, preferred_element_type=jnp.float32,
                  precision=lax.Precision.HIGHEST) + carry_ref[...]
    cum_ref[...] = cum
    carry_ref[...] = cum[cum.shape[0] - 1:, :]


def _project(x2d, norm_g, w_perm, gain_row, cos, sin, bf_row, *, tm, tiles_per_batch, with_blocks):
    n = x2d.shape[0]
    nt = n // tm
    ncmp = max(tm // NSA_BLOCK, 8) if not with_blocks else tm // NSA_BLOCK
    bd = jnp.asarray(np.kron(np.eye(2), np.full((HEAD_DIM, HEAD_DIM), 1.0 / HEAD_DIM)), jnp.bfloat16)
    tri = jnp.asarray(np.tril(np.ones((tm, tm))), jnp.float32)
    f32, bf16 = jnp.float32, jnp.bfloat16
    row = lambda w: pl.BlockSpec((tm, w), lambda i: (i, 0))
    full = lambda a: pl.BlockSpec(a.shape, lambda i: (0,) * a.ndim)
    pos_spec = pl.BlockSpec((tm, LANES), lambda i: (i % tiles_per_batch, 0))
    qdt = w_perm.dtype
    outs = [
        (512, qdt),
        (512, f32),
        (256, f32),
        (256, bf16),
        (256, bf16),
        None,
        (256, qdt),
        (512, f32),
        (512, bf16),
        (512, qdt),
        (128, f32),
        (128, bf16),
        (128, f32),
        (128, bf16),
        (128, f32),
    ]
    out_shape, out_specs = [], []
    for o in outs:
        if o is None:
            out_shape.append(jax.ShapeDtypeStruct((nt * ncmp, 2 * LANES), f32))
            out_specs.append(pl.BlockSpec((ncmp, 2 * LANES), lambda i: (i, 0)))
        else:
            out_shape.append(jax.ShapeDtypeStruct((n, o[0]), o[1]))
            out_specs.append(row(o[0]))
    return pl.pallas_call(
        functools.partial(_proj_kernel, tiles_per_batch=tiles_per_batch, with_blocks=with_blocks),
        grid=(nt,),
        in_specs=[row(D_MODEL), full(norm_g), full(w_perm), full(gain_row), pos_spec, pos_spec,
                  full(bd), full(bf_row), full(tri)],
        out_specs=out_specs,
        out_shape=out_shape,
        scratch_shapes=[pltpu.VMEM((1, LANES), f32)],
        compiler_params=pltpu.CompilerParams(dimension_semantics=("arbitrary",),
                                             vmem_limit_bytes=VMEM_LIMIT),
        name="proj",
    )(x2d, norm_g, w_perm, gain_row, cos, sin, bd, bf_row, tri)


def _layer_tables(w_in_l, nsa_qk_g_l, fox_qk_g_l, fox_b_f_l, dsa_qk_g_l):
    perm = _proj_column_permutation()
    w_ext = jnp.concatenate([w_in_l, jnp.zeros((D_MODEL, 1), w_in_l.dtype)], axis=1)
    w_perm = jnp.take(w_ext, perm, axis=1).astype(jnp.float32)
    w_mg = w_in_l[:, _O_MG:].astype(jnp.float32)
    one = jnp.ones((HEAD_DIM,), jnp.float32)
    two = lambda g: jnp.concatenate([g, g])
    ones2 = two(one)
    parts = [two(nsa_qk_g_l[0])] * 4
    parts += [two(nsa_qk_g_l[1]), ones2, two(nsa_qk_g_l[2]), ones2, two(nsa_qk_g_l[3]), ones2]
    parts += [two(fox_qk_g_l[0])] * 2 + [two(fox_qk_g_l[1])] * 2 + [ones2] * 2
    parts += [jnp.concatenate([dsa_qk_g_l[0], one])] * 4
    parts += [jnp.concatenate([dsa_qk_g_l[1], one]), ones2]
    gain_row = jnp.concatenate(parts).astype(jnp.float32)[None, :]
    bf_row = jnp.zeros((LANES,), jnp.float32).at[_M_FF:_M_IW].set(fox_b_f_l.astype(jnp.float32))[None, :]
    return w_perm, w_mg, gain_row, bf_row


_NT = (((1,), (1,)), ((), ()))


def _half_masks():
    lane = lax.broadcasted_iota(jnp.int32, (1, LANES), 1)
    return lane < HEAD_DIM, lane >= HEAD_DIM


def _online_update(s, valid, v, m_prev, l_prev, acc_prev):
    s = jnp.where(valid, s, -jnp.inf)
    m_new = jnp.maximum(m_prev, jnp.max(s, axis=-1, keepdims=True))
    p = jnp.exp(s - m_new)
    alpha = jnp.exp(m_prev - m_new)
    l_new = alpha * l_prev + jnp.sum(p, axis=-1, keepdims=True)
    pv = jnp.dot(p.reshape(-1, p.shape[-1]).astype(jnp.bfloat16), v, preferred_element_type=jnp.float32)
    acc_new = alpha * acc_prev + pv.reshape(acc_prev.shape)
    return m_new, l_new, acc_new


def _two_stage_loop(n, produce, consume, buf0, buf1):
    produce(0, buf0)

    def pair(j, _):
        c = 2 * j
        produce(c + 1, buf1)
        consume(c, buf0)
        produce(c + 2, buf0)
        consume(c + 1, buf1)
        return 0

    pairs = (n - 1) // 2
    lax.fori_loop(0, pairs, pair, 0)
    c = 2 * pairs

    @pl.when(c + 1 < n)
    def _():
        produce(c + 1, buf1)
        consume(c, buf0)
        consume(c + 1, buf1)

    @pl.when(c + 1 >= n)
    def _():
        consume(c, buf0)


def _finish(l, acc):
    return acc / jnp.where(l > 0, l, 1.0)


def _fox_kernel(q_ref, kv_ref, cq_ref, ck_ref, o_ref, m_ref, l_ref, acc_ref, *, tq, tk):
    i = pl.program_id(1)
    j = pl.program_id(2)
    left, right = _half_masks()

    @pl.when(j == 0)
    def _():
        m_ref[...] = jnp.full_like(m_ref, NEG_BIG)
        l_ref[...] = jnp.zeros_like(l_ref)
        acc_ref[...] = jnp.zeros_like(acc_ref)

    @pl.when(j * tk < (i + 1) * tq)
    def _():
        rows = i * tq + lax.broadcasted_iota(jnp.int32, (tq, 1), 0)
        cols = j * tk + lax.broadcasted_iota(jnp.int32, (1, tk), 1)
        valid = cols <= rows
        for h in range(FOX_HEADS):
            c = h // 2
            qc = q_ref[:, c * LANES:(c + 1) * LANES]
            qh = jnp.where(left if h % 2 == 0 else right, qc, jnp.zeros_like(qc))
            k = kv_ref[:, c * LANES:(c + 1) * LANES]
            v = kv_ref[:, (2 + c) * LANES:(3 + c) * LANES]
            s = lax.dot_general(qh, k, _NT, preferred_element_type=jnp.float32)
            s = s + cq_ref[:, _M_FF + h:_M_FF + h + 1] - ck_ref[h:h + 1, :]
            m_ref[h], l_ref[h], acc_ref[h] = _online_update(s, valid, v, m_ref[h], l_ref[h], acc_ref[h])

    @pl.when(j == pl.num_programs(2) - 1)
    def _():
        for c in range(FOX_HEADS // 2):
            o = jnp.where(left, _finish(l_ref[2 * c], acc_ref[2 * c]), _finish(l_ref[2 * c + 1], acc_ref[2 * c + 1]))
            o_ref[:, c * LANES:(c + 1) * LANES] = o.astype(o_ref.dtype)


def _fox_prompt(qf, fox_kv, cum, cum_t, *, batch, seq, tq, tk):
    nq = seq // tq
    nk = seq // tk
    f32 = jnp.float32
    last = lambda i, j: jnp.minimum(j, ((i + 1) * tq - 1) // tk)
    return pl.pallas_call(
        functools.partial(_fox_kernel, tq=tq, tk=tk),
        grid=(batch, nq, nk),
        in_specs=[pl.BlockSpec((tq, 2 * LANES), lambda b, i, j: (b * nq + i, 0)),
                  pl.BlockSpec((tk, 4 * LANES), lambda b, i, j: (b * nk + last(i, j), 0)),
                  pl.BlockSpec((tq, LANES), lambda b, i, j: (b * nq + i, 0)),
                  pl.BlockSpec((None, FOX_HEADS, tk), lambda b, i, j: (b, 0, last(i, j)))],
        out_specs=pl.BlockSpec((tq, 2 * LANES), lambda b, i, j: (b * nq + i, 0)),
        out_shape=jax.ShapeDtypeStruct((batch * seq, 2 * LANES), jnp.bfloat16),
        scratch_shapes=[pltpu.VMEM((FOX_HEADS, tq, 1), f32), pltpu.VMEM((FOX_HEADS, tq, 1), f32),
                        pltpu.VMEM((FOX_HEADS, tq, LANES), f32)],
        compiler_params=pltpu.CompilerParams(dimension_semantics=("arbitrary", "arbitrary", "arbitrary")),
        name="fox_prompt",
    )(qf, fox_kv, cum, cum_t)


def _select_top_blocks(score, n_top):
    nb = score.shape[-1]
    blk = lax.broadcasted_iota(jnp.int32, (1, nb), 1).astype(jnp.float32)

    def body(_, carry):
        sc, sel = carry
        mx = jnp.max(sc, axis=-1, keepdims=True)
        first = jnp.min(jnp.where(sc == mx, blk, float(nb)), axis=-1, keepdims=True)
        pick = blk == first
        return jnp.where(pick, -jnp.inf, sc), jnp.where(pick, 1.0, sel)

    _, sel = lax.fori_loop(0, n_top, body, (score, jnp.zeros_like(score)))
    return sel


def _block_importance_scores(p_sum, qpos, nb):
    blk = lax.broadcasted_iota(jnp.int32, (1, nb), 1)
    cur = qpos // NSA_BLOCK
    forced = (blk == 0) | (blk == cur) | (blk == cur - 1)
    score = jnp.where(forced, p_sum + NSA_FORCE_BONUS, p_sum)
    return jnp.where(blk <= cur, score, -1.0)


def _masked_softmax(s, valid):
    s = jnp.where(valid, s, NEG_BIG)
    m = jnp.max(s, axis=-1, keepdims=True)
    e = jnp.where(valid, jnp.exp(s - m), 0.0)
    d = jnp.sum(e, axis=-1, keepdims=True)
    return e / jnp.where(d > 0, d, 1.0)


def _nsa_kernel(q_ref, misc_ref, cmp_ref, slc_ref, win_ref, o_ref, m_sc, l_sc, acc_sc, s0_sc, s1_sc,
                *, tq, tk, seq):
    i = pl.program_id(1)
    nb = seq // NSA_BLOCK
    g4 = NSA_GROUP
    left, right = _half_masks()
    qpos = i * tq + lax.broadcasted_iota(jnp.int32, (tq, 1), 0)
    blk = lax.broadcasted_iota(jnp.int32, (1, nb), 1)
    done = (blk + 1) * NSA_BLOCK <= qpos + 1
    kcm = cmp_ref[:, 0:LANES]
    vcm = cmp_ref[:, LANES:2 * LANES]
    lane_pos = lax.broadcasted_iota(jnp.int32, (1, tk), 1)
    n_top = min(NSA_TOP_BLOCKS, nb)
    n_chunks = ((i + 1) * tq + tk - 1) // tk
    tw = min(tq + NSA_WINDOW, seq)
    win_start = pl.multiple_of(jnp.clip(i * tq - NSA_WINDOW, 0, seq - tw), 16)
    win_pos = win_start + lax.broadcasted_iota(jnp.int32, (1, tw), 1)
    win_dist = qpos - win_pos
    win_valid = (win_dist >= 0) & (win_dist <= NSA_WINDOW)
    qs = [q_ref[:, c * LANES:(c + 1) * LANES] for c in range(g4)]
    qgs, o_cs, scores = [], [], []
    for g in range(NSA_KV_HEADS):
        hm = left if g == 0 else right
        qg = jnp.concatenate([jnp.where(hm, q, jnp.zeros_like(q)) for q in qs], axis=0)
        s_c = lax.dot_general(qg, kcm, _NT, preferred_element_type=jnp.float32).reshape(g4, tq, nb)
        p_c = _masked_softmax(s_c, done[None])
        o_cs.append(jnp.dot(p_c.reshape(g4 * tq, nb).astype(jnp.bfloat16), vcm,
                            preferred_element_type=jnp.float32).reshape(g4, tq, LANES))
        scores.append(_block_importance_scores(jnp.sum(p_c, axis=0), qpos, nb))
        qgs.append(qg)
    sel_all = _select_top_blocks(jnp.concatenate(scores, axis=0), n_top).astype(jnp.bfloat16)

    outs = []
    for g in range(NSA_KV_HEADS):
        qg, o_c = qgs[g], o_cs[g]
        sel = sel_all[g * tq:(g + 1) * tq]

        m_sc[...] = jnp.full_like(m_sc, NEG_BIG)
        l_sc[...] = jnp.zeros_like(l_sc)
        acc_sc[...] = jnp.zeros_like(acc_sc)

        def scores(c, dst, qg=qg):
            start = pl.multiple_of(c * tk, tk)
            dst[...] = lax.dot_general(qg, slc_ref[pl.ds(start, tk), 0:LANES], _NT,
                                       preferred_element_type=jnp.float32)

        def attend(c, src, sel=sel):
            start = pl.multiple_of(c * tk, tk)
            v = slc_ref[pl.ds(start, tk), LANES:2 * LANES]
            kpos = c * tk + lane_pos
            expand = (lax.broadcasted_iota(jnp.int32, (nb, 1), 0) == kpos // NSA_BLOCK)
            chosen = jnp.dot(sel, jnp.where(expand, 1.0, 0.0).astype(jnp.bfloat16),
                             preferred_element_type=jnp.float32) > 0.5
            valid = chosen & (kpos <= qpos)
            s = src[...].reshape(g4, tq, tk)
            m_sc[...], l_sc[...], acc_sc[...] = _online_update(s, valid[None], v, m_sc[...], l_sc[...], acc_sc[...])

        _two_stage_loop(n_chunks, scores, attend, s0_sc, s1_sc)
        o_s = _finish(l_sc[...], acc_sc[...])

        k = win_ref[pl.ds(win_start, tw), 0:LANES]
        v = win_ref[pl.ds(win_start, tw), LANES:2 * LANES]
        s_w = lax.dot_general(qg, k, _NT, preferred_element_type=jnp.float32).reshape(g4, tq, tw)
        p_w = _masked_softmax(s_w, win_valid[None])
        o_w = jnp.dot(p_w.reshape(g4 * tq, tw).astype(jnp.bfloat16), v,
                      preferred_element_type=jnp.float32).reshape(g4, tq, LANES)

        heads = []
        for c in range(g4):
            h = g * g4 + c
            gate = lambda br: misc_ref[:, _M_GATE + br * NSA_HEADS + h:_M_GATE + br * NSA_HEADS + h + 1]
            heads.append(gate(0) * o_c[c] + gate(1) * o_s[c] + gate(2) * o_w[c])
        outs.append(heads)
    for c in range(g4):
        o_ref[:, c * LANES:(c + 1) * LANES] = jnp.where(left, outs[0][c], outs[1][c]).astype(o_ref.dtype)


def _nsa_prompt(qn, misc, cmp_bf, slc_kv, win_kv, *, batch, seq, tq, tk):
    nq = seq // tq
    nb = seq // NSA_BLOCK
    f32 = jnp.float32
    g4 = NSA_GROUP
    return pl.pallas_call(
        functools.partial(_nsa_kernel, tq=tq, tk=tk, seq=seq),
        grid=(batch, nq),
        in_specs=[pl.BlockSpec((tq, 4 * LANES), lambda b, i: (b * nq + i, 0)),
                  pl.BlockSpec((tq, LANES), lambda b, i: (b * nq + i, 0)),
                  pl.BlockSpec((nb, 2 * LANES), lambda b, i: (b, 0)),
                  pl.BlockSpec((seq, 2 * LANES), lambda b, i: (b, 0)),
                  pl.BlockSpec((seq, 2 * LANES), lambda b, i: (b, 0))],
        out_specs=pl.BlockSpec((tq, 4 * LANES), lambda b, i: (b * nq + i, 0)),
        out_shape=jax.ShapeDtypeStruct((batch * seq, 4 * LANES), jnp.bfloat16),
        scratch_shapes=[pltpu.VMEM((g4, tq, 1), f32), pltpu.VMEM((g4, tq, 1), f32),
                        pltpu.VMEM((g4, tq, LANES), f32),
                        pltpu.VMEM((g4 * tq, tk), f32), pltpu.VMEM((g4 * tq, tk), f32)],
        compiler_params=pltpu.CompilerParams(dimension_semantics=("arbitrary", "arbitrary"),
                                             vmem_limit_bytes=VMEM_LIMIT),
        name="nsa_prompt",
    )(qn, misc, cmp_bf, slc_kv, win_kv)


_INT_MIN = -2 ** 31


_KEY_NEG_INF = -2139095041
_MIN_NORMAL_BITS = 0x00800000
_MIN_NORMAL = float(np.float32(2.0) ** -126)


def _key_value(k):
    bits = jnp.where(k < 0, k ^ jnp.int32(0x7FFFFFFF), k)
    v = lax.bitcast_convert_type(bits, jnp.float32)
    v = jnp.where(k < _KEY_NEG_INF, -jnp.inf, v)
    return jnp.where((k > 0) & (k < _MIN_NORMAL_BITS), _MIN_NORMAL, v)


def _lane_fold(x):
    acc = x[:, 0:LANES]
    for c in range(1, x.shape[-1] // LANES):
        acc = acc + x[:, c * LANES:(c + 1) * LANES]
    return acc


def _dsa_kernel(dqi_ref, misc_ref, dki_ref, v_ref, upper_ref, o_ref, key_sc, top_sc, seen_sc, m_sc, l_sc, acc_sc,
                s0_sc, s1_sc, *, tq, tk, n_keep):
    i = pl.program_id(1)
    nh = DSA_HEADS
    left, right = _half_masks()
    qpos = i * tq + lax.broadcasted_iota(jnp.int32, (tq, 1), 0)
    lane_pos = lax.broadcasted_iota(jnp.int32, (1, tk), 1)
    chunks = [dqi_ref[:, h * LANES:(h + 1) * LANES] for h in range(nh)]
    q_att = jnp.concatenate([jnp.where(left, q, jnp.zeros_like(q)) for q in chunks], axis=0)
    q_idx = jnp.concatenate([jnp.where(right, q, jnp.zeros_like(q)) for q in chunks], axis=0)
    n_chunks = ((i + 1) * tq + tk - 1) // tk

    def score_body(c, _):
        start = pl.multiple_of(c * tk, tk)
        kk = dki_ref[pl.ds(start, tk), :]
        a = lax.dot_general(q_idx, kk, _NT, preferred_element_type=jnp.float32).reshape(nh, tq, tk)
        a = jnp.maximum(a, 0.0)
        sc = a[0] * misc_ref[:, _M_IW:_M_IW + 1]
        for h in range(1, nh):
            sc = sc + a[h] * misc_ref[:, _M_IW + h:_M_IW + h + 1]
        kpos = c * tk + lane_pos
        sc = jnp.where(sc == 0.0, 0.0, sc)
        sc = jnp.where(kpos <= qpos, sc, -jnp.inf)
        key_sc[c] = sc
        bits = lax.bitcast_convert_type(sc, jnp.int32)
        top = lax.bitcast_convert_type(bits & jnp.int32(-65536), jnp.float32)
        top_sc[c] = top.astype(jnp.bfloat16)
        return 0

    lax.fori_loop(0, n_chunks, score_body, 0)

    def count(pred):
        def body(c, acc):
            return acc + _lane_fold(jnp.where(pred(c), 1.0, 0.0))
        part = lax.fori_loop(0, n_chunks, body, jnp.zeros((tq, LANES), jnp.float32))
        return jnp.sum(part, axis=-1, keepdims=True)

    one, zero = jnp.bfloat16(1.0), jnp.bfloat16(0.0)

    def top_body(it, lo16):
        cand16 = lo16 + lax.shift_left(jnp.int32(1), jnp.int32(15) - it)
        raw = jnp.where(cand16 < 0, cand16 ^ jnp.int32(0x7FFF), cand16)
        raw = jnp.where((cand16 > 0) & (cand16 < 0x80), jnp.int32(0x80), raw)
        cand = lax.bitcast_convert_type(lax.shift_left(raw, jnp.int32(16)), jnp.float32).astype(jnp.bfloat16)

        def body(c, acc):
            return acc + _lane_fold(jnp.where(top_sc[c] >= cand, one, zero))

        part = lax.fori_loop(0, n_chunks, body, jnp.zeros((tq, LANES), jnp.bfloat16))
        cnt = jnp.sum(part.astype(jnp.float32), axis=-1, keepdims=True)
        return jnp.where(cnt >= n_keep, cand16, lo16)

    lo16 = lax.fori_loop(0, 16, top_body, jnp.full((tq, 1), -2 ** 15, jnp.int32))

    def bit_body(it, lo):
        cand = lo + lax.shift_left(jnp.int32(1), jnp.int32(15) - it)
        cand_value = _key_value(cand)
        cnt = count(lambda c: key_sc[c] >= cand_value)
        return jnp.where(cnt >= n_keep, cand, lo)

    thr = _key_value(lax.fori_loop(0, 16, bit_body, lax.shift_left(lo16, jnp.int32(16))))

    need = n_keep - count(lambda c: key_sc[c] > thr)
    seen_sc[...] = jnp.zeros_like(seen_sc)
    m_sc[...] = jnp.full_like(m_sc, NEG_BIG)
    l_sc[...] = jnp.zeros_like(l_sc)
    acc_sc[...] = jnp.zeros_like(acc_sc)

    def scores(c, dst):
        start = pl.multiple_of(c * tk, tk)
        dst[...] = lax.dot_general(q_att, dki_ref[pl.ds(start, tk), :], _NT, preferred_element_type=jnp.float32)

    def attend(c, src):
        start = pl.multiple_of(c * tk, tk)
        vv = v_ref[pl.ds(start, tk), :]
        kpos = c * tk + lane_pos
        key = key_sc[c]
        tie = key == thr
        rank = seen_sc[...] + jnp.dot(jnp.where(tie, 1.0, 0.0).astype(jnp.bfloat16), upper_ref[...],
                                      preferred_element_type=jnp.float32)
        seen_sc[...] = rank[:, tk - 1:tk]
        valid = ((key > thr) | (tie & (rank <= need))) & (kpos <= qpos)
        s = src[...].reshape(nh, tq, tk)
        m_sc[...], l_sc[...], acc_sc[...] = _online_update(s, valid[None], vv, m_sc[...], l_sc[...], acc_sc[...])

    _two_stage_loop(n_chunks, scores, attend, s0_sc, s1_sc)
    o = _finish(l_sc[...], acc_sc[...])
    for h in range(nh):
        o_ref[:, h * LANES:(h + 1) * LANES] = jnp.where(left, o[h], 0.0).astype(o_ref.dtype)


def _dsa_prompt(dqi, misc, dki_bf, misc_bf, *, batch, seq, tq, tk, n_keep):
    nq = seq // tq
    nk = seq // tk
    f32 = jnp.float32
    nh = DSA_HEADS
    assert seq // LANES <= 256, "per-lane key counts are accumulated in bf16 (exact up to 256)"
    upper = jnp.asarray(np.triu(np.ones((tk, tk))), jnp.bfloat16)
    return pl.pallas_call(
        functools.partial(_dsa_kernel, tq=tq, tk=tk, n_keep=n_keep),
        grid=(batch, nq),
        in_specs=[pl.BlockSpec((tq, 4 * LANES), lambda b, i: (b * nq + i, 0)),
                  pl.BlockSpec((tq, LANES), lambda b, i: (b * nq + i, 0)),
                  pl.BlockSpec((seq, LANES), lambda b, i: (b, 0)),
                  pl.BlockSpec((seq, LANES), lambda b, i: (b, 0)),
                  pl.BlockSpec((tk, tk), lambda b, i: (0, 0))],
        out_specs=pl.BlockSpec((tq, 4 * LANES), lambda b, i: (b * nq + i, 0)),
        out_shape=jax.ShapeDtypeStruct((batch * seq, 4 * LANES), jnp.bfloat16),
        scratch_shapes=[pltpu.VMEM((nk, tq, tk), f32), pltpu.VMEM((nk, tq, tk), jnp.bfloat16),
                        pltpu.VMEM((tq, 1), f32),
                        pltpu.VMEM((nh, tq, 1), f32), pltpu.VMEM((nh, tq, 1), f32),
                        pltpu.VMEM((nh, tq, LANES), f32),
                        pltpu.VMEM((nh * tq, tk), f32), pltpu.VMEM((nh * tq, tk), f32)],
        compiler_params=pltpu.CompilerParams(dimension_semantics=("arbitrary", "arbitrary"),
                                             vmem_limit_bytes=VMEM_LIMIT),
        name="dsa_prompt",
    )(dqi, misc, dki_bf, misc_bf, upper)


def _merge_kernel(x_ref, g_ref, on_ref, of_ref, od_ref, wmg_ref, wbn_ref, wbf_ref, wbd_ref, wo_ref, out_ref):
    precise = wo_ref.dtype == jnp.float32
    cdt = jnp.float32 if precise else jnp.bfloat16
    dot = functools.partial(jnp.dot, preferred_element_type=jnp.float32,
                            precision=lax.Precision.HIGHEST if precise else None)
    x = x_ref[...]
    xn = _rms_rows(x, g_ref[...]).astype(cdt)
    y = None
    for br, (o_ref, wb_ref) in enumerate(((on_ref, wbn_ref), (of_ref, wbf_ref), (od_ref, wbd_ref))):
        gate = jax.nn.sigmoid(dot(xn, wmg_ref[:, br * D_MODEL:(br + 1) * D_MODEL]))
        term = gate * dot(o_ref[...], wb_ref[...])
        y = term if y is None else y + term
    out_ref[...] = x + dot(y.astype(cdt), wo_ref[...])


def _merge(x2d, norm_g, o_nsa, o_fox, o_dsa, w_mg, w_bn, w_bf, w_bd, w_o, *, tm):
    n = x2d.shape[0]
    row = lambda a: pl.BlockSpec((tm, a.shape[1]), lambda i: (i, 0))
    full = lambda a: pl.BlockSpec(a.shape, lambda i: (0,) * a.ndim)
    args = (x2d, norm_g, o_nsa, o_fox, o_dsa, w_mg, w_bn, w_bf, w_bd, w_o)
    return pl.pallas_call(
        _merge_kernel,
        grid=(n // tm,),
        in_specs=[row(x2d), full(norm_g), row(o_nsa), row(o_fox), row(o_dsa)] + [full(a) for a in args[5:]],
        out_specs=row(x2d),
        out_shape=jax.ShapeDtypeStruct(x2d.shape, x2d.dtype),
        compiler_params=pltpu.CompilerParams(dimension_semantics=("arbitrary",), vmem_limit_bytes=VMEM_LIMIT),
        name="merge",
    )(*args)


_R_GROUP = N_EXPERTS


def _route(logits):
    lane = lax.broadcasted_iota(jnp.int32, (1, LANES), 1)
    lanef = lane.astype(jnp.float32)
    is_grp = (lane >= _R_GROUP) & (lane < _R_GROUP + N_GROUPS)
    lg = jnp.where(is_grp, logits, -jnp.inf)
    gmax = jnp.max(lg, axis=-1, keepdims=True)
    grp = jnp.min(jnp.where(lg == gmax, lanef, float(LANES)), axis=-1, keepdims=True) - _R_GROUP
    g1 = 1.0 / jnp.sum(jnp.where(is_grp, jnp.exp(lg - gmax), 0.0), axis=-1, keepdims=True)
    in_grp = (lane < N_EXPERTS) & ((lane // EXPERTS_PER_GROUP).astype(jnp.float32) == grp)
    le = jnp.where(in_grp, logits, -jnp.inf)
    m1 = jnp.max(le, axis=-1, keepdims=True)
    i1 = jnp.min(jnp.where(le == m1, lanef, float(LANES)), axis=-1, keepdims=True)
    le2 = jnp.where(lanef == i1, -jnp.inf, le)
    m2 = jnp.max(le2, axis=-1, keepdims=True)
    i2 = jnp.min(jnp.where(le2 == m2, lanef, float(LANES)), axis=-1, keepdims=True)
    t = jnp.exp(m2 - m1)
    w1 = g1 * (1.0 / (1.0 + t))
    w2 = g1 * (t / (1.0 + t))
    return jnp.where(lanef == i1, w1, jnp.where(lanef == i2, w2, 0.0))


def _moe_kernel(x_ref, g_ref, wr_ref, br_ref, wg_ref, wu_ref, wd_ref, out_ref, xn_sc, comb_sc, acc_sc):
    e = pl.program_id(1)

    @pl.when(e == 0)
    def _():
        x = x_ref[...]
        xn = _rms_rows(x, g_ref[...])
        logits = jnp.dot(xn, wr_ref[...], preferred_element_type=jnp.float32,
                         precision=lax.Precision.HIGHEST) + br_ref[...]
        comb_sc[...] = _route(logits)
        xn_sc[...] = xn.astype(xn_sc.dtype)
        acc_sc[...] = x

    dot = functools.partial(jnp.dot, preferred_element_type=jnp.float32,
                            precision=lax.Precision.HIGHEST if wg_ref.dtype == jnp.float32 else None)
    xn = xn_sc[...]
    h = jax.nn.silu(dot(xn, wg_ref[...])) * dot(xn, wu_ref[...])
    y = dot(h.astype(xn.dtype), wd_ref[...])
    lane = lax.broadcasted_iota(jnp.int32, (1, LANES), 1)
    ce = jnp.sum(jnp.where(lane == e, comb_sc[...], 0.0), axis=-1, keepdims=True)
    acc_sc[...] += ce * y

    @pl.when(e == pl.num_programs(1) - 1)
    def _():
        out_ref[...] = acc_sc[...]


def _moe(x2d, norm_g, w_router, b_router, w_gate, w_up, w_down, *, tm):
    n = x2d.shape[0]
    f32 = jnp.float32
    return pl.pallas_call(
        _moe_kernel,
        grid=(n // tm, N_EXPERTS),
        in_specs=[pl.BlockSpec((tm, D_MODEL), lambda i, e: (i, 0)),
                  pl.BlockSpec((1, D_MODEL), lambda i, e: (0, 0)),
                  pl.BlockSpec((D_MODEL, LANES), lambda i, e: (0, 0)),
                  pl.BlockSpec((1, LANES), lambda i, e: (0, 0)),
                  pl.BlockSpec((None, D_MODEL, D_EXPERT), lambda i, e: (e, 0, 0)),
                  pl.BlockSpec((None, D_MODEL, D_EXPERT), lambda i, e: (e, 0, 0)),
                  pl.BlockSpec((None, D_EXPERT, D_MODEL), lambda i, e: (e, 0, 0))],
        out_specs=pl.BlockSpec((tm, D_MODEL), lambda i, e: (i, 0)),
        out_shape=jax.ShapeDtypeStruct(x2d.shape, x2d.dtype),
        scratch_shapes=[pltpu.VMEM((tm, D_MODEL), w_gate.dtype), pltpu.VMEM((tm, LANES), f32),
                        pltpu.VMEM((tm, D_MODEL), f32)],
        compiler_params=pltpu.CompilerParams(dimension_semantics=("arbitrary", "arbitrary"),
                                             vmem_limit_bytes=VMEM_LIMIT),
        name="moe",
    )(x2d, norm_g, w_router, b_router, w_gate, w_up, w_down)


def _seq_spec(shape):
    nd = len(shape)
    return pl.BlockSpec((None,) + tuple(shape[1:]), lambda b, s, pt: (b,) + (0,) * (nd - 1))


def _token0_page(col, dtype):
    first = lax.broadcasted_iota(jnp.int32, (1, PAGE_SIZE), 1) == 0
    return jnp.where(first, col, 0.0).astype(dtype)


def _hi_lo(x):
    hi = x.astype(jnp.bfloat16)
    return hi, (x - hi.astype(jnp.float32)).astype(jnp.bfloat16)


def _scores(q, keys_sc):
    return jnp.dot(q, keys_sc[...], preferred_element_type=jnp.float32)


def _weighted_values(p, values_sc):
    return lax.dot_general(p.astype(jnp.bfloat16), values_sc[...], _NT, preferred_element_type=jnp.float32)


def _split_dot_nt(a, b_bf16):
    hi = a.astype(jnp.bfloat16)
    lo = (a - hi.astype(jnp.float32)).astype(jnp.bfloat16)
    return (lax.dot_general(hi, b_bf16, _NT, preferred_element_type=jnp.float32)
            + lax.dot_general(lo, b_bf16, _NT, preferred_element_type=jnp.float32))


def _nsa_decode_kernel(pt_ref, qbd_ref, gate_ref, new_ref, neww_ref, cw_ref, *rest, pg, n_pages):
    page_refs = rest[:pg]
    o_ref, kc_sc, kcl_sc, vc_sc, ks_sc, vs_sc = rest[pg:]
    s = pl.program_id(1)
    n_tok = n_pages * PAGE_SIZE
    l_pad = n_tok + PAGE_SIZE
    nb_s = n_tok // NSA_BLOCK + 1
    nbpad = -(-nb_s // 8) * 8
    bf16 = jnp.bfloat16
    nh = NSA_HEADS

    for k in range(pg):
        page = page_refs[k][...]
        cols = slice(k * PAGE_SIZE, (k + 1) * PAGE_SIZE)
        kc_sc[:, cols], kcl_sc[:, cols] = _hi_lo(page[0 * LANES:1 * LANES])
        vc_sc[:, cols] = page[1 * LANES:2 * LANES].astype(bf16)
        ks_sc[:, cols] = page[2 * LANES:3 * LANES].astype(bf16)
        vs_sc[:, cols] = page[3 * LANES:4 * LANES].astype(bf16)

    @pl.when(s == pl.num_programs(1) - 1)
    def _():
        new = new_ref[...]
        kc_sc[:, n_tok:l_pad], kcl_sc[:, n_tok:l_pad] = _hi_lo(_token0_page(new[0 * LANES:1 * LANES], jnp.float32))
        vc_sc[:, n_tok:l_pad] = _token0_page(new[1 * LANES:2 * LANES], bf16)
        ks_sc[:, n_tok:l_pad] = _token0_page(new[2 * LANES:3 * LANES], bf16)
        vs_sc[:, n_tok:l_pad] = _token0_page(new[3 * LANES:4 * LANES], bf16)
        q_hl = qbd_ref[...]
        qbd = q_hl[0:nh]
        qpos = n_tok
        kpos = lax.broadcasted_iota(jnp.int32, (1, l_pad), 1)
        blk = lax.broadcasted_iota(jnp.int32, (1, nbpad), 1)
        incid = jnp.where(lax.broadcasted_iota(jnp.int32, (nbpad, 1), 0) == kpos // NSA_BLOCK, 1.0, 0.0).astype(bf16)
        inv = 1.0 / NSA_BLOCK

        s_hl = _scores(q_hl, kc_sc)
        s_tok = s_hl[0:nh] + s_hl[nh:2 * nh] + _scores(qbd, kcl_sc)
        s_c = _split_dot_nt(s_tok, incid) * inv
        done = ((blk + 1) * NSA_BLOCK <= qpos + 1) & (blk < nb_s)
        p_c = _masked_softmax(s_c, done)
        p_tok = jnp.dot(p_c.astype(bf16), incid, preferred_element_type=jnp.float32) * inv
        o_c = _weighted_values(p_tok, vc_sc)

        imp = jnp.sum(p_c.reshape(NSA_KV_HEADS, NSA_GROUP, nbpad), axis=1)
        score = _block_importance_scores(imp, jnp.full((NSA_KV_HEADS, 1), qpos, jnp.int32), nbpad)
        score = jnp.where(blk < nb_s, score, -2.0)
        sel = _select_top_blocks(score, min(NSA_TOP_BLOCKS, nb_s))
        sel8 = jnp.concatenate([jnp.broadcast_to(sel[g:g + 1], (NSA_GROUP, nbpad)) for g in range(NSA_KV_HEADS)],
                               axis=0).astype(bf16)
        chosen = jnp.dot(sel8, incid, preferred_element_type=jnp.float32) > 0.5
        p_s = _masked_softmax(_scores(qbd, ks_sc), chosen & (kpos <= qpos))
        o_s = _weighted_values(p_s, vs_sc)

        cw = cw_ref[...].astype(bf16)
        nw = neww_ref[...]
        win_keep = cw.shape[1]
        s_w = jnp.concatenate([jnp.dot(qbd, cw[0:LANES], preferred_element_type=jnp.float32),
                               jnp.dot(qbd, _token0_page(nw[0:LANES], bf16), preferred_element_type=jnp.float32)],
                              axis=1)
        wpos = lax.broadcasted_iota(jnp.int32, (1, win_keep + PAGE_SIZE), 1)
        p_w = _masked_softmax(s_w, wpos <= win_keep).astype(bf16)
        o_w = (lax.dot_general(p_w[:, :win_keep], cw[LANES:2 * LANES], _NT, preferred_element_type=jnp.float32)
               + lax.dot_general(p_w[:, win_keep:], _token0_page(nw[LANES:2 * LANES], bf16), _NT,
                                 preferred_element_type=jnp.float32))

        g = gate_ref[...]
        o_ref[...] = g[:, 0:1] * o_c + g[:, 1:2] * o_s + g[:, 2:3] * o_w


def _page_specs_t(layer, pg, rows):
    return [pl.BlockSpec((None, None, rows, PAGE_SIZE),
                         functools.partial(lambda b, s, pt, k: (layer, pt[b, s * pg + k], 0, 0), k=k))
            for k in range(pg)]


def _decode_call(kernel_fn, name, page_table, fixed, const, caches, layer, out_tail, scratch, *, pg):
    bs, n_pages = page_table.shape
    assert pg == n_pages, "the sample kernels place each page at a static offset of the sequence buffers"
    const_specs = [pl.BlockSpec(a.shape, functools.partial(lambda b, s, pt, nd: (0,) * nd, nd=a.ndim)) for a in const]
    page_specs = []
    for cache in caches:
        page_specs += _page_specs_t(layer, pg, cache.shape[2])
    grid_spec = pltpu.PrefetchScalarGridSpec(
        num_scalar_prefetch=1,
        grid=(bs, n_pages // pg),
        in_specs=[_seq_spec(a.shape) for a in fixed] + const_specs + page_specs,
        out_specs=_seq_spec((bs,) + out_tail),
        scratch_shapes=scratch)
    operands = list(fixed) + list(const)
    for cache in caches:
        operands += [cache] * pg
    return pl.pallas_call(
        kernel_fn,
        grid_spec=grid_spec,
        out_shape=jax.ShapeDtypeStruct((bs,) + out_tail, jnp.float32),
        compiler_params=pltpu.CompilerParams(dimension_semantics=("arbitrary", "arbitrary"),
                                             vmem_limit_bytes=VMEM_LIMIT),
        name=name,
    )(page_table, *operands)


def _nsa_decode(page_table, qbd, gates, new_t, neww_t, cache_win_t, cache_t, layer, *, pg):
    n_pages = page_table.shape[1]
    buf = pltpu.VMEM((LANES, (n_pages + 1) * PAGE_SIZE), jnp.bfloat16)
    return _decode_call(functools.partial(_nsa_decode_kernel, pg=pg, n_pages=n_pages), "nsa_sample", page_table,
                        (qbd, gates, new_t, neww_t, cache_win_t), (), (cache_t,), layer, (NSA_HEADS, LANES),
                        [buf, buf, buf, buf, buf], pg=pg)


def _fox_decode_kernel(pt_ref, qbd_ref, new_ref, newlf_ref, *rest, pg, n_pages):
    page_refs = rest[:pg]
    lf_refs = rest[pg:2 * pg]
    o_ref, k_sc, v_sc, lf_sc, cum_sc = rest[2 * pg:]
    s = pl.program_id(1)
    n_tok = n_pages * PAGE_SIZE
    l_pad = n_tok + PAGE_SIZE
    bf16 = jnp.bfloat16
    hi = lax.Precision.HIGHEST
    w = FOX_HEADS * HEAD_DIM

    for k in range(pg):
        page = page_refs[k][...]
        cols = slice(k * PAGE_SIZE, (k + 1) * PAGE_SIZE)
        k_sc[:, cols] = page[0:w].astype(bf16)
        v_sc[:, cols] = page[w:2 * w].astype(bf16)
        lf = lf_refs[k][...]
        for h in range(FOX_HEADS):
            lf_sc[h, k:k + 1, :] = lf[h:h + 1, :]

    @pl.when(s == pl.num_programs(1) - 1)
    def _():
        new = new_ref[...]
        k_sc[:, n_tok:l_pad] = _token0_page(new[0:w], bf16)
        v_sc[:, n_tok:l_pad] = _token0_page(new[w:2 * w], bf16)
        iota2 = lambda n, axis: lax.broadcasted_iota(jnp.int32, (n, n), axis)
        upper_incl = jnp.where(iota2(PAGE_SIZE, 0) <= iota2(PAGE_SIZE, 1), 1.0, 0.0)
        lower_strict = jnp.where(iota2(n_pages, 1) < iota2(n_pages, 0), 1.0, 0.0)
        cqs = []
        for h in range(FOX_HEADS):
            within = jnp.dot(lf_sc[h], upper_incl, preferred_element_type=jnp.float32, precision=hi)
            before = jnp.dot(lower_strict, within, preferred_element_type=jnp.float32, precision=hi)
            cum = within + before[:, PAGE_SIZE - 1:PAGE_SIZE]
            cum_sc[h] = cum
            cqs.append(cum[n_pages - 1:n_pages, PAGE_SIZE - 1:PAGE_SIZE] + newlf_ref[:, h:h + 1])
        cq = jnp.concatenate(cqs, axis=0)
        qbd = qbd_ref[...]
        decay = []
        for p in range(n_pages):
            ck = jnp.concatenate([cum_sc[h, p:p + 1, :] for h in range(FOX_HEADS)], axis=0)
            decay.append(cq - ck)
        decay.append(jnp.zeros((FOX_HEADS, PAGE_SIZE), jnp.float32))
        sc = _scores(qbd, k_sc) + jnp.concatenate(decay, axis=1)
        kpos = lax.broadcasted_iota(jnp.int32, (1, l_pad), 1)
        o_ref[...] = _weighted_values(_masked_softmax(sc, kpos <= n_tok), v_sc)


def _fox_decode(page_table, qbd, new_t, new_lf, cache_t, cache_lf_t, layer, *, pg):
    n_pages = page_table.shape[1]
    w = FOX_HEADS * HEAD_DIM
    buf = pltpu.VMEM((w, (n_pages + 1) * PAGE_SIZE), jnp.bfloat16)
    lfbuf = pltpu.VMEM((FOX_HEADS, n_pages, PAGE_SIZE), jnp.float32)
    return _decode_call(functools.partial(_fox_decode_kernel, pg=pg, n_pages=n_pages), "fox_sample", page_table,
                        (qbd, new_t, new_lf), (), (cache_t, cache_lf_t), layer, (FOX_HEADS, w),
                        [buf, buf, lfbuf, lfbuf], pg=pg)


def _dsa_decode_kernel(pt_ref, qatt_ref, qidx_ref, iw_ref, new_ref, *rest, pg, n_pages, n_keep):
    page_refs = rest[:pg]
    o_ref, kv_sc, ik_sc, ikl_sc = rest[pg:]
    s = pl.program_id(1)
    n_tok = n_pages * PAGE_SIZE
    l_pad = n_tok + PAGE_SIZE
    bf16 = jnp.bfloat16
    hd = HEAD_DIM
    nh = DSA_IDX_HEADS

    for k in range(pg):
        page = page_refs[k][...]
        cols = slice(k * PAGE_SIZE, (k + 1) * PAGE_SIZE)
        kv_sc[:, cols] = page[0:2 * hd].astype(bf16)
        ik_sc[:, cols], ikl_sc[:, cols] = _hi_lo(page[2 * hd:3 * hd])

    @pl.when(s == pl.num_programs(1) - 1)
    def _():
        new = new_ref[...]
        kv_sc[:, n_tok:l_pad] = _token0_page(new[0:2 * hd], bf16)
        ik_sc[:, n_tok:l_pad], ikl_sc[:, n_tok:l_pad] = _hi_lo(_token0_page(new[2 * hd:3 * hd], jnp.float32))
        kpos = lax.broadcasted_iota(jnp.int32, (1, l_pad), 1)
        causal = kpos <= n_tok
        q_hl = qidx_ref[...]
        a_hl = _scores(q_hl, ik_sc)
        a = a_hl[0:nh] + a_hl[nh:2 * nh] + _scores(q_hl[0:nh], ikl_sc)
        a = jnp.maximum(a, 0.0) * iw_ref[...]
        sc = a[0:1]
        for h in range(1, DSA_IDX_HEADS):
            sc = sc + a[h:h + 1]
        sc = jnp.where(sc == 0.0, 0.0, sc)
        key = jnp.where(causal, sc, -jnp.inf)

        def count(mask):
            return jnp.sum(jnp.where(mask, 1.0, 0.0), axis=-1, keepdims=True)

        def bit_body(it, lo):
            cand = lo + lax.shift_left(jnp.int32(1), jnp.int32(31) - it)
            return jnp.where(count(key >= _key_value(cand)) >= n_keep, cand, lo)

        thr = _key_value(lax.fori_loop(0, 32, bit_body, jnp.full((1, 1), _INT_MIN, jnp.int32)))
        need = n_keep - count(key > thr)
        tie = jnp.where(key == thr, kpos, jnp.int32(2 ** 30))
        idx_bits = max(1, (l_pad - 1).bit_length())

        def idx_body(it, bound):
            step = lax.shift_left(jnp.int32(1), jnp.int32(idx_bits - 1) - it)
            return jnp.where(count(tie <= bound + step - 1) < need, bound + step, bound)

        bound = lax.fori_loop(0, idx_bits, idx_body, jnp.zeros((1, 1), jnp.int32))
        valid = ((key > thr) | (tie <= bound)) & causal
        pr = _masked_softmax(_scores(qatt_ref[...], kv_sc), valid)
        o_ref[...] = _weighted_values(pr, kv_sc)


def _dsa_decode(page_table, q_att, q_idx, iw_col, new_t, cache_t, layer, *, pg, n_keep):
    n_pages = page_table.shape[1]
    hd = HEAD_DIM
    return _decode_call(functools.partial(_dsa_decode_kernel, pg=pg, n_pages=n_pages, n_keep=n_keep), "dsa_sample",
                        page_table, (q_att, q_idx, iw_col, new_t), (), (cache_t,), layer, (DSA_HEADS, LANES),
                        [pltpu.VMEM((2 * hd, (n_pages + 1) * PAGE_SIZE), jnp.bfloat16),
                         pltpu.VMEM((hd, (n_pages + 1) * PAGE_SIZE), jnp.bfloat16),
                         pltpu.VMEM((hd, (n_pages + 1) * PAGE_SIZE), jnp.bfloat16)], pg=pg)


def _merge_weights(w_bn, w_bf, w_bd, w_o):
    hd = HEAD_DIM
    rows = []
    for c in range(NSA_GROUP):
        rows += list(range(hd * c, hd * c + hd)) + list(range(hd * (c + NSA_GROUP), hd * (c + NSA_GROUP) + hd))
    w_bn_p = jnp.take(w_bn, np.asarray(rows, np.int32), axis=0)
    w_bd_p = jnp.pad(w_bd.reshape(DSA_HEADS, hd, D_MODEL), ((0, 0), (0, hd), (0, 0)))
    w_bd_p = w_bd_p.reshape(DSA_HEADS * LANES, D_MODEL)
    return w_bn_p, w_bf, w_bd_p, w_o


def _router_weights(w_rg, b_rg, w_re, b_re):
    pad = LANES - N_EXPERTS - N_GROUPS
    w = jnp.concatenate([w_re, w_rg, jnp.zeros((D_MODEL, pad), w_re.dtype)], axis=1).astype(jnp.float32)
    b = jnp.concatenate([b_re, b_rg, jnp.zeros((pad,), b_re.dtype)]).astype(jnp.float32)[None, :]
    return w, b


TM_PROJ = 512
TQ_FOX = 512
TQ_NSA = 256
TQ_DSA = 256
TK_ATTN = 1024
TM_MERGE = 512
TM_MOE = 1024


def _hi_lo_rows(q):
    hi = q.astype(jnp.bfloat16)
    lo = (q - hi.astype(jnp.float32)).astype(jnp.bfloat16)
    return jnp.concatenate([hi, lo], axis=1)


def _dsa_rows(dki, misc):
    return jnp.concatenate([dki[:, :HEAD_DIM], misc[:, :HEAD_DIM], dki[:, HEAD_DIM:]], axis=-1)


def kernel(x_prompt, x_sample, cache_nsa, cache_fox, cache_fox_logf, cache_dsa, cache_win, page_table, norm_attn_g, w_in, nsa_qk_g, fox_qk_g, fox_b_f, dsa_qk_g, w_branch_nsa, w_branch_fox, w_branch_dsa, w_out, norm_ffn_g, w_router_group, b_router_group, w_router_expert, b_router_expert, w_exp_gate, w_exp_up, w_exp_down):
    depth = w_in.shape[0]
    B, S, D = x_prompt.shape
    Bs, T, _ = x_sample.shape
    assert T == 1 and D == D_MODEL
    n_pages = page_table.shape[1]
    past_len = n_pages * PAGE_SIZE
    win_keep = cache_win.shape[2]
    n_phys = cache_nsa.shape[1]
    bf16 = jnp.bfloat16
    hd = HEAD_DIM

    tm_proj = min(TM_PROJ, S)
    cos_p, sin_p = _rope_tables(jnp.arange(S))
    cos_s, sin_s = _rope_tables(jnp.full((Bs,), past_len))
    pg = n_pages
    c_nsa = jnp.moveaxis(cache_nsa.reshape(depth, n_phys, PAGE_SIZE, 4 * LANES), 2, 3)
    c_fox = jnp.moveaxis(cache_fox.reshape(depth, n_phys, PAGE_SIZE, 2 * FOX_HEADS * hd), 2, 3)
    c_lf = jnp.moveaxis(cache_fox_logf.astype(jnp.float32), 2, 3)
    c_dsa = jnp.moveaxis(cache_dsa.reshape(depth, n_phys, PAGE_SIZE, 3 * hd), 2, 3)
    c_win = jnp.moveaxis(cache_win.reshape(depth, Bs, win_keep, 2 * LANES), 2, 3)
    lane = jnp.arange(LANES)
    left = lane < hd

    xp = x_prompt.reshape(B * S, D)
    xs = x_sample.reshape(Bs, D)
    outs = [[] for _ in range(10)]
    for l in range(depth):
        w_perm, w_mg, gain_row, bf_row = _layer_tables(w_in[l], nsa_qk_g[l], fox_qk_g[l], fox_b_f[l], dsa_qk_g[l])
        w_m = (w_mg,) + _merge_weights(w_branch_nsa[l], w_branch_fox[l], w_branch_dsa[l], w_out[l])
        w_m_bf = tuple(w.astype(bf16) for w in w_m)
        w_r, b_r = _router_weights(w_router_group[l], b_router_group[l], w_router_expert[l], b_router_expert[l])
        w_e = (w_exp_gate[l].astype(bf16), w_exp_up[l].astype(bf16), w_exp_down[l].astype(bf16))
        g_attn = norm_attn_g[l][None, :]
        g_ffn = norm_ffn_g[l][None, :]

        (qn, nsa_rows, win_rows, slc_kv, win_kv, cmp, qf, fox_rows, fox_kv, dqi, dki, dki_bf, misc, misc_bf,
         cum) = _project(xp, g_attn, w_perm.astype(bf16), gain_row, cos_p, sin_p, bf_row,
                         tm=tm_proj, tiles_per_batch=S // tm_proj, with_blocks=True)
        cum_t = cum[:, _M_FF:_M_IW].reshape(B, S, FOX_HEADS).transpose(0, 2, 1)
        o_fox = _fox_prompt(qf, fox_kv, cum, cum_t, batch=B, seq=S, tq=min(TQ_FOX, S), tk=min(TK_ATTN, S))
        o_nsa = _nsa_prompt(qn, misc, cmp.astype(bf16), slc_kv, win_kv, batch=B, seq=S, tq=min(TQ_NSA, S),
                            tk=min(TK_ATTN, S))
        o_dsa = _dsa_prompt(dqi, misc, dki_bf, misc_bf, batch=B, seq=S, tq=min(TQ_DSA, S), tk=min(TK_ATTN, S),
                            n_keep=min(DSA_TOPK, S // 4))
        xp = _merge(xp, g_attn, o_nsa, o_fox, o_dsa, *w_m_bf, tm=min(TM_MERGE, B * S))
        xp = _moe(xp, g_ffn, w_r, b_r, *w_e, tm=min(TM_MOE, B * S))
        outs[0].append(nsa_rows.reshape(B, S, 4, NSA_KV_HEADS, hd))
        outs[2].append(fox_rows.reshape(B, S, 2, FOX_HEADS, hd))
        outs[4].append(misc[:, _M_FF:_M_IW].reshape(B, S, FOX_HEADS))
        outs[6].append(_dsa_rows(dki, misc).reshape(B, S, 3, hd))
        outs[8].append(win_rows.reshape(B, S, 2, NSA_KV_HEADS, hd)[:, S - min(NSA_WINDOW, S):])

        (qn, nsa_rows, win_rows, _, _, _, qf, fox_rows, _, dqi, dki, _, misc, _, _) = _project(
            xs, g_attn, w_perm, gain_row, cos_s, sin_s, bf_row, tm=Bs, tiles_per_batch=1, with_blocks=False)
        chunks = qn.reshape(Bs, NSA_GROUP, LANES)
        qbd = _hi_lo_rows(jnp.concatenate([jnp.where(left, chunks, 0), jnp.where(left, 0, chunks)], axis=1))
        gates = misc[:, _M_GATE:_M_FF].reshape(Bs, 3, NSA_HEADS).transpose(0, 2, 1)
        gates = jnp.pad(gates, ((0, 0), (0, 0), (0, LANES - 3)))
        o = _nsa_decode(page_table, qbd, gates, nsa_rows[:, :, None], win_rows[:, :, None], c_win[l], c_nsa, l, pg=pg)
        o_nsa = jnp.where(left, o[:, :NSA_GROUP], o[:, NSA_GROUP:]).reshape(Bs, 4 * LANES)

        head_of_lane = jnp.arange(FOX_HEADS * hd) // hd
        qbd_f = jnp.where(head_of_lane[None, None, :] == jnp.arange(FOX_HEADS)[None, :, None], qf[:, None, :], 0)
        qbd_f = qbd_f.astype(bf16)
        o = _fox_decode(page_table, qbd_f, fox_rows[:, :, None], misc[:, None, _M_FF:_M_IW], c_fox, c_lf, l, pg=pg)
        o_fox = jnp.einsum('bhhd->bhd', o.reshape(Bs, FOX_HEADS, FOX_HEADS, hd)).reshape(Bs, FOX_HEADS * hd)

        chunks = dqi.reshape(Bs, DSA_HEADS, LANES)
        q_att = jnp.where(left, chunks, 0).astype(bf16)
        q_idx = _hi_lo_rows(chunks[..., hd:])
        dsa_new = _dsa_rows(dki, misc)
        o = _dsa_decode(page_table, q_att, q_idx, misc[:, _M_IW:_M_END, None], dsa_new[:, :, None], c_dsa, l,
                        pg=pg, n_keep=min(DSA_TOPK, (past_len + 1) // 4))
        o_dsa = jnp.concatenate([o[..., hd:], jnp.zeros_like(o[..., hd:])], axis=-1).reshape(Bs, 4 * LANES)

        xs = _merge(xs, g_attn, o_nsa, o_fox, o_dsa, *w_m, tm=Bs)
        xs = _moe(xs, g_ffn, w_r, b_r, w_exp_gate[l], w_exp_up[l], w_exp_down[l], tm=Bs)
        outs[1].append(nsa_rows.reshape(Bs, 1, 4, NSA_KV_HEADS, hd))
        outs[3].append(fox_rows.reshape(Bs, 1, 2, FOX_HEADS, hd))
        outs[5].append(misc[:, _M_FF:_M_IW].reshape(Bs, 1, FOX_HEADS))
        outs[7].append(dsa_new.reshape(Bs, 1, 3, hd))
        win_all = jnp.concatenate([cache_win[l], win_rows.reshape(Bs, 1, 2, NSA_KV_HEADS, hd)], axis=1)
        outs[9].append(win_all[:, 1:])
    return (xp.reshape(B, S, D), xs.reshape(Bs, 1, D)) + tuple(jnp.stack(o) for o in outs)
```
